```python
import jax, jax.numpy as jnp
from jax import lax
import numpy as np

D_MODEL = 2048
BATCH = 4
SEQ = 4096
DEPTH = 1

CHUNK = 64
N_LEFT_CHUNKS = 8
N_BAND = N_LEFT_CHUNKS + 1
A_HEADS = 8
A_WIDTH = D_MODEL // 2
A_HEAD_DIM = A_WIDTH // A_HEADS
REL_CLIP = 256
POOL_WINDOWS = (2, 4, 8, 16)
POOL_GROUPS = len(POOL_WINDOWS)
POOL_WIDTH = D_MODEL // 4
POOL_GROUP_DIM = POOL_WIDTH // POOL_GROUPS
M_HEADS = 4
M_WIDTH = D_MODEL // 4
M_HEAD_DIM = M_WIDTH // M_HEADS
N_MEM = 256
N_BRANCH = 3
IN_SPLITS = (A_WIDTH, A_WIDTH, A_WIDTH, A_WIDTH, POOL_WIDTH, POOL_WIDTH, M_WIDTH, M_WIDTH)
IN_WIDTH = sum(IN_SPLITS)
EPS = 1e-6
NEG_INF = -1e30

kernel_name = "hybrid_chunk_attn_pool_mem_block"


def _rmsnorm(t, gain):
    tf = t.astype(jnp.float32)
    tf = tf * lax.rsqrt(jnp.mean(tf * tf, axis=-1, keepdims=True) + EPS)
    return (tf * gain.astype(jnp.float32)).astype(t.dtype)


def _head_rmsnorm(t, gain):
    tf = t.astype(jnp.float32)
    tf = tf * lax.rsqrt(jnp.mean(tf * tf, axis=-1, keepdims=True) + EPS)
    return (tf * gain.astype(jnp.float32)).astype(t.dtype)


def _chunk_band_attention(q, k, v, q_gain, k_gain, rel_bias):
    B, S, H, Dh = q.shape
    nc = S // CHUNK
    q = _head_rmsnorm(q, q_gain).reshape(B, nc, CHUNK, H, Dh)
    k = _head_rmsnorm(k, k_gain)
    pad = jnp.zeros((B, N_LEFT_CHUNKS * CHUNK, H, Dh), k.dtype)
    kp = jnp.concatenate([pad, k], axis=1).reshape(B, nc + N_LEFT_CHUNKS, CHUNK, H, Dh)
    vp = jnp.concatenate([pad.astype(v.dtype), v], axis=1).reshape(B, nc + N_LEFT_CHUNKS, CHUNK, H, Dh)
    band_idx = np.arange(nc)[:, None] + np.arange(N_BAND)[None, :]
    kb = kp[:, band_idx].reshape(B, nc, N_BAND * CHUNK, H, Dh)
    vb = vp[:, band_idx].reshape(B, nc, N_BAND * CHUNK, H, Dh)
    scores = jnp.einsum('bnqhd,bnkhd->bnhqk', q, kb).astype(jnp.float32) * (Dh ** -0.5)
    i = np.arange(CHUNK)
    s_off = np.arange(N_BAND)
    dist = ((N_LEFT_CHUNKS - s_off)[None, :, None] * CHUNK
            + i[:, None, None] - i[None, None, :]).reshape(CHUNK, N_BAND * CHUNK)
    bias_idx = np.clip(dist, -REL_CLIP, REL_CLIP) + REL_CLIP
    bias = rel_bias.astype(jnp.float32)[:, bias_idx]
    valid = (np.arange(nc)[:, None] + s_off[None, :] - N_LEFT_CHUNKS) >= 0
    valid = np.repeat(valid, CHUNK, axis=1)
    scores = scores + bias[None, None]
    scores = jnp.where(valid[None, :, None, None, :], scores, NEG_INF)
    probs = jax.nn.softmax(scores, axis=-1).astype(v.dtype)
    out = jnp.einsum('bnhqk,bnkhd->bnqhd', probs, vb)
    return out.reshape(B, S, H * Dh)


def _multiscale_pool(v, pool_w, pool_scale):
    B, S, P = v.shape
    vg = v.reshape(B, S, POOL_GROUPS, POOL_GROUP_DIM).astype(jnp.float32)
    cs = jnp.concatenate([jnp.zeros((B, 1, POOL_GROUPS, POOL_GROUP_DIM), jnp.float32),
                          jnp.cumsum(vg, axis=1)], axis=1)
    t = np.arange(S)
    outs = []
    for g, w in enumerate(POOL_WINDOWS):
        lo = np.maximum(t + 1 - w, 0)
        cnt = (t + 1 - lo).astype(np.float32)
        mean = (cs[:, 1:, g] - cs[:, lo, g]) / cnt[None, :, None]
        outs.append(mean - vg[:, :, g])
    pooled = jnp.stack(outs, axis=2).astype(v.dtype)
    mixed = jnp.einsum('bsgc,gcd->bsgd', pooled, pool_w).reshape(B, S, P)
    return mixed * pool_scale


def _memory_attention(q, mk, mv, q_gain, k_gain):
    B, S, H, Dh = q.shape
    q = _head_rmsnorm(q, q_gain)
    mk = _head_rmsnorm(mk, k_gain)
    scores = jnp.einsum('bshd,bnhd->bhsn', q, mk).astype(jnp.float32) * (Dh ** -0.5)
    probs = jax.nn.softmax(scores, axis=-1).astype(mv.dtype)
    out = jnp.einsum('bhsn,bnhd->bshd', probs, mv)
    return out.reshape(B, S, H * Dh)


def setup_inputs(seed: int = 0) -> dict:
    key = jax.random.key(seed)
    ks = jax.random.split(key, 20)
    L, D = DEPTH, D_MODEL
    nrm = jax.random.normal
    f32 = jnp.float32
    return {
        "x": nrm(ks[0], (BATCH, SEQ, D), f32),
        "mem": nrm(ks[1], (BATCH, N_MEM, D), f32),
        "norm_gain": 1.0 + 0.1 * nrm(ks[2], (L, D), f32),
        "mem_norm_gain": 1.0 + 0.1 * nrm(ks[3], (L, D), f32),
        "w_in": nrm(ks[4], (L, D, IN_WIDTH), f32) * D ** -0.5,
        "w_merge": nrm(ks[5], (L, D, N_BRANCH * D), f32) * D ** -0.5,
        "b_merge": 0.1 * nrm(ks[6], (L, N_BRANCH * D), f32),
        "a_q_gain": 1.0 + 0.1 * nrm(ks[7], (L, A_HEADS, A_HEAD_DIM), f32),
        "a_k_gain": 1.0 + 0.1 * nrm(ks[8], (L, A_HEADS, A_HEAD_DIM), f32),
        "a_rel_bias": 0.5 * nrm(ks[9], (L, A_HEADS, 2 * REL_CLIP + 1), f32),
        "pool_w": nrm(ks[10], (L, POOL_GROUPS, POOL_GROUP_DIM, POOL_GROUP_DIM), f32) * POOL_GROUP_DIM ** -0.5,
        "pool_scale": 1.0 + 0.1 * nrm(ks[11], (L, POOL_WIDTH), f32),
        "w_mem_kv": nrm(ks[12], (L, D, 2 * M_WIDTH), f32) * D ** -0.5,
        "m_q_gain": 1.0 + 0.1 * nrm(ks[13], (L, M_HEADS, M_HEAD_DIM), f32),
        "m_k_gain": 1.0 + 0.1 * nrm(ks[14], (L, M_HEADS, M_HEAD_DIM), f32),
        "w_branch_a": nrm(ks[15], (L, A_WIDTH, D), f32) * A_WIDTH ** -0.5,
        "w_branch_b": nrm(ks[16], (L, POOL_WIDTH, D), f32) * POOL_WIDTH ** -0.5,
        "w_branch_m": nrm(ks[17], (L, M_WIDTH, D), f32) * M_WIDTH ** -0.5,
        "w_out": nrm(ks[18], (L, D, D), f32) * D ** -0.5,
    }


def reference(x, mem, norm_gain, mem_norm_gain, w_in, w_merge, b_merge, a_q_gain, a_k_gain,
              a_rel_bias, pool_w, pool_scale, w_mem_kv, m_q_gain, m_k_gain,
              w_branch_a, w_branch_b, w_branch_m, w_out):
    B, S, D = x.shape
    offsets = [int(o) for o in np.cumsum(IN_SPLITS)[:-1]]
    for l in range(DEPTH):
        h = _rmsnorm(x, norm_gain[l])
        proj = h @ w_in[l]
        qa, ka, va, ga, vb, gb, qm, gm = jnp.split(proj, offsets, axis=-1)
        gates = jax.nn.sigmoid(h @ w_merge[l] + b_merge[l]).reshape(B, S, N_BRANCH, D)

        o_a = _chunk_band_attention(qa.reshape(B, S, A_HEADS, A_HEAD_DIM),
                                    ka.reshape(B, S, A_HEADS, A_HEAD_DIM),
                                    va.reshape(B, S, A_HEADS, A_HEAD_DIM),
                                    a_q_gain[l], a_k_gain[l], a_rel_bias[l])
        o_a = o_a * jax.nn.silu(ga)

        o_b = _multiscale_pool(vb, pool_w[l], pool_scale[l]) * jax.nn.silu(gb)

        mem_h = _rmsnorm(mem, mem_norm_gain[l])
        mk, mv = jnp.split(mem_h @ w_mem_kv[l], 2, axis=-1)
        n_mem = mem.shape[1]
        o_m = _memory_attention(qm.reshape(B, S, M_HEADS, M_HEAD_DIM),
                                mk.reshape(B, n_mem, M_HEADS, M_HEAD_DIM),
                                mv.reshape(B, n_mem, M_HEADS, M_HEAD_DIM),
                                m_q_gain[l], m_k_gain[l])
        o_m = o_m * jax.nn.silu(gm)

        y = (gates[:, :, 0] * (o_a @ w_branch_a[l])
             + gates[:, :, 1] * (o_b @ w_branch_b[l])
             + gates[:, :, 2] * (o_m @ w_branch_m[l]))
        x = x + y @ w_out[l]
    return x
```

```python
import functools

import jax
import jax.numpy as jnp
from jax import lax
from jax.experimental import pallas as pl
from jax.experimental.pallas import tpu as pltpu

F32 = jnp.float32
BF16 = jnp.bfloat16

CHUNK = 64
N_LEFT_CHUNKS = 8
A_HEADS = 8
M_HEADS = 4
HEAD_DIM = 128
REL_CLIP = 256
POOL_WINDOWS = (2, 4, 8, 16)
EPS = 1e-6
NEG_INF = -1e30

LANES_V7X = 128
VMEM_LIMIT_V7X = 60000 * 1024

TM_IN = 1024
TN_IN = 512
TQ = 128
N_KT = 5
TQ_MEM = 256
TR_POOL = 512
POOL_HALO = 16
TM_OUT = 512
CN_OUT = 512

G_QA, G_KA, G_VA, G_GA, G_VB, G_GB, G_QM, G_GM = 0, 8, 16, 24, 32, 36, 40, 44
N_GROUPS = 48


def _params(sem):
    return pltpu.CompilerParams(dimension_semantics=sem, vmem_limit_bytes=VMEM_LIMIT_V7X)


def _rms_rows(t):
    return t * lax.rsqrt(jnp.mean(t * t, axis=-1, keepdims=True) + EPS)


def _sigmoid(z):
    return 1.0 / (1.0 + jnp.exp(-z))


def _in_proj_kernel(x_ref, ng_ref, w_ref, gain_ref, p_ref, vb_ref, h_scr):
    j = pl.program_id(2)

    @pl.when(j == 0)
    def _():
        h_scr[...] = (_rms_rows(x_ref[0]) * ng_ref[...]).astype(BF16)

    acc = jnp.dot(h_scr[...], w_ref[...], preferred_element_type=F32)
    groups = TN_IN // LANES_V7X

    g0 = j * groups
    is_norm = (g0 < G_VA) | ((g0 >= G_QM) & (g0 < G_GM))
    is_silu = ((g0 >= G_GA) & (g0 < G_VB)) | ((g0 >= G_GB) & (g0 < G_QM)) | (g0 >= G_GM)

    @pl.when(is_norm)
    def _():
        for g in range(groups):
            sl = slice(g * LANES_V7X, (g + 1) * LANES_V7X)
            p_ref[0, g] = (_rms_rows(acc[:, sl]) * gain_ref[:, sl]).astype(BF16)

    @pl.when(is_silu)
    def _():
        for g in range(groups):
            a = acc[:, g * LANES_V7X:(g + 1) * LANES_V7X]
            p_ref[0, g] = (a * _sigmoid(a)).astype(BF16)

    @pl.when(jnp.logical_not(is_norm | is_silu))
    def _():
        for g in range(groups):
            p_ref[0, g] = acc[:, g * LANES_V7X:(g + 1) * LANES_V7X].astype(BF16)

    @pl.when(g0 == G_VB)
    def _():
        vb_ref[0] = acc


def _in_proj(x, norm_gain, w_in_bf, gain_all):
    B, S, D = x.shape
    n_j = w_in_bf.shape[1] // TN_IN
    groups = TN_IN // LANES_V7X
    return pl.pallas_call(
        _in_proj_kernel,
        grid=(B, S // TM_IN, n_j),
        in_specs=[
            pl.BlockSpec((1, TM_IN, D), lambda b, t, j: (b, t, 0)),
            pl.BlockSpec((1, D), lambda b, t, j: (0, 0)),
            pl.BlockSpec((D, TN_IN), lambda b, t, j: (0, j)),
            pl.BlockSpec((1, TN_IN), lambda b, t, j: (0, j)),
        ],
        out_specs=[
            pl.BlockSpec((1, groups, TM_IN, LANES_V7X), lambda b, t, j: (b, j, t, 0)),
            pl.BlockSpec((1, TM_IN, TN_IN), lambda b, t, j: (b, t, 0)),
        ],
        out_shape=[
            jax.ShapeDtypeStruct((B, N_GROUPS, S, LANES_V7X), BF16),
            jax.ShapeDtypeStruct((B, S, TN_IN), F32),
        ],
        scratch_shapes=[pltpu.VMEM((TM_IN, D), BF16)],
        compiler_params=_params(("parallel", "arbitrary", "arbitrary")),
        name="in_proj",
    )(x, norm_gain, w_in_bf, gain_all)


def _mem_kv_kernel(mem_ref, mg_ref, w_ref, kg_ref, o_ref):
    mh = (_rms_rows(mem_ref[0]) * mg_ref[...]).astype(BF16)
    kv = jnp.dot(mh, w_ref[...], preferred_element_type=F32)
    for g in range(M_HEADS):
        sl = slice(g * HEAD_DIM, (g + 1) * HEAD_DIM)
        o_ref[0, g] = (_rms_rows(kv[:, sl]) * kg_ref[:, sl]).astype(BF16)
    for g in range(M_HEADS, 2 * M_HEADS):
        o_ref[0, g] = kv[:, g * HEAD_DIM:(g + 1) * HEAD_DIM].astype(BF16)


def _mem_kv(mem, mem_norm_gain, w_kv_bf, k_gain):
    B, N, D = mem.shape
    W = w_kv_bf.shape[1]
    return pl.pallas_call(
        _mem_kv_kernel,
        grid=(B,),
        in_specs=[
            pl.BlockSpec((1, N, D), lambda b: (b, 0, 0)),
            pl.BlockSpec((1, D), lambda b: (0, 0)),
            pl.BlockSpec((D, W), lambda b: (0, 0)),
            pl.BlockSpec((1, W // 2), lambda b: (0, 0)),
        ],
        out_specs=pl.BlockSpec((1, 2 * M_HEADS, N, HEAD_DIM), lambda b: (b, 0, 0, 0)),
        out_shape=jax.ShapeDtypeStruct((B, 2 * M_HEADS, N, HEAD_DIM), BF16),
        compiler_params=_params(("arbitrary",)),
        name="mem_kv",
    )(mem, mem_norm_gain, w_kv_bf, k_gain)


_BAND = N_KT * TQ
_RWRAP = _BAND + TQ


def _rel_bias_kernel(r_ref, o_ref):
    row = lax.broadcasted_iota(jnp.int32, (TQ, _RWRAP), 0)
    row_b = lax.broadcasted_iota(jnp.int32, (TQ, _BAND), 0)
    col_b = lax.broadcasted_iota(jnp.int32, (TQ, _BAND), 1)
    lo = jnp.where(row_b < CHUNK, 0, CHUNK)
    valid = (col_b >= lo) & (col_b < lo + _BAND - CHUNK)
    for h in range(A_HEADS):
        t = jnp.broadcast_to(r_ref[h:h + 1, :], (TQ, _RWRAP))
        for k in range(7):
            t = jnp.where(((row >> k) & 1) == 1, pltpu.roll(t, 1 << k, 1), t)
        o_ref[h] = jnp.where(valid, t[:, :_BAND], NEG_INF)


def _rel_bias_tiles(rel_bias):
    H = rel_bias.shape[0]
    edge = jnp.broadcast_to(rel_bias[:, 2 * REL_CLIP:], (H, 2 * REL_CLIP))
    mid = jnp.flip(rel_bias[:, 2 * REL_CLIP + 1 - (_BAND - REL_CLIP):], axis=1)
    r_ext = jnp.concatenate([edge[:, :REL_CLIP], mid, edge[:, :_RWRAP - _BAND]], axis=1)
    return pl.pallas_call(
        _rel_bias_kernel,
        out_shape=jax.ShapeDtypeStruct((H, TQ, _BAND), F32),
        compiler_params=pltpu.CompilerParams(vmem_limit_bytes=VMEM_LIMIT_V7X),
        name="rel_bias",
    )(r_ext)


def _band_attn_kernel(q_ref, k_ref, v_ref, g_ref, bias_ref, o_ref):
    n_qt = q_ref.shape[2] // TQ
    col = lax.broadcasted_iota(jnp.int32, (TQ, _BAND), 1)

    def body(qt, carry):
        q0 = pl.multiple_of(qt * TQ, TQ)
        q = q_ref[0, 0, pl.ds(q0, TQ), :]
        k_tiles, v_tiles = [], []
        for a in range(N_KT):
            k0 = pl.multiple_of(jnp.maximum(qt - (N_KT - 1) + a, 0) * TQ, TQ)
            k_tiles.append(k_ref[0, 0, pl.ds(k0, TQ), :])
            v_tiles.append(v_ref[0, 0, pl.ds(k0, TQ), :])
        kb = jnp.concatenate(k_tiles, axis=0)
        vb = jnp.concatenate(v_tiles, axis=0)
        s = lax.dot_general(q, kb, (((1,), (1,)), ((), ())), preferred_element_type=F32)
        s = s + bias_ref[0]
        s = jnp.where(col >= (N_KT - 1 - qt) * TQ, s, NEG_INF)
        m = jnp.max(s, axis=-1, keepdims=True)
        p = jnp.exp(s - m)
        l = jnp.sum(p, axis=-1, keepdims=True)
        o = jnp.dot(p.astype(BF16), vb, preferred_element_type=F32) / l
        o_ref[0, 0, pl.ds(q0, TQ), :] = (o * g_ref[0, 0, pl.ds(q0, TQ), :].astype(F32)).astype(BF16)
        return carry

    lax.fori_loop(0, n_qt, body, 0)


def _band_attn(p, bias):
    B, _, S, _ = p.shape
    blk = (1, 1, S, HEAD_DIM)
    return pl.pallas_call(
        _band_attn_kernel,
        grid=(B, A_HEADS),
        in_specs=[
            pl.BlockSpec(blk, lambda b, h: (b, G_QA + h, 0, 0)),
            pl.BlockSpec(blk, lambda b, h: (b, G_KA + h, 0, 0)),
            pl.BlockSpec(blk, lambda b, h: (b, G_VA + h, 0, 0)),
            pl.BlockSpec(blk, lambda b, h: (b, G_GA + h, 0, 0)),
            pl.BlockSpec((1, TQ, _BAND), lambda b, h: (h, 0, 0)),
        ],
        out_specs=pl.BlockSpec(blk, lambda b, h: (b, h, 0, 0)),
        out_shape=jax.ShapeDtypeStruct((B, A_HEADS, S, HEAD_DIM), BF16),
        compiler_params=_params(("parallel", "arbitrary")),
        name="band_attn",
    )(p, p, p, p, bias)


def _mem_attn_kernel(q_ref, g_ref, k_ref, v_ref, o_ref):
    n_qt = q_ref.shape[2] // TQ_MEM
    mk = k_ref[0, 0]
    mv = v_ref[0, 0]

    def body(qt, carry):
        q0 = pl.multiple_of(qt * TQ_MEM, TQ_MEM)
        q = q_ref[0, 0, pl.ds(q0, TQ_MEM), :]
        s = lax.dot_general(q, mk, (((1,), (1,)), ((), ())), preferred_element_type=F32)
        m = jnp.max(s, axis=-1, keepdims=True)
        p = jnp.exp(s - m)
        l = jnp.sum(p, axis=-1, keepdims=True)
        o = jnp.dot(p.astype(BF16), mv, preferred_element_type=F32) / l
        o_ref[0, 0, pl.ds(q0, TQ_MEM), :] = (
            o * g_ref[0, 0, pl.ds(q0, TQ_MEM), :].astype(F32)).astype(BF16)
        return carry

    lax.fori_loop(0, n_qt, body, 0)


def _mem_attn(p, memkv):
    B, _, S, _ = p.shape
    N = memkv.shape[2]
    blk = (1, 1, S, HEAD_DIM)
    kblk = (1, 1, N, HEAD_DIM)
    return pl.pallas_call(
        _mem_attn_kernel,
        grid=(B, M_HEADS),
        in_specs=[
            pl.BlockSpec(blk, lambda b, h: (b, G_QM + h, 0, 0)),
            pl.BlockSpec(blk, lambda b, h: (b, G_GM + h, 0, 0)),
            pl.BlockSpec(kblk, lambda b, h: (b, h, 0, 0)),
            pl.BlockSpec(kblk, lambda b, h: (b, M_HEADS + h, 0, 0)),
        ],
        out_specs=pl.BlockSpec(blk, lambda b, h: (b, h, 0, 0)),
        out_shape=jax.ShapeDtypeStruct((B, M_HEADS, S, HEAD_DIM), BF16),
        compiler_params=_params(("parallel", "arbitrary")),
        name="mem_attn",
    )(p, p, memkv, memkv)


def _pool_mix_kernel(vb_ref, prev_ref, g_ref, pw_ref, ps_ref, o_ref):
    t = pl.program_id(1)
    cur = vb_ref[0]
    prev = jnp.where(t > 0, prev_ref[0], 0.0)
    pos = t * TR_POOL + lax.broadcasted_iota(jnp.int32, (TR_POOL, 1), 0)
    for g, w in enumerate(POOL_WINDOWS):
        sl = slice(g * LANES_V7X, (g + 1) * LANES_V7X)
        acc = jnp.concatenate([prev[:, sl], cur[:, sl]], axis=0)
        d = 1
        while d < w:
            acc = acc + pltpu.roll(acc, d, 0)
            d *= 2
        cnt = jnp.minimum(pos + 1, w).astype(F32)
        pooled = acc[POOL_HALO:, :] / cnt - cur[:, sl]
        mixed = jnp.dot(pooled.astype(BF16), pw_ref[g], preferred_element_type=F32)
        o_ref[0, g] = (mixed * ps_ref[:, sl] * g_ref[0, g].astype(F32)).astype(BF16)


def _pool_mix(vb, p, pool_w_bf, pool_scale):
    B, S, P = vb.shape
    G = len(POOL_WINDOWS)
    halo_blocks = TR_POOL // POOL_HALO
    return pl.pallas_call(
        _pool_mix_kernel,
        grid=(B, S // TR_POOL),
        in_specs=[
            pl.BlockSpec((1, TR_POOL, P), lambda b, t: (b, t, 0)),
            pl.BlockSpec((1, POOL_HALO, P), lambda b, t: (b, jnp.maximum(t * halo_blocks - 1, 0), 0)),
            pl.BlockSpec((1, G, TR_POOL, LANES_V7X), lambda b, t: (b, G_GB // G, t, 0)),
            pl.BlockSpec((G, LANES_V7X, LANES_V7X), lambda b, t: (0, 0, 0)),
            pl.BlockSpec((1, P), lambda b, t: (0, 0)),
        ],
        out_specs=pl.BlockSpec((1, G, TR_POOL, LANES_V7X), lambda b, t: (b, 0, t, 0)),
        out_shape=jax.ShapeDtypeStruct((B, G, S, LANES_V7X), BF16),
        compiler_params=_params(("parallel", "arbitrary")),
        name="pool_mix",
    )(vb, vb, p, pool_w_bf, pool_scale)


def _merge_out_kernel(x_ref, ng_ref, oa_ref, ob_ref, om_ref,
                      wg0_ref, wg1_ref, wg2_ref, b0_ref, b1_ref, b2_ref,
                      wa_ref, wb_ref, wm_ref, wo_ref, out_ref, h_scr):
    n = pl.program_id(2)

    @pl.when(n == 0)
    def _():
        h_scr[...] = (_rms_rows(x_ref[0]) * ng_ref[...]).astype(BF16)

    h = h_scr[...]

    def branch(o_ref, w_ref, wg_ref, b_ref):
        o = jnp.concatenate([o_ref[0, g] for g in range(o_ref.shape[1])], axis=-1)
        gate = _sigmoid(jnp.dot(h, wg_ref[...], preferred_element_type=F32) + b_ref[...])
        return gate * jnp.dot(o, w_ref[...], preferred_element_type=F32)

    y = (branch(oa_ref, wa_ref, wg0_ref, b0_ref)
         + branch(ob_ref, wb_ref, wg1_ref, b1_ref)
         + branch(om_ref, wm_ref, wg2_ref, b2_ref))
    contrib = jnp.dot(y.astype(BF16), wo_ref[...], preferred_element_type=F32)

    @pl.when(n == 0)
    def _():
        out_ref[0] = x_ref[0] + contrib

    @pl.when(n > 0)
    def _():
        out_ref[0] = out_ref[0] + contrib


def _merge_out(x, norm_gain, oga, ogb, ogm, w_merge_bf, b_merge, wa_bf, wb_bf, wm_bf, wo_bf):
    B, S, D = x.shape
    n_c = D // CN_OUT

    def gate_w(r):
        return pl.BlockSpec((D, CN_OUT), lambda b, t, n: (0, r * n_c + n))

    def gate_b(r):
        return pl.BlockSpec((1, CN_OUT), lambda b, t, n: (0, r * n_c + n))

    def heads(nh):
        return pl.BlockSpec((1, nh, TM_OUT, HEAD_DIM), lambda b, t, n: (b, 0, t, 0))

    def cols(rows):
        return pl.BlockSpec((rows, CN_OUT), lambda b, t, n: (0, n))

    return pl.pallas_call(
        _merge_out_kernel,
        grid=(B, S // TM_OUT, n_c),
        in_specs=[
            pl.BlockSpec((1, TM_OUT, D), lambda b, t, n: (b, t, 0)),
            pl.BlockSpec((1, D), lambda b, t, n: (0, 0)),
            heads(oga.shape[1]), heads(ogb.shape[1]), heads(ogm.shape[1]),
            gate_w(0), gate_w(1), gate_w(2), gate_b(0), gate_b(1), gate_b(2),
            cols(wa_bf.shape[0]), cols(wb_bf.shape[0]), cols(wm_bf.shape[0]),
            pl.BlockSpec((CN_OUT, D), lambda b, t, n: (n, 0)),
        ],
        out_specs=pl.BlockSpec((1, TM_OUT, D), lambda b, t, n: (b, t, 0)),
        out_shape=jax.ShapeDtypeStruct((B, S, D), F32),
        scratch_shapes=[pltpu.VMEM((TM_OUT, D), BF16)],
        compiler_params=_params(("parallel", "arbitrary", "arbitrary")),
        name="merge_out",
    )(x, norm_gain, oga, ogb, ogm, w_merge_bf, w_merge_bf, w_merge_bf,
      b_merge, b_merge, b_merge, wa_bf, wb_bf, wm_bf, wo_bf)


def kernel(x, mem, norm_gain, mem_norm_gain, w_in, w_merge, b_merge, a_q_gain, a_k_gain,
           a_rel_bias, pool_w, pool_scale, w_mem_kv, m_q_gain, m_k_gain,
           w_branch_a, w_branch_b, w_branch_m, w_out):
    depth = w_in.shape[0]
    scale = HEAD_DIM ** -0.5
    for l in range(depth):
        a_w, p_w, m_w = w_branch_a.shape[1], w_branch_b.shape[1], w_branch_m.shape[1]
        gain_all = jnp.concatenate([
            a_q_gain[l].reshape(1, a_w) * scale, a_k_gain[l].reshape(1, a_w),
            jnp.ones((1, 2 * a_w + 2 * p_w), F32),
            m_q_gain[l].reshape(1, m_w) * scale, jnp.ones((1, m_w), F32)], axis=1)
        ng = norm_gain[l:l + 1]

        p, vb = _in_proj(x, ng, w_in[l].astype(BF16), gain_all)
        memkv = _mem_kv(mem, mem_norm_gain[l:l + 1], w_mem_kv[l].astype(BF16),
                        m_k_gain[l].reshape(1, m_w))
        bias = _rel_bias_tiles(a_rel_bias[l])
        oga = _band_attn(p, bias)
        ogm = _mem_attn(p, memkv)
        ogb = _pool_mix(vb, p, pool_w[l].astype(BF16), pool_scale[l:l + 1])
        x = _merge_out(x, ng, oga, ogb, ogm, w_merge[l].astype(BF16), b_merge[l:l + 1],
                       w_branch_a[l].astype(BF16), w_branch_b[l].astype(BF16),
                       w_branch_m[l].astype(BF16), w_out[l].astype(BF16))
    return x
```

```python
import jax
import jax.numpy as jnp
from jax import lax
from jax.experimental import pallas as pl
from jax.experimental.pallas import tpu as pltpu

F32 = jnp.float32
BF16 = jnp.bfloat16

CHUNK = 64
N_LEFT_CHUNKS = 8
A_HEADS = 8
M_HEADS = 4
HEAD_DIM = 128
REL_CLIP = 256
POOL_WINDOWS = (2, 4, 8, 16)
EPS = 1e-6
NEG_INF = -1e30
LOG2_E = 1.4426950408889634

LANES_V7X = 128
VMEM_LIMIT_V7X = 60000 * 1024

TM_IN = 1024
TN_IN = 512
TQ = 128
N_KT = 5
ATTN_TILES_PER_STEP = 4
ATTN_HEADS_PER_STEP = 4
TQ_MEM = 256
MEM_TILES_PER_STEP = 2
TR_POOL = 512
POOL_HALO = 16
TM_OUT = 512
CN_OUT = 512

G_QA, G_KA, G_VA, G_GA, G_VB, G_GB, G_QM, G_GM = 0, 8, 16, 24, 32, 36, 40, 44
N_GROUPS = 48


def _params(sem):
    return pltpu.CompilerParams(dimension_semantics=sem, vmem_limit_bytes=VMEM_LIMIT_V7X)


def _rms_rows(t):
    return t * lax.rsqrt(jnp.mean(t * t, axis=-1, keepdims=True) + EPS)


def _sigmoid(z):
    return 1.0 / (1.0 + jnp.exp(-z))


def _in_proj_kernel(x_ref, ng_ref, w_ref, gain_ref, p_ref, vb_ref, h_scr):
    j = pl.program_id(2)

    @pl.when(j == 0)
    def _():
        h_scr[...] = (_rms_rows(x_ref[0]) * ng_ref[...]).astype(BF16)

    acc = jnp.dot(h_scr[...], w_ref[...], preferred_element_type=F32)
    groups = TN_IN // LANES_V7X

    g0 = j * groups
    is_norm = (g0 < G_VA) | ((g0 >= G_QM) & (g0 < G_GM))
    is_silu = ((g0 >= G_GA) & (g0 < G_VB)) | ((g0 >= G_GB) & (g0 < G_QM)) | (g0 >= G_GM)

    @pl.when(is_norm)
    def _():
        for g in range(groups):
            sl = slice(g * LANES_V7X, (g + 1) * LANES_V7X)
            p_ref[0, g] = (_rms_rows(acc[:, sl]) * gain_ref[:, sl]).astype(BF16)

    @pl.when(is_silu)
    def _():
        for g in range(groups):
            a = acc[:, g * LANES_V7X:(g + 1) * LANES_V7X]
            p_ref[0, g] = (a * _sigmoid(a)).astype(BF16)

    @pl.when(jnp.logical_not(is_norm | is_silu))
    def _():
        for g in range(groups):
            p_ref[0, g] = acc[:, g * LANES_V7X:(g + 1) * LANES_V7X].astype(BF16)

    @pl.when(g0 == G_VB)
    def _():
        vb_ref[0] = acc


def _in_proj(x, norm_gain, w_in_bf, gain_all):
    B, S, D = x.shape
    n_j = w_in_bf.shape[1] // TN_IN
    groups = TN_IN // LANES_V7X
    return pl.pallas_call(
        _in_proj_kernel,
        grid=(B, S // TM_IN, n_j),
        in_specs=[
            pl.BlockSpec((1, TM_IN, D), lambda b, t, j: (b, t, 0)),
            pl.BlockSpec((1, D), lambda b, t, j: (0, 0)),
            pl.BlockSpec((D, TN_IN), lambda b, t, j: (0, j)),
            pl.BlockSpec((1, TN_IN), lambda b, t, j: (0, j)),
        ],
        out_specs=[
            pl.BlockSpec((1, groups, TM_IN, LANES_V7X), lambda b, t, j: (b, j, t, 0)),
            pl.BlockSpec((1, TM_IN, TN_IN), lambda b, t, j: (b, t, 0)),
        ],
        out_shape=[
            jax.ShapeDtypeStruct((B, N_GROUPS, S, LANES_V7X), BF16),
            jax.ShapeDtypeStruct((B, S, TN_IN), F32),
        ],
        scratch_shapes=[pltpu.VMEM((TM_IN, D), BF16)],
        compiler_params=_params(("parallel", "arbitrary", "arbitrary")),
        name="in_proj",
    )(x, norm_gain, w_in_bf, gain_all)


def _mem_kv_kernel(mem_ref, mg_ref, w_ref, kg_ref, o_ref):
    mh = (_rms_rows(mem_ref[0]) * mg_ref[...]).astype(BF16)
    kv = jnp.dot(mh, w_ref[...], preferred_element_type=F32)
    for g in range(M_HEADS):
        sl = slice(g * HEAD_DIM, (g + 1) * HEAD_DIM)
        o_ref[0, g] = (_rms_rows(kv[:, sl]) * kg_ref[:, sl]).astype(BF16)
    for g in range(M_HEADS, 2 * M_HEADS):
        o_ref[0, g] = kv[:, g * HEAD_DIM:(g + 1) * HEAD_DIM].astype(BF16)


def _mem_kv(mem, mem_norm_gain, w_kv_bf, k_gain):
    B, N, D = mem.shape
    W = w_kv_bf.shape[1]
    return pl.pallas_call(
        _mem_kv_kernel,
        grid=(B,),
        in_specs=[
            pl.BlockSpec((1, N, D), lambda b: (b, 0, 0)),
            pl.BlockSpec((1, D), lambda b: (0, 0)),
            pl.BlockSpec((D, W), lambda b: (0, 0)),
            pl.BlockSpec((1, W // 2), lambda b: (0, 0)),
        ],
        out_specs=pl.BlockSpec((1, 2 * M_HEADS, N, HEAD_DIM), lambda b: (b, 0, 0, 0)),
        out_shape=jax.ShapeDtypeStruct((B, 2 * M_HEADS, N, HEAD_DIM), BF16),
        compiler_params=_params(("arbitrary",)),
        name="mem_kv",
    )(mem, mem_norm_gain, w_kv_bf, k_gain)


_BAND = N_KT * TQ
_RWRAP = _BAND + TQ


def _rel_bias_kernel(r_ref, o_ref):
    row = lax.broadcasted_iota(jnp.int32, (TQ, _RWRAP), 0)
    row_b = lax.broadcasted_iota(jnp.int32, (TQ, _BAND), 0)
    col_b = lax.broadcasted_iota(jnp.int32, (TQ, _BAND), 1)
    lo = jnp.where(row_b < CHUNK, 0, CHUNK)
    valid = (col_b >= lo) & (col_b < lo + _BAND - CHUNK)
    for h in range(A_HEADS):
        t = jnp.broadcast_to(r_ref[h:h + 1, :], (TQ, _RWRAP))
        for k in range(7):
            t = jnp.where(((row >> k) & 1) == 1, pltpu.roll(t, 1 << k, 1), t)
        o_ref[h] = jnp.where(valid, t[:, :_BAND] * LOG2_E, NEG_INF)


def _rel_bias_tiles(rel_bias):
    H = rel_bias.shape[0]
    edge = jnp.broadcast_to(rel_bias[:, 2 * REL_CLIP:], (H, 2 * REL_CLIP))
    mid = jnp.flip(rel_bias[:, 2 * REL_CLIP + 1 - (_BAND - REL_CLIP):], axis=1)
    r_ext = jnp.concatenate([edge[:, :REL_CLIP], mid, edge[:, :_RWRAP - _BAND]], axis=1)
    return pl.pallas_call(
        _rel_bias_kernel,
        out_shape=jax.ShapeDtypeStruct((H, TQ, _BAND), F32),
        compiler_params=pltpu.CompilerParams(vmem_limit_bytes=VMEM_LIMIT_V7X),
        name="rel_bias",
    )(r_ext)


def _attention_pipeline(n_groups, tiles, scores_fn, values_fn, s_scr, p_scr, l_scr):
    rows, width = l_scr.shape[1:]
    s_scr[...] = jnp.zeros(s_scr.shape, F32)
    p_scr[...] = jnp.zeros(p_scr.shape, BF16)
    l_scr[...] = jnp.ones(l_scr.shape, F32)

    def body(j, carry):
        g3 = jnp.clip(j - 2, 0, n_groups - 1)
        for u in range(tiles):
            values_fn(g3, u, p_scr[u], l_scr[u])
        for u in range(tiles):
            s = s_scr[u]
            p = jnp.exp2(s - jnp.max(s, axis=-1, keepdims=True))
            l_scr[u] = jnp.broadcast_to(jnp.sum(p, axis=-1, keepdims=True), (rows, width))
            p_scr[u] = p.astype(BF16)
        g1 = jnp.minimum(j, n_groups - 1)
        for u in range(tiles):
            s_scr[u] = scores_fn(g1, u)
        return carry

    lax.fori_loop(0, n_groups + 2, body, 0)


def _attention_scratch(tiles, rows, cols):
    return [pltpu.VMEM((tiles, rows, cols), F32), pltpu.VMEM((tiles, rows, cols), BF16),
            pltpu.VMEM((tiles, rows, HEAD_DIM), F32)]


_QK_DIMS = (((1,), (1,)), ((), ()))


def _band_attn_kernel(q_ref, k_ref, v_ref, g_ref, bias_ref, o_ref, s_scr, p_scr, l_scr):
    groups_per_head = q_ref.shape[2] // (TQ * ATTN_TILES_PER_STEP)
    col = lax.broadcasted_iota(jnp.int32, (TQ, _BAND), 1)

    def coords(group, u):
        head = group // groups_per_head
        qt = (group % groups_per_head) * ATTN_TILES_PER_STEP + u
        return head, qt, pl.multiple_of(qt * TQ, TQ)

    def band(ref, head, qt):
        tiles = []
        for a in range(N_KT):
            k0 = pl.multiple_of(jnp.maximum(qt - (N_KT - 1) + a, 0) * TQ, TQ)
            tiles.append(ref[0, head, pl.ds(k0, TQ), :])
        return jnp.concatenate(tiles, axis=0)

    def scores(group, u):
        head, qt, q0 = coords(group, u)
        s = lax.dot_general(q_ref[0, head, pl.ds(q0, TQ), :], band(k_ref, head, qt), _QK_DIMS,
                            preferred_element_type=F32)
        s = s + bias_ref[head]
        return jnp.where(col >= (N_KT - 1 - qt) * TQ, s, NEG_INF)

    def values(group, u, p, l):
        head, qt, q0 = coords(group, u)
        o = jnp.dot(p, band(v_ref, head, qt), preferred_element_type=F32) / l
        o_ref[0, head, pl.ds(q0, TQ), :] = (
            o * g_ref[0, head, pl.ds(q0, TQ), :].astype(F32)).astype(BF16)

    _attention_pipeline(ATTN_HEADS_PER_STEP * groups_per_head, ATTN_TILES_PER_STEP,
                        scores, values, s_scr, p_scr, l_scr)


def _band_attn(p, bias):
    B, _, S, _ = p.shape
    hb = ATTN_HEADS_PER_STEP
    blk = (1, hb, S, HEAD_DIM)

    def seg(g_first):
        return pl.BlockSpec(blk, lambda b, h: (b, g_first // hb + h, 0, 0))

    return pl.pallas_call(
        _band_attn_kernel,
        grid=(B, A_HEADS // hb),
        in_specs=[seg(G_QA), seg(G_KA), seg(G_VA), seg(G_GA),
                  pl.BlockSpec((hb, TQ, _BAND), lambda b, h: (h, 0, 0))],
        out_specs=seg(0),
        out_shape=jax.ShapeDtypeStruct((B, A_HEADS, S, HEAD_DIM), BF16),
        scratch_shapes=_attention_scratch(ATTN_TILES_PER_STEP, TQ, _BAND),
        compiler_params=_params(("parallel", "arbitrary")),
        name="band_attn",
    )(p, p, p, p, bias)


def _mem_attn_kernel(q_ref, g_ref, kv_ref, o_ref, s_scr, p_scr, l_scr):
    groups_per_head = q_ref.shape[2] // (TQ_MEM * MEM_TILES_PER_STEP)

    def coords(group, u):
        head = group // groups_per_head
        qt = (group % groups_per_head) * MEM_TILES_PER_STEP + u
        return head, pl.multiple_of(qt * TQ_MEM, TQ_MEM)

    def scores(group, u):
        head, q0 = coords(group, u)
        return lax.dot_general(q_ref[0, head, pl.ds(q0, TQ_MEM), :], kv_ref[0, head], _QK_DIMS,
                               preferred_element_type=F32)

    def values(group, u, p, l):
        head, q0 = coords(group, u)
        o = jnp.dot(p, kv_ref[0, M_HEADS + head], preferred_element_type=F32) / l
        o_ref[0, head, pl.ds(q0, TQ_MEM), :] = (
            o * g_ref[0, head, pl.ds(q0, TQ_MEM), :].astype(F32)).astype(BF16)

    _attention_pipeline(M_HEADS * groups_per_head, MEM_TILES_PER_STEP,
                        scores, values, s_scr, p_scr, l_scr)


def _mem_attn(p, memkv):
    B, _, S, _ = p.shape
    N = memkv.shape[2]
    blk = (1, M_HEADS, S, HEAD_DIM)
    return pl.pallas_call(
        _mem_attn_kernel,
        grid=(B,),
        in_specs=[
            pl.BlockSpec(blk, lambda b: (b, G_QM // M_HEADS, 0, 0)),
            pl.BlockSpec(blk, lambda b: (b, G_GM // M_HEADS, 0, 0)),
            pl.BlockSpec((1, 2 * M_HEADS, N, HEAD_DIM), lambda b: (b, 0, 0, 0)),
        ],
        out_specs=pl.BlockSpec(blk, lambda b: (b, 0, 0, 0)),
        out_shape=jax.ShapeDtypeStruct((B, M_HEADS, S, HEAD_DIM), BF16),
        scratch_shapes=_attention_scratch(MEM_TILES_PER_STEP, TQ_MEM, N),
        compiler_params=_params(("arbitrary",)),
        name="mem_attn",
    )(p, p, memkv)


def _pool_mix_kernel(vb_ref, prev_ref, g_ref, pw_ref, ps_ref, o_ref):
    t = pl.program_id(1)
    cur = vb_ref[0]
    prev = jnp.where(t > 0, prev_ref[0], 0.0)
    pos = t * TR_POOL + lax.broadcasted_iota(jnp.int32, (TR_POOL, 1), 0)
    for g, w in enumerate(POOL_WINDOWS):
        sl = slice(g * LANES_V7X, (g + 1) * LANES_V7X)
        acc = jnp.concatenate([prev[:, sl], cur[:, sl]], axis=0)
        d = 1
        while d < w:
            acc = acc + pltpu.roll(acc, d, 0)
            d *= 2
        cnt = jnp.minimum(pos + 1, w).astype(F32)
        pooled = acc[POOL_HALO:, :] / cnt - cur[:, sl]
        mixed = jnp.dot(pooled.astype(BF16), pw_ref[g], preferred_element_type=F32)
        o_ref[0, g] = (mixed * ps_ref[:, sl] * g_ref[0, g].astype(F32)).astype(BF16)


def _pool_mix(vb, p, pool_w_bf, pool_scale):
    B, S, P = vb.shape
    G = len(POOL_WINDOWS)
    halo_blocks = TR_POOL // POOL_HALO
    return pl.pallas_call(
        _pool_mix_kernel,
        grid=(B, S // TR_POOL),
        in_specs=[
            pl.BlockSpec((1, TR_POOL, P), lambda b, t: (b, t, 0)),
            pl.BlockSpec((1, POOL_HALO, P), lambda b, t: (b, jnp.maximum(t * halo_blocks - 1, 0), 0)),
            pl.BlockSpec((1, G, TR_POOL, LANES_V7X), lambda b, t: (b, G_GB // G, t, 0)),
            pl.BlockSpec((G, LANES_V7X, LANES_V7X), lambda b, t: (0, 0, 0)),
            pl.BlockSpec((1, P), lambda b, t: (0, 0)),
        ],
        out_specs=pl.BlockSpec((1, G, TR_POOL, LANES_V7X), lambda b, t: (b, 0, t, 0)),
        out_shape=jax.ShapeDtypeStruct((B, G, S, LANES_V7X), BF16),
        compiler_params=_params(("parallel", "arbitrary")),
        name="pool_mix",
    )(vb, vb, p, pool_w_bf, pool_scale)


def _merge_out_kernel(x_ref, ng_ref, oa_ref, ob_ref, om_ref,
                      wg0_ref, wg1_ref, wg2_ref, b0_ref, b1_ref, b2_ref,
                      wa_ref, wb_ref, wm_ref, wo_ref, out_ref, h_scr):
    n = pl.program_id(2)

    @pl.when(n == 0)
    def _():
        h_scr[...] = (_rms_rows(x_ref[0]) * ng_ref[...]).astype(BF16)

    h = h_scr[...]

    def branch(o_ref, w_ref, wg_ref, b_ref):
        o = jnp.concatenate([o_ref[0, g] for g in range(o_ref.shape[1])], axis=-1)
        gate = _sigmoid(jnp.dot(h, wg_ref[...], preferred_element_type=F32) + b_ref[...])
        return gate * jnp.dot(o, w_ref[...], preferred_element_type=F32)

    y = (branch(oa_ref, wa_ref, wg0_ref, b0_ref)
         + branch(ob_ref, wb_ref, wg1_ref, b1_ref)
         + branch(om_ref, wm_ref, wg2_ref, b2_ref))
    contrib = jnp.dot(y.astype(BF16), wo_ref[...], preferred_element_type=F32)

    @pl.when(n == 0)
    def _():
        out_ref[0] = x_ref[0] + contrib

    @pl.when(n > 0)
    def _():
        out_ref[0] = out_ref[0] + contrib


def _merge_out(x, norm_gain, oga, ogb, ogm, w_merge_bf, b_merge, wa_bf, wb_bf, wm_bf, wo_bf):
    B, S, D = x.shape
    n_c = D // CN_OUT

    def gate_w(r):
        return pl.BlockSpec((D, CN_OUT), lambda b, t, n: (0, r * n_c + n))

    def gate_b(r):
        return pl.BlockSpec((1, CN_OUT), lambda b, t, n: (0, r * n_c + n))

    def heads(nh):
        return pl.BlockSpec((1, nh, TM_OUT, HEAD_DIM), lambda b, t, n: (b, 0, t, 0))

    def cols(rows):
        return pl.BlockSpec((rows, CN_OUT), lambda b, t, n: (0, n))

    return pl.pallas_call(
        _merge_out_kernel,
        grid=(B, S // TM_OUT, n_c),
        in_specs=[
            pl.BlockSpec((1, TM_OUT, D), lambda b, t, n: (b, t, 0)),
            pl.BlockSpec((1, D), lambda b, t, n: (0, 0)),
            heads(oga.shape[1]), heads(ogb.shape[1]), heads(ogm.shape[1]),
            gate_w(0), gate_w(1), gate_w(2), gate_b(0), gate_b(1), gate_b(2),
            cols(wa_bf.shape[0]), cols(wb_bf.shape[0]), cols(wm_bf.shape[0]),
            pl.BlockSpec((CN_OUT, D), lambda b, t, n: (n, 0)),
        ],
        out_specs=pl.BlockSpec((1, TM_OUT, D), lambda b, t, n: (b, t, 0)),
        out_shape=jax.ShapeDtypeStruct((B, S, D), F32),
        scratch_shapes=[pltpu.VMEM((TM_OUT, D), BF16)],
        compiler_params=_params(("parallel", "arbitrary", "arbitrary")),
        name="merge_out",
    )(x, norm_gain, oga, ogb, ogm, w_merge_bf, w_merge_bf, w_merge_bf,
      b_merge, b_merge, b_merge, wa_bf, wb_bf, wm_bf, wo_bf)


def kernel(x, mem, norm_gain, mem_norm_gain, w_in, w_merge, b_merge, a_q_gain, a_k_gain,
           a_rel_bias, pool_w, pool_scale, w_mem_kv, m_q_gain, m_k_gain,
           w_branch_a, w_branch_b, w_branch_m, w_out):
    depth = w_in.shape[0]
    scale = HEAD_DIM ** -0.5 * LOG2_E
    for l in range(depth):
        a_w, p_w, m_w = w_branch_a.shape[1], w_branch_b.shape[1], w_branch_m.shape[1]
        gain_all = jnp.concatenate([
            a_q_gain[l].reshape(1, a_w) * scale, a_k_gain[l].reshape(1, a_w),
            jnp.ones((1, 2 * a_w + 2 * p_w), F32),
            m_q_gain[l].reshape(1, m_w) * scale, jnp.ones((1, m_w), F32)], axis=1)
        ng = norm_gain[l:l + 1]

        p, vb = _in_proj(x, ng, w_in[l].astype(BF16), gain_all)
        memkv = _mem_kv(mem, mem_norm_gain[l:l + 1], w_mem_kv[l].astype(BF16),
                        m_k_gain[l].reshape(1, m_w))
        bias = _rel_bias_tiles(a_rel_bias[l])
        oga = _band_attn(p, bias)
        ogm = _mem_attn(p, memkv)
        ogb = _pool_mix(vb, p, pool_w[l].astype(BF16), pool_scale[l:l + 1])
        x = _merge_out(x, ng, oga, ogb, ogm, w_merge[l].astype(BF16), b_merge[l:l + 1],
                       w_branch_a[l].astype(BF16), w_branch_b[l].astype(BF16),
                       w_branch_m[l].astype(BF16), w_out[l].astype(BF16))
    return x
```

```python
import functools

import jax
import jax.numpy as jnp
from jax import lax
from jax.experimental import pallas as pl
from jax.experimental.pallas import tpu as pltpu

F32 = jnp.float32
BF16 = jnp.bfloat16

CHUNK = 64
N_LEFT_CHUNKS = 8
A_HEADS = 8
M_HEADS = 4
HEAD_DIM = 128
REL_CLIP = 256
POOL_WINDOWS = (2, 4, 8, 16)
EPS = 1e-6
NEG_INF = -1e30
LOG2_E = 1.4426950408889634

LANES_V7X = 128
MXU_COLS_V7X = 256
VMEM_LIMIT_V7X = 60000 * 1024

TM_IN = 1024
TN_IN = 512
IN_K_CHUNKS = 8
TQ = 128
N_KT = 5
ATTN_TILES_PER_STEP = 4
ATTN_HEADS_PER_STEP = 4
TQ_MEM = 256
MEM_TILES_PER_STEP = 2
TR_POOL = 512
POOL_HALO = 16
TM_OUT = 512
CN_OUT = 512

G_QA, G_KA, G_VA, G_GA, G_VB, G_GB, G_QM, G_GM = 0, 8, 16, 24, 32, 36, 40, 44
N_GROUPS = 48


def _params(sem):
    return pltpu.CompilerParams(dimension_semantics=sem, vmem_limit_bytes=VMEM_LIMIT_V7X)


def _rms_rows(t):
    return t * lax.rsqrt(jnp.mean(t * t, axis=-1, keepdims=True) + EPS)


def _sigmoid(z):
    return 1.0 / (1.0 + jnp.exp(-z))


def _tied_zero(src, shape, dtype):
    rows = src.shape[0]
    folded = jnp.sum(src.reshape(rows // 8, 8, src.shape[1]), axis=0)
    bits = pltpu.bitcast(folded, jnp.uint32)
    zero = pltpu.bitcast((bits >> 16) >> 16, F32)
    return jnp.tile(zero, (shape[0] // 8, shape[1] // src.shape[1])).astype(dtype)


def _in_proj_kernel(x_ref, ng_ref, w_ref, gain_ref, p_ref, vb_ref, h_scr, acc_scr, *, n_steps, n_j):
    f = pl.program_id(0)
    groups = TN_IN // LANES_V7X
    j_cur = jnp.minimum(f, n_steps - 1) % n_j
    g_prev = (jnp.maximum(f - 1, 0) % n_j) * groups

    @pl.when(f == 0)
    def _():
        acc_scr[...] = jnp.zeros(acc_scr.shape, F32)

    @pl.when((j_cur == 0) & (f < n_steps))
    def _():
        h_scr[...] = (_rms_rows(x_ref[0]) * ng_ref[...]).astype(BF16)

    @pl.when(g_prev == G_VB)
    def _():
        vb_ref[0] = acc_scr[...]

    is_norm = (g_prev < G_VA) | ((g_prev >= G_QM) & (g_prev < G_GM))
    is_silu = ((g_prev >= G_GA) & (g_prev < G_VB)) | ((g_prev >= G_GB) & (g_prev < G_QM)) | (g_prev >= G_GM)
    k_chunk = h_scr.shape[1] // IN_K_CHUNKS
    pieces = [(n0, c) for n0 in range(0, TN_IN, MXU_COLS_V7X) for c in range(IN_K_CHUNKS)]
    stride = len(pieces) // groups
    accs = {}
    ties = {}
    for i, (n0, c) in enumerate(pieces):
        ks = slice(c * k_chunk, (c + 1) * k_chunk)
        lhs = h_scr[:, ks]
        if i in ties:
            zero = _tied_zero(ties.pop(i), (16, k_chunk), BF16)
            lhs = jnp.concatenate([lhs[:16] + zero, lhs[16:]], axis=0)
        part = jnp.dot(lhs, w_ref[ks, n0:n0 + MXU_COLS_V7X], preferred_element_type=F32)
        accs[n0] = part if c == 0 else accs[n0] + part
        if i % stride == 0:
            g = i // stride
            sl = slice(g * LANES_V7X, (g + 1) * LANES_V7X)
            a = acc_scr[:, sl]
            normed = _rms_rows(a) * gain_ref[:, sl]
            res = jnp.where(is_norm, normed, jnp.where(is_silu, a * _sigmoid(a), a))
            p_ref[0, g] = res.astype(BF16)
            ties[i + stride - 1] = res
    for n0, acc in accs.items():
        acc_scr[:, n0:n0 + MXU_COLS_V7X] = acc


def _in_proj(x, norm_gain, w_in_bf, gain_all):
    B, S, D = x.shape
    n_t = S // TM_IN
    n_j = w_in_bf.shape[1] // TN_IN
    n_steps = B * n_t * n_j
    groups = TN_IN // LANES_V7X

    def cur(f):
        f = jnp.minimum(f, n_steps - 1)
        return f // (n_t * n_j), (f // n_j) % n_t, f % n_j

    def prev(f):
        return cur(jnp.maximum(f - 1, 0))

    return pl.pallas_call(
        functools.partial(_in_proj_kernel, n_steps=n_steps, n_j=n_j),
        grid=(n_steps + 1,),
        in_specs=[
            pl.BlockSpec((1, TM_IN, D), lambda f: (cur(f)[0], cur(f)[1], 0)),
            pl.BlockSpec((1, D), lambda f: (0, 0)),
            pl.BlockSpec((D, TN_IN), lambda f: (0, cur(f)[2])),
            pl.BlockSpec((1, TN_IN), lambda f: (0, prev(f)[2])),
        ],
        out_specs=[
            pl.BlockSpec((1, groups, TM_IN, LANES_V7X),
                         lambda f: (prev(f)[0], prev(f)[2], prev(f)[1], 0)),
            pl.BlockSpec((1, TM_IN, TN_IN), lambda f: (prev(f)[0], prev(f)[1], 0)),
        ],
        out_shape=[
            jax.ShapeDtypeStruct((B, N_GROUPS, S, LANES_V7X), BF16),
            jax.ShapeDtypeStruct((B, S, TN_IN), F32),
        ],
        scratch_shapes=[pltpu.VMEM((TM_IN, D), BF16), pltpu.VMEM((TM_IN, TN_IN), F32)],
        compiler_params=_params(("arbitrary",)),
        name="in_proj",
    )(x, norm_gain, w_in_bf, gain_all)


def _mem_kv_kernel(mem_ref, mg_ref, w_ref, kg_ref, o_ref):
    mh = (_rms_rows(mem_ref[0]) * mg_ref[...]).astype(BF16)
    kv = jnp.dot(mh, w_ref[...], preferred_element_type=F32)
    for g in range(M_HEADS):
        sl = slice(g * HEAD_DIM, (g + 1) * HEAD_DIM)
        o_ref[0, g] = (_rms_rows(kv[:, sl]) * kg_ref[:, sl]).astype(BF16)
    for g in range(M_HEADS, 2 * M_HEADS):
        o_ref[0, g] = kv[:, g * HEAD_DIM:(g + 1) * HEAD_DIM].astype(BF16)


def _mem_kv(mem, mem_norm_gain, w_kv_bf, k_gain):
    B, N, D = mem.shape
    W = w_kv_bf.shape[1]
    return pl.pallas_call(
        _mem_kv_kernel,
        grid=(B,),
        in_specs=[
            pl.BlockSpec((1, N, D), lambda b: (b, 0, 0)),
            pl.BlockSpec((1, D), lambda b: (0, 0)),
            pl.BlockSpec((D, W), lambda b: (0, 0)),
            pl.BlockSpec((1, W // 2), lambda b: (0, 0)),
        ],
        out_specs=pl.BlockSpec((1, 2 * M_HEADS, N, HEAD_DIM), lambda b: (b, 0, 0, 0)),
        out_shape=jax.ShapeDtypeStruct((B, 2 * M_HEADS, N, HEAD_DIM), BF16),
        compiler_params=_params(("arbitrary",)),
        name="mem_kv",
    )(mem, mem_norm_gain, w_kv_bf, k_gain)


_BAND = N_KT * TQ
_RWRAP = _BAND + TQ


def _rel_bias_kernel(r_ref, o_ref):
    row = lax.broadcasted_iota(jnp.int32, (TQ, _RWRAP), 0)
    row_b = lax.broadcasted_iota(jnp.int32, (TQ, _BAND), 0)
    col_b = lax.broadcasted_iota(jnp.int32, (TQ, _BAND), 1)
    lo = jnp.where(row_b < CHUNK, 0, CHUNK)
    valid = (col_b >= lo) & (col_b < lo + _BAND - CHUNK)
    for h in range(A_HEADS):
        t = jnp.broadcast_to(r_ref[h:h + 1, :], (TQ, _RWRAP))
        for k in range(7):
            t = jnp.where(((row >> k) & 1) == 1, pltpu.roll(t, 1 << k, 1), t)
        o_ref[h] = jnp.where(valid, t[:, :_BAND] * LOG2_E, NEG_INF)


def _rel_bias_tiles(rel_bias):
    H = rel_bias.shape[0]
    edge = jnp.broadcast_to(rel_bias[:, 2 * REL_CLIP:], (H, 2 * REL_CLIP))
    mid = jnp.flip(rel_bias[:, 2 * REL_CLIP + 1 - (_BAND - REL_CLIP):], axis=1)
    r_ext = jnp.concatenate([edge[:, :REL_CLIP], mid, edge[:, :_RWRAP - _BAND]], axis=1)
    return pl.pallas_call(
        _rel_bias_kernel,
        out_shape=jax.ShapeDtypeStruct((H, TQ, _BAND), F32),
        compiler_params=pltpu.CompilerParams(vmem_limit_bytes=VMEM_LIMIT_V7X),
        name="rel_bias",
    )(r_ext)


def _attention_pipeline(n_groups, tiles, scores_fn, values_fn, s_scr, p_scr, l_scr):
    rows, width = l_scr.shape[1:]
    s_scr[...] = jnp.zeros(s_scr.shape, F32)
    p_scr[...] = jnp.zeros(p_scr.shape, BF16)
    l_scr[...] = jnp.ones(l_scr.shape, F32)

    def body(j, carry):
        g3 = jnp.clip(j - 2, 0, n_groups - 1)
        for u in range(tiles):
            values_fn(g3, u, p_scr[u], l_scr[u])
        for u in range(tiles):
            s = s_scr[u]
            p = jnp.exp2(s - jnp.max(s, axis=-1, keepdims=True))
            l_scr[u] = jnp.broadcast_to(jnp.sum(p, axis=-1, keepdims=True), (rows, width))
            p_scr[u] = p.astype(BF16)
        g1 = jnp.minimum(j, n_groups - 1)
        for u in range(tiles):
            s_scr[u] = scores_fn(g1, u)
        return carry

    lax.fori_loop(0, n_groups + 2, body, 0)


def _attention_scratch(tiles, rows, cols):
    return [pltpu.VMEM((tiles, rows, cols), F32), pltpu.VMEM((tiles, rows, cols), BF16),
            pltpu.VMEM((tiles, rows, HEAD_DIM), F32)]


_QK_DIMS = (((1,), (1,)), ((), ()))


def _band_attn_kernel(q_ref, k_ref, v_ref, g_ref, bias_ref, o_ref, s_scr, p_scr, l_scr):
    groups_per_head = q_ref.shape[2] // (TQ * ATTN_TILES_PER_STEP)
    col = lax.broadcasted_iota(jnp.int32, (TQ, _BAND), 1)

    def coords(group, u):
        head = group // groups_per_head
        qt = (group % groups_per_head) * ATTN_TILES_PER_STEP + u
        return head, qt, pl.multiple_of(qt * TQ, TQ)

    def band(ref, head, qt):
        tiles = []
        for a in range(N_KT):
            k0 = pl.multiple_of(jnp.maximum(qt - (N_KT - 1) + a, 0) * TQ, TQ)
            tiles.append(ref[0, head, pl.ds(k0, TQ), :])
        return jnp.concatenate(tiles, axis=0)

    def scores(group, u):
        head, qt, q0 = coords(group, u)
        s = lax.dot_general(q_ref[0, head, pl.ds(q0, TQ), :], band(k_ref, head, qt), _QK_DIMS,
                            preferred_element_type=F32)
        s = s + bias_ref[head]
        return jnp.where(col >= (N_KT - 1 - qt) * TQ, s, NEG_INF)

    def values(group, u, p, l):
        head, qt, q0 = coords(group, u)
        o = jnp.dot(p, band(v_ref, head, qt), preferred_element_type=F32) / l
        o_ref[0, head, pl.ds(q0, TQ), :] = (
            o * g_ref[0, head, pl.ds(q0, TQ), :].astype(F32)).astype(BF16)

    _attention_pipeline(ATTN_HEADS_PER_STEP * groups_per_head, ATTN_TILES_PER_STEP,
                        scores, values, s_scr, p_scr, l_scr)


def _band_attn(p, bias):
    B, _, S, _ = p.shape
    hb = ATTN_HEADS_PER_STEP
    blk = (1, hb, S, HEAD_DIM)

    def seg(g_first):
        return pl.BlockSpec(blk, lambda b, h: (b, g_first // hb + h, 0, 0))

    return pl.pallas_call(
        _band_attn_kernel,
        grid=(B, A_HEADS // hb),
        in_specs=[seg(G_QA), seg(G_KA), seg(G_VA), seg(G_GA),
                  pl.BlockSpec((hb, TQ, _BAND), lambda b, h: (h, 0, 0))],
        out_specs=seg(0),
        out_shape=jax.ShapeDtypeStruct((B, A_HEADS, S, HEAD_DIM), BF16),
        scratch_shapes=_attention_scratch(ATTN_TILES_PER_STEP, TQ, _BAND),
        compiler_params=_params(("parallel", "arbitrary")),
        name="band_attn",
    )(p, p, p, p, bias)


def _mem_attn_kernel(q_ref, g_ref, kv_ref, o_ref, s_scr, p_scr, l_scr):
    groups_per_head = q_ref.shape[2] // (TQ_MEM * MEM_TILES_PER_STEP)

    def coords(group, u):
        head = group // groups_per_head
        qt = (group % groups_per_head) * MEM_TILES_PER_STEP + u
        return head, pl.multiple_of(qt * TQ_MEM, TQ_MEM)

    def scores(group, u):
        head, q0 = coords(group, u)
        return lax.dot_general(q_ref[0, head, pl.ds(q0, TQ_MEM), :], kv_ref[0, head], _QK_DIMS,
                               preferred_element_type=F32)

    def values(group, u, p, l):
        head, q0 = coords(group, u)
        o = jnp.dot(p, kv_ref[0, M_HEADS + head], preferred_element_type=F32) / l
        o_ref[0, head, pl.ds(q0, TQ_MEM), :] = (
            o * g_ref[0, head, pl.ds(q0, TQ_MEM), :].astype(F32)).astype(BF16)

    _attention_pipeline(M_HEADS * groups_per_head, MEM_TILES_PER_STEP,
                        scores, values, s_scr, p_scr, l_scr)


def _mem_attn(p, memkv):
    B, _, S, _ = p.shape
    N = memkv.shape[2]
    blk = (1, M_HEADS, S, HEAD_DIM)
    return pl.pallas_call(
        _mem_attn_kernel,
        grid=(B,),
        in_specs=[
            pl.BlockSpec(blk, lambda b: (b, G_QM // M_HEADS, 0, 0)),
            pl.BlockSpec(blk, lambda b: (b, G_GM // M_HEADS, 0, 0)),
            pl.BlockSpec((1, 2 * M_HEADS, N, HEAD_DIM), lambda b: (b, 0, 0, 0)),
        ],
        out_specs=pl.BlockSpec(blk, lambda b: (b, 0, 0, 0)),
        out_shape=jax.ShapeDtypeStruct((B, M_HEADS, S, HEAD_DIM), BF16),
        scratch_shapes=_attention_scratch(MEM_TILES_PER_STEP, TQ_MEM, N),
        compiler_params=_params(("arbitrary",)),
        name="mem_attn",
    )(p, p, memkv)


def _pool_mix_kernel(vb_ref, prev_ref, g_ref, pw_ref, ps_ref, o_ref):
    t = pl.program_id(1)
    cur = vb_ref[0]
    prev = jnp.where(t > 0, prev_ref[0], 0.0)
    pos = t * TR_POOL + lax.broadcasted_iota(jnp.int32, (TR_POOL, 1), 0)
    for g, w in enumerate(POOL_WINDOWS):
        sl = slice(g * LANES_V7X, (g + 1) * LANES_V7X)
        acc = jnp.concatenate([prev[:, sl], cur[:, sl]], axis=0)
        d = 1
        while d < w:
            acc = acc + pltpu.roll(acc, d, 0)
            d *= 2
        cnt = jnp.minimum(pos + 1, w).astype(F32)
        pooled = acc[POOL_HALO:, :] / cnt - cur[:, sl]
        mixed = jnp.dot(pooled.astype(BF16), pw_ref[g], preferred_element_type=F32)
        o_ref[0, g] = (mixed * ps_ref[:, sl] * g_ref[0, g].astype(F32)).astype(BF16)


def _pool_mix(vb, p, pool_w_bf, pool_scale):
    B, S, P = vb.shape
    G = len(POOL_WINDOWS)
    halo_blocks = TR_POOL // POOL_HALO
    return pl.pallas_call(
        _pool_mix_kernel,
        grid=(B, S // TR_POOL),
        in_specs=[
            pl.BlockSpec((1, TR_POOL, P), lambda b, t: (b, t, 0)),
            pl.BlockSpec((1, POOL_HALO, P), lambda b, t: (b, jnp.maximum(t * halo_blocks - 1, 0), 0)),
            pl.BlockSpec((1, G, TR_POOL, LANES_V7X), lambda b, t: (b, G_GB // G, t, 0)),
            pl.BlockSpec((G, LANES_V7X, LANES_V7X), lambda b, t: (0, 0, 0)),
            pl.BlockSpec((1, P), lambda b, t: (0, 0)),
        ],
        out_specs=pl.BlockSpec((1, G, TR_POOL, LANES_V7X), lambda b, t: (b, 0, t, 0)),
        out_shape=jax.ShapeDtypeStruct((B, G, S, LANES_V7X), BF16),
        compiler_params=_params(("parallel", "arbitrary")),
        name="pool_mix",
    )(vb, vb, p, pool_w_bf, pool_scale)


def _merge_out_kernel(x_ref, ng_ref, oa_ref, ob_ref, om_ref,
                      wg0_ref, wg1_ref, wg2_ref, b0_ref, b1_ref, b2_ref,
                      wa_ref, wb_ref, wm_ref, wo_ref, out_ref, h_scr):
    n = pl.program_id(2)

    @pl.when(n == 0)
    def _():
        x = x_ref[0]
        h_scr[...] = (_rms_rows(x) * ng_ref[...]).astype(BF16)
        out_ref[0] = x

    h = h_scr[...]

    def branch(o_ref, w_ref, wg_ref, b_ref):
        o = jnp.concatenate([o_ref[0, g] for g in range(o_ref.shape[1])], axis=-1)
        gate = _sigmoid(jnp.dot(h, wg_ref[...], preferred_element_type=F32) + b_ref[...])
        return gate * jnp.dot(o, w_ref[...], preferred_element_type=F32)

    y = (branch(oa_ref, wa_ref, wg0_ref, b0_ref)
         + branch(ob_ref, wb_ref, wg1_ref, b1_ref)
         + branch(om_ref, wm_ref, wg2_ref, b2_ref))
    out_ref[0] += jnp.dot(y.astype(BF16), wo_ref[...], preferred_element_type=F32)


def _merge_out(x, norm_gain, oga, ogb, ogm, w_merge_bf, b_merge, wa_bf, wb_bf, wm_bf, wo_bf):
    B, S, D = x.shape
    n_c = D // CN_OUT

    def gate_w(r):
        return pl.BlockSpec((D, CN_OUT), lambda b, t, n: (0, r * n_c + n))

    def gate_b(r):
        return pl.BlockSpec((1, CN_OUT), lambda b, t, n: (0, r * n_c + n))

    def heads(nh):
        return pl.BlockSpec((1, nh, TM_OUT, HEAD_DIM), lambda b, t, n: (b, 0, t, 0))

    def cols(rows):
        return pl.BlockSpec((rows, CN_OUT), lambda b, t, n: (0, n))

    return pl.pallas_call(
        _merge_out_kernel,
        grid=(B, S // TM_OUT, n_c),
        in_specs=[
            pl.BlockSpec((1, TM_OUT, D), lambda b, t, n: (b, t, 0)),
            pl.BlockSpec((1, D), lambda b, t, n: (0, 0)),
            heads(oga.shape[1]), heads(ogb.shape[1]), heads(ogm.shape[1]),
            gate_w(0), gate_w(1), gate_w(2), gate_b(0), gate_b(1), gate_b(2),
            cols(wa_bf.shape[0]), cols(wb_bf.shape[0]), cols(wm_bf.shape[0]),
            pl.BlockSpec((CN_OUT, D), lambda b, t, n: (n, 0)),
        ],
        out_specs=pl.BlockSpec((1, TM_OUT, D), lambda b, t, n: (b, t, 0)),
        out_shape=jax.ShapeDtypeStruct((B, S, D), F32),
        scratch_shapes=[pltpu.VMEM((TM_OUT, D), BF16)],
        compiler_params=_params(("parallel", "arbitrary", "arbitrary")),
        name="merge_out",
    )(x, norm_gain, oga, ogb, ogm, w_merge_bf, w_merge_bf, w_merge_bf,
      b_merge, b_merge, b_merge, wa_bf, wb_bf, wm_bf, wo_bf)


def kernel(x, mem, norm_gain, mem_norm_gain, w_in, w_merge, b_merge, a_q_gain, a_k_gain,
           a_rel_bias, pool_w, pool_scale, w_mem_kv, m_q_gain, m_k_gain,
           w_branch_a, w_branch_b, w_branch_m, w_out):
    depth = w_in.shape[0]
    scale = HEAD_DIM ** -0.5 * LOG2_E
    for l in range(depth):
        a_w, p_w, m_w = w_branch_a.shape[1], w_branch_b.shape[1], w_branch_m.shape[1]
        gain_all = jnp.concatenate([
            a_q_gain[l].reshape(1, a_w) * scale, a_k_gain[l].reshape(1, a_w),
            jnp.ones((1, 2 * a_w + 2 * p_w), F32),
            m_q_gain[l].reshape(1, m_w) * scale, jnp.ones((1, m_w), F32)], axis=1)
        ng = norm_gain[l:l + 1]

        p, vb = _in_proj(x, ng, w_in[l].astype(BF16), gain_all)
        memkv = _mem_kv(mem, mem_norm_gain[l:l + 1], w_mem_kv[l].astype(BF16),
                        m_k_gain[l].reshape(1, m_w))
        bias = _rel_bias_tiles(a_rel_bias[l])
        oga = _band_attn(p, bias)
        ogm = _mem_attn(p, memkv)
        ogb = _pool_mix(vb, p, pool_w[l].astype(BF16), pool_scale[l:l + 1])
        x = _merge_out(x, ng, oga, ogb, ogm, w_merge[l].astype(BF16), b_merge[l:l + 1],
                       w_branch_a[l].astype(BF16), w_branch_b[l].astype(BF16),
                       w_branch_m[l].astype(BF16), w_out[l].astype(BF16))
    return x
```

```python
import functools

import jax
import jax.numpy as jnp
from jax import lax
from jax.experimental import pallas as pl
from jax.experimental.pallas import tpu as pltpu

F32 = jnp.float32
BF16 = jnp.bfloat16

CHUNK = 64
N_LEFT_CHUNKS = 8
A_HEADS = 8
M_HEADS = 4
HEAD_DIM = 128
REL_CLIP = 256
POOL_WINDOWS = (2, 4, 8, 16)
EPS = 1e-6
NEG_INF = -1e30
LOG2_E = 1.4426950408889634

LANES_V7X = 128
MXU_COLS_V7X = 256
VMEM_LIMIT_V7X = 60000 * 1024

TM_IN = 1024
TN_IN = 1024
IN_K_CHUNKS = 8
TQ = 128
N_KT = 5
ATTN_TILES_PER_STEP = 4
ATTN_HEADS_PER_STEP = 4
TQ_MEM = 256
MEM_TILES_PER_STEP = 2
TR_POOL = 2048
POOL_HALO = 16
TM_OUT = 512
CN_OUT = 512

G_QA, G_KA, G_VA, G_GA, G_VB, G_GB, G_QM, G_GM = 0, 8, 16, 24, 32, 36, 40, 44
N_GROUPS = 48


def _params(sem):
    return pltpu.CompilerParams(dimension_semantics=sem, vmem_limit_bytes=VMEM_LIMIT_V7X)


def _rms_rows(t):
    return t * lax.rsqrt(jnp.mean(t * t, axis=-1, keepdims=True) + EPS)


def _sigmoid(z):
    return 1.0 / (1.0 + jnp.exp(-z))


def _tied_zero(src, shape, dtype):
    rows = src.shape[0]
    folded = jnp.sum(src.reshape(rows // 8, 8, src.shape[1]), axis=0)
    bits = pltpu.bitcast(folded, jnp.uint32)
    zero = pltpu.bitcast((bits >> 16) >> 16, F32)
    return jnp.tile(zero, (shape[0] // 8, shape[1] // src.shape[1])).astype(dtype)


def _in_proj_kernel(x_ref, ng_ref, w_ref, gain_ref, p_ref, vb_ref, h_scr, acc_scr, *, n_steps, n_j):
    f = pl.program_id(0)
    groups = TN_IN // LANES_V7X
    j_cur = jnp.minimum(f, n_steps - 1) % n_j
    g_prev = (jnp.maximum(f - 1, 0) % n_j) * groups

    @pl.when(f == 0)
    def _():
        acc_scr[...] = jnp.zeros(acc_scr.shape, F32)

    @pl.when((j_cur == 0) & (f < n_steps))
    def _():
        h_scr[...] = (_rms_rows(x_ref[0]) * ng_ref[...]).astype(BF16)

    @pl.when(g_prev == G_VB // groups * groups)
    def _():
        vb0 = G_VB % groups * LANES_V7X
        vb_ref[0] = acc_scr[:, vb0:vb0 + vb_ref.shape[2]]

    k_chunk = h_scr.shape[1] // IN_K_CHUNKS
    pieces = [(n0, c) for n0 in range(0, TN_IN, MXU_COLS_V7X) for c in range(IN_K_CHUNKS)]
    stride = len(pieces) // groups
    accs = {}
    ties = {}
    for i, (n0, c) in enumerate(pieces):
        ks = slice(c * k_chunk, (c + 1) * k_chunk)
        lhs = h_scr[:, ks]
        if i in ties:
            zero = _tied_zero(ties.pop(i), (16, k_chunk), BF16)
            lhs = jnp.concatenate([lhs[:16] + zero, lhs[16:]], axis=0)
        part = jnp.dot(lhs, w_ref[ks, n0:n0 + MXU_COLS_V7X], preferred_element_type=F32)
        accs[n0] = part if c == 0 else accs[n0] + part
        if i % stride == 0:
            g = i // stride
            sl = slice(g * LANES_V7X, (g + 1) * LANES_V7X)
            a = acc_scr[:, sl]
            gi = g_prev + g
            is_norm = (gi < G_VA) | ((gi >= G_QM) & (gi < G_GM))
            is_silu = ((gi >= G_GA) & (gi < G_VB)) | ((gi >= G_GB) & (gi < G_QM)) | (gi >= G_GM)
            normed = _rms_rows(a) * gain_ref[:, sl]
            res = jnp.where(is_norm, normed, jnp.where(is_silu, a * _sigmoid(a), a))
            p_ref[0, g] = res.astype(BF16)
            ties[i + stride - 1] = res
    for n0, acc in accs.items():
        acc_scr[:, n0:n0 + MXU_COLS_V7X] = acc


def _in_proj(x, norm_gain, w_in_bf, gain_all):
    B, S, D = x.shape
    n_t = S // TM_IN
    n_j = w_in_bf.shape[1] // TN_IN
    n_steps = B * n_t * n_j
    groups = TN_IN // LANES_V7X
    vb_cols = (G_GB - G_VB) * LANES_V7X

    def cur(f):
        f = jnp.minimum(f, n_steps - 1)
        return f // (n_t * n_j), (f // n_j) % n_t, f % n_j

    def prev(f):
        return cur(jnp.maximum(f - 1, 0))

    return pl.pallas_call(
        functools.partial(_in_proj_kernel, n_steps=n_steps, n_j=n_j),
        grid=(n_steps + 1,),
        in_specs=[
            pl.BlockSpec((1, TM_IN, D), lambda f: (cur(f)[0], cur(f)[1], 0)),
            pl.BlockSpec((1, D), lambda f: (0, 0)),
            pl.BlockSpec((D, TN_IN), lambda f: (0, cur(f)[2])),
            pl.BlockSpec((1, TN_IN), lambda f: (0, prev(f)[2])),
        ],
        out_specs=[
            pl.BlockSpec((1, groups, TM_IN, LANES_V7X),
                         lambda f: (prev(f)[0], prev(f)[2], prev(f)[1], 0)),
            pl.BlockSpec((1, TM_IN, vb_cols), lambda f: (prev(f)[0], prev(f)[1], 0)),
        ],
        out_shape=[
            jax.ShapeDtypeStruct((B, N_GROUPS, S, LANES_V7X), BF16),
            jax.ShapeDtypeStruct((B, S, vb_cols), F32),
        ],
        scratch_shapes=[pltpu.VMEM((TM_IN, D), BF16), pltpu.VMEM((TM_IN, TN_IN), F32)],
        compiler_params=_params(("arbitrary",)),
        name="in_proj",
    )(x, norm_gain, w_in_bf, gain_all)


def _mem_kv_kernel(mem_ref, mg_ref, w_ref, kg_ref, o_ref):
    mh = (_rms_rows(mem_ref[0]) * mg_ref[...]).astype(BF16)
    kv = jnp.dot(mh, w_ref[...], preferred_element_type=F32)
    for g in range(M_HEADS):
        sl = slice(g * HEAD_DIM, (g + 1) * HEAD_DIM)
        o_ref[0, g] = (_rms_rows(kv[:, sl]) * kg_ref[:, sl]).astype(BF16)
    for g in range(M_HEADS, 2 * M_HEADS):
        o_ref[0, g] = kv[:, g * HEAD_DIM:(g + 1) * HEAD_DIM].astype(BF16)


def _mem_kv(mem, mem_norm_gain, w_kv_bf, k_gain):
    B, N, D = mem.shape
    W = w_kv_bf.shape[1]
    return pl.pallas_call(
        _mem_kv_kernel,
        grid=(B,),
        in_specs=[
            pl.BlockSpec((1, N, D), lambda b: (b, 0, 0)),
            pl.BlockSpec((1, D), lambda b: (0, 0)),
            pl.BlockSpec((D, W), lambda b: (0, 0)),
            pl.BlockSpec((1, W // 2), lambda b: (0, 0)),
        ],
        out_specs=pl.BlockSpec((1, 2 * M_HEADS, N, HEAD_DIM), lambda b: (b, 0, 0, 0)),
        out_shape=jax.ShapeDtypeStruct((B, 2 * M_HEADS, N, HEAD_DIM), BF16),
        compiler_params=_params(("arbitrary",)),
        name="mem_kv",
    )(mem, mem_norm_gain, w_kv_bf, k_gain)


_BAND = N_KT * TQ
_RWRAP = _BAND + TQ


def _rel_bias_kernel(r_ref, o_ref):
    row = lax.broadcasted_iota(jnp.int32, (TQ, _RWRAP), 0)
    row_b = lax.broadcasted_iota(jnp.int32, (TQ, _BAND), 0)
    col_b = lax.broadcasted_iota(jnp.int32, (TQ, _BAND), 1)
    lo = jnp.where(row_b < CHUNK, 0, CHUNK)
    valid = (col_b >= lo) & (col_b < lo + _BAND - CHUNK)
    for h in range(A_HEADS):
        t = jnp.broadcast_to(r_ref[h:h + 1, :], (TQ, _RWRAP))
        for k in range(7):
            t = jnp.where(((row >> k) & 1) == 1, pltpu.roll(t, 1 << k, 1), t)
        o_ref[h] = jnp.where(valid, t[:, :_BAND] * LOG2_E, NEG_INF)


def _rel_bias_tiles(rel_bias):
    H = rel_bias.shape[0]
    edge = jnp.broadcast_to(rel_bias[:, 2 * REL_CLIP:], (H, 2 * REL_CLIP))
    mid = jnp.flip(rel_bias[:, 2 * REL_CLIP + 1 - (_BAND - REL_CLIP):], axis=1)
    r_ext = jnp.concatenate([edge[:, :REL_CLIP], mid, edge[:, :_RWRAP - _BAND]], axis=1)
    return pl.pallas_call(
        _rel_bias_kernel,
        out_shape=jax.ShapeDtypeStruct((H, TQ, _BAND), F32),
        compiler_params=pltpu.CompilerParams(vmem_limit_bytes=VMEM_LIMIT_V7X),
        name="rel_bias",
    )(r_ext)


def _attention_pipeline(n_groups, tiles, scores_fn, values_fn, s_scr, p_scr, l_scr):
    rows, width = l_scr.shape[1:]
    s_scr[...] = jnp.zeros(s_scr.shape, F32)
    p_scr[...] = jnp.zeros(p_scr.shape, BF16)
    l_scr[...] = jnp.ones(l_scr.shape, F32)

    def body(j, carry):
        g3 = jnp.clip(j - 2, 0, n_groups - 1)
        for u in range(tiles):
            values_fn(g3, u, p_scr[u], l_scr[u])
        for u in range(tiles):
            s = s_scr[u]
            p = jnp.exp2(s - jnp.max(s, axis=-1, keepdims=True))
            l_scr[u] = jnp.broadcast_to(jnp.sum(p, axis=-1, keepdims=True), (rows, width))
            p_scr[u] = p.astype(BF16)
        g1 = jnp.minimum(j, n_groups - 1)
        for u in range(tiles):
            s_scr[u] = scores_fn(g1, u)
        return carry

    lax.fori_loop(0, n_groups + 2, body, 0)


def _attention_scratch(tiles, rows, cols):
    return [pltpu.VMEM((tiles, rows, cols), F32), pltpu.VMEM((tiles, rows, cols), BF16),
            pltpu.VMEM((tiles, rows, HEAD_DIM), F32)]


_QK_DIMS = (((1,), (1,)), ((), ()))


def _band_attn_kernel(q_ref, k_ref, v_ref, g_ref, bias_ref, o_ref, *scratch):
    groups_per_head = q_ref.shape[2] // (TQ * ATTN_TILES_PER_STEP)
    col = lax.broadcasted_iota(jnp.int32, (TQ, _BAND), 1)

    def coords(group, u):
        head = group // groups_per_head
        qt = (group % groups_per_head) * ATTN_TILES_PER_STEP + u
        return head, qt, pl.multiple_of(qt * TQ, TQ)

    def band(ref, head, qt):
        tiles = []
        for a in range(N_KT):
            k0 = pl.multiple_of(jnp.maximum(qt - (N_KT - 1) + a, 0) * TQ, TQ)
            tiles.append(ref[0, head, pl.ds(k0, TQ), :])
        return jnp.concatenate(tiles, axis=0)

    def scores(group, u):
        head, qt, q0 = coords(group, u)
        s = lax.dot_general(q_ref[0, head, pl.ds(q0, TQ), :], band(k_ref, head, qt), _QK_DIMS,
                            preferred_element_type=F32)
        s = s + bias_ref[head]
        return jnp.where(col >= (N_KT - 1 - qt) * TQ, s, NEG_INF)

    def values(group, u, p, l):
        head, qt, q0 = coords(group, u)
        o = jnp.dot(p, band(v_ref, head, qt), preferred_element_type=F32) / l
        o_ref[0, head, pl.ds(q0, TQ), :] = (
            o * g_ref[0, head, pl.ds(q0, TQ), :].astype(F32)).astype(BF16)

    _attention_pipeline(ATTN_HEADS_PER_STEP * groups_per_head, ATTN_TILES_PER_STEP,
                        scores, values, *scratch)


def _band_attn(p, bias):
    B, _, S, _ = p.shape
    hb = ATTN_HEADS_PER_STEP
    blk = (1, hb, S, HEAD_DIM)

    def seg(g_first):
        return pl.BlockSpec(blk, lambda b, h: (b, g_first // hb + h, 0, 0))

    return pl.pallas_call(
        _band_attn_kernel,
        grid=(B, A_HEADS // hb),
        in_specs=[seg(G_QA), seg(G_KA), seg(G_VA), seg(G_GA),
                  pl.BlockSpec((hb, TQ, _BAND), lambda b, h: (h, 0, 0))],
        out_specs=seg(0),
        out_shape=jax.ShapeDtypeStruct((B, A_HEADS, S, HEAD_DIM), BF16),
        scratch_shapes=_attention_scratch(ATTN_TILES_PER_STEP, TQ, _BAND),
        compiler_params=_params(("parallel", "arbitrary")),
        name="band_attn",
    )(p, p, p, p, bias)


def _mem_attn_kernel(q_ref, g_ref, kv_ref, o_ref, *scratch):
    groups_per_head = q_ref.shape[2] // (TQ_MEM * MEM_TILES_PER_STEP)

    def coords(group, u):
        head = group // groups_per_head
        qt = (group % groups_per_head) * MEM_TILES_PER_STEP + u
        return head, pl.multiple_of(qt * TQ_MEM, TQ_MEM)

    def scores(group, u):
        head, q0 = coords(group, u)
        return lax.dot_general(q_ref[0, head, pl.ds(q0, TQ_MEM), :], kv_ref[0, head], _QK_DIMS,
                               preferred_element_type=F32)

    def values(group, u, p, l):
        head, q0 = coords(group, u)
        o = jnp.dot(p, kv_ref[0, M_HEADS + head], preferred_element_type=F32) / l
        o_ref[0, head, pl.ds(q0, TQ_MEM), :] = (
            o * g_ref[0, head, pl.ds(q0, TQ_MEM), :].astype(F32)).astype(BF16)

    _attention_pipeline(M_HEADS * groups_per_head, MEM_TILES_PER_STEP,
                        scores, values, *scratch)


def _mem_attn(p, memkv):
    B, _, S, _ = p.shape
    N = memkv.shape[2]
    blk = (1, M_HEADS, S, HEAD_DIM)
    return pl.pallas_call(
        _mem_attn_kernel,
        grid=(B,),
        in_specs=[
            pl.BlockSpec(blk, lambda b: (b, G_QM // M_HEADS, 0, 0)),
            pl.BlockSpec(blk, lambda b: (b, G_GM // M_HEADS, 0, 0)),
            pl.BlockSpec((1, 2 * M_HEADS, N, HEAD_DIM), lambda b: (b, 0, 0, 0)),
        ],
        out_specs=pl.BlockSpec(blk, lambda b: (b, 0, 0, 0)),
        out_shape=jax.ShapeDtypeStruct((B, M_HEADS, S, HEAD_DIM), BF16),
        scratch_shapes=_attention_scratch(MEM_TILES_PER_STEP, TQ_MEM, N),
        compiler_params=_params(("arbitrary",)),
        name="mem_attn",
    )(p, p, memkv)


def _pool_mix_kernel(vb_ref, prev_ref, g_ref, pw_ref, ps_ref, o_ref):
    t = pl.program_id(1)
    cur = vb_ref[0]
    prev = jnp.where(t > 0, prev_ref[0], 0.0)
    pos = t * TR_POOL + lax.broadcasted_iota(jnp.int32, (TR_POOL, 1), 0)
    for g, w in enumerate(POOL_WINDOWS):
        sl = slice(g * LANES_V7X, (g + 1) * LANES_V7X)
        acc = jnp.concatenate([prev[:, sl], cur[:, sl]], axis=0)
        d = 1
        while d < w:
            acc = acc + pltpu.roll(acc, d, 0)
            d *= 2
        cnt = jnp.minimum(pos + 1, w).astype(F32)
        pooled = acc[POOL_HALO:, :] / cnt - cur[:, sl]
        mixed = jnp.dot(pooled.astype(BF16), pw_ref[g], preferred_element_type=F32)
        o_ref[0, g] = (mixed * ps_ref[:, sl] * g_ref[0, g].astype(F32)).astype(BF16)


def _pool_mix(vb, p, pool_w_bf, pool_scale):
    B, S, P = vb.shape
    G = len(POOL_WINDOWS)
    halo_blocks = TR_POOL // POOL_HALO
    return pl.pallas_call(
        _pool_mix_kernel,
        grid=(B, S // TR_POOL),
        in_specs=[
            pl.BlockSpec((1, TR_POOL, P), lambda b, t: (b, t, 0)),
            pl.BlockSpec((1, POOL_HALO, P), lambda b, t: (b, jnp.maximum(t * halo_blocks - 1, 0), 0)),
            pl.BlockSpec((1, G, TR_POOL, LANES_V7X), lambda b, t: (b, G_GB // G, t, 0)),
            pl.BlockSpec((G, LANES_V7X, LANES_V7X), lambda b, t: (0, 0, 0)),
            pl.BlockSpec((1, P), lambda b, t: (0, 0)),
        ],
        out_specs=pl.BlockSpec((1, G, TR_POOL, LANES_V7X), lambda b, t: (b, 0, t, 0)),
        out_shape=jax.ShapeDtypeStruct((B, G, S, LANES_V7X), BF16),
        compiler_params=_params(("parallel", "arbitrary")),
        name="pool_mix",
    )(vb, vb, p, pool_w_bf, pool_scale)


def _merge_out_kernel(x_ref, ng_ref, oa_ref, ob_ref, om_ref,
                      wg0_ref, wg1_ref, wg2_ref, b0_ref, b1_ref, b2_ref,
                      wa_ref, wb_ref, wm_ref, wo_ref, out_ref, h_scr):
    n = pl.program_id(2)

    @pl.when(n == 0)
    def _():
        x = x_ref[0]
        h_scr[...] = (_rms_rows(x) * ng_ref[...]).astype(BF16)
        out_ref[0] = x

    h = h_scr[...]

    def branch(o_ref, w_ref, wg_ref, b_ref):
        o = jnp.concatenate([o_ref[0, g] for g in range(o_ref.shape[1])], axis=-1)
        gate = _sigmoid(jnp.dot(h, wg_ref[...], preferred_element_type=F32) + b_ref[...])
        return gate * jnp.dot(o, w_ref[...], preferred_element_type=F32)

    y = (branch(oa_ref, wa_ref, wg0_ref, b0_ref)
         + branch(ob_ref, wb_ref, wg1_ref, b1_ref)
         + branch(om_ref, wm_ref, wg2_ref, b2_ref))
    out_ref[0] += jnp.dot(y.astype(BF16), wo_ref[...], preferred_element_type=F32)


def _merge_out(x, norm_gain, oga, ogb, ogm, w_merge_bf, b_merge, wa_bf, wb_bf, wm_bf, wo_bf):
    B, S, D = x.shape
    n_c = D // CN_OUT

    def gate_w(r):
        return pl.BlockSpec((D, CN_OUT), lambda b, t, n: (0, r * n_c + n))

    def gate_b(r):
        return pl.BlockSpec((1, CN_OUT), lambda b, t, n: (0, r * n_c + n))

    def heads(nh):
        return pl.BlockSpec((1, nh, TM_OUT, HEAD_DIM), lambda b, t, n: (b, 0, t, 0))

    def cols(rows):
        return pl.BlockSpec((rows, CN_OUT), lambda b, t, n: (0, n))

    return pl.pallas_call(
        _merge_out_kernel,
        grid=(B, S // TM_OUT, n_c),
        in_specs=[
            pl.BlockSpec((1, TM_OUT, D), lambda b, t, n: (b, t, 0)),
            pl.BlockSpec((1, D), lambda b, t, n: (0, 0)),
            heads(oga.shape[1]), heads(ogb.shape[1]), heads(ogm.shape[1]),
            gate_w(0), gate_w(1), gate_w(2), gate_b(0), gate_b(1), gate_b(2),
            cols(wa_bf.shape[0]), cols(wb_bf.shape[0]), cols(wm_bf.shape[0]),
            pl.BlockSpec((CN_OUT, D), lambda b, t, n: (n, 0)),
        ],
        out_specs=pl.BlockSpec((1, TM_OUT, D), lambda b, t, n: (b, t, 0)),
        out_shape=jax.ShapeDtypeStruct((B, S, D), F32),
        scratch_shapes=[pltpu.VMEM((TM_OUT, D), BF16)],
        compiler_params=_params(("parallel", "arbitrary", "arbitrary")),
        name="merge_out",
    )(x, norm_gain, oga, ogb, ogm, w_merge_bf, w_merge_bf, w_merge_bf,
      b_merge, b_merge, b_merge, wa_bf, wb_bf, wm_bf, wo_bf)


def kernel(x, mem, norm_gain, mem_norm_gain, w_in, w_merge, b_merge, a_q_gain, a_k_gain,
           a_rel_bias, pool_w, pool_scale, w_mem_kv, m_q_gain, m_k_gain,
           w_branch_a, w_branch_b, w_branch_m, w_out):
    depth = w_in.shape[0]
    scale = HEAD_DIM ** -0.5 * LOG2_E
    for l in range(depth):
        a_w, p_w, m_w = w_branch_a.shape[1], w_branch_b.shape[1], w_branch_m.shape[1]
        gain_all = jnp.concatenate([
            a_q_gain[l].reshape(1, a_w) * scale, a_k_gain[l].reshape(1, a_w),
            jnp.ones((1, 2 * a_w + 2 * p_w), F32),
            m_q_gain[l].reshape(1, m_w) * scale, jnp.ones((1, m_w), F32)], axis=1)
        ng = norm_gain[l:l + 1]

        p, vb = _in_proj(x, ng, w_in[l].astype(BF16), gain_all)
        memkv = _mem_kv(mem, mem_norm_gain[l:l + 1], w_mem_kv[l].astype(BF16),
                        m_k_gain[l].reshape(1, m_w))
        bias = _rel_bias_tiles(a_rel_bias[l])
        oga = _band_attn(p, bias)
        ogm = _mem_attn(p, memkv)
        ogb = _pool_mix(vb, p, pool_w[l].astype(BF16), pool_scale[l:l + 1])
        x = _merge_out(x, ng, oga, ogb, ogm, w_merge[l].astype(BF16), b_merge[l:l + 1],
                       w_branch_a[l].astype(BF16), w_branch_b[l].astype(BF16),
                       w_branch_m[l].astype(BF16), w_out[l].astype(BF16))
    return x
```

```python
import functools

import jax
import jax.numpy as jnp
from jax import lax
from jax.experimental import pallas as pl
from jax.experimental.pallas import tpu as pltpu

F32 = jnp.float32
BF16 = jnp.bfloat16

CHUNK = 64
N_LEFT_CHUNKS = 8
A_HEADS = 8
M_HEADS = 4
HEAD_DIM = 128
REL_CLIP = 256
POOL_WINDOWS = (2, 4, 8, 16)
EPS = 1e-6
NEG_INF = -1e30
LOG2_E = 1.4426950408889634

LANES_V7X = 128
MXU_COLS_V7X = 256
VMEM_LIMIT_V7X = 60000 * 1024

TM_IN = 1024
TN_IN = 1024
IN_K_CHUNKS = 8
TQ = 128
N_KT = 5
ATTN_TILES_PER_STEP = 4
ATTN_HEADS_PER_STEP = 4
TQ_MEM = 256
MEM_TILES_PER_STEP = 2
TR_POOL = 2048
POOL_HALO = 16
TM_OUT = 512
CN_OUT = 512

G_QA, G_KA, G_VA, G_GA, G_VB, G_GB, G_QM, G_GM = 0, 8, 16, 24, 32, 36, 40, 44
N_GROUPS = 48


def _params(sem):
    return pltpu.CompilerParams(dimension_semantics=sem, vmem_limit_bytes=VMEM_LIMIT_V7X)


def _rms_rows(t):
    return t * lax.rsqrt(jnp.mean(t * t, axis=-1, keepdims=True) + EPS)


def _sigmoid(z):
    return 1.0 / (1.0 + jnp.exp(-z))


def _tied_zero(src, shape, dtype):
    rows = src.shape[0]
    folded = jnp.sum(src.reshape(rows // 8, 8, src.shape[1]), axis=0)
    bits = pltpu.bitcast(folded, jnp.uint32)
    zero = pltpu.bitcast((bits >> 16) >> 16, F32)
    return jnp.tile(zero, (shape[0] // 8, shape[1] // src.shape[1])).astype(dtype)


def _in_proj_kernel(*refs, n_steps, n_j, n_cast):
    x_ref, ng_ref, w_ref, gain_ref = refs[:4]
    cast_src = refs[4:4 + n_cast]
    p_ref, vb_ref = refs[4 + n_cast:6 + n_cast]
    cast_dst = refs[6 + n_cast:6 + 2 * n_cast]
    h_scr, acc_scr = refs[6 + 2 * n_cast:]
    f = pl.program_id(0)
    groups = TN_IN // LANES_V7X
    j_cur = jnp.minimum(f, n_steps - 1) % n_j
    g_prev = (jnp.maximum(f - 1, 0) % n_j) * groups

    @pl.when(f == 0)
    def _():
        acc_scr[...] = jnp.zeros(acc_scr.shape, F32)

    @pl.when((j_cur == 0) & (f < n_steps))
    def _():
        h_scr[...] = (_rms_rows(x_ref[0]) * ng_ref[...]).astype(BF16)

    @pl.when(g_prev == G_VB // groups * groups)
    def _():
        vb0 = G_VB % groups * LANES_V7X
        vb_ref[0] = acc_scr[:, vb0:vb0 + vb_ref.shape[2]]

    k_chunk = h_scr.shape[1] // IN_K_CHUNKS
    pieces = [(n0, c) for n0 in range(0, TN_IN, MXU_COLS_V7X) for c in range(IN_K_CHUNKS)]
    stride = len(pieces) // groups
    accs = {}
    ties = {}
    for i, (n0, c) in enumerate(pieces):
        ks = slice(c * k_chunk, (c + 1) * k_chunk)
        lhs = h_scr[:, ks]
        if i in ties:
            zero = _tied_zero(ties.pop(i), (16, k_chunk), BF16)
            lhs = jnp.concatenate([lhs[:16] + zero, lhs[16:]], axis=0)
        part = jnp.dot(lhs, w_ref[ks, n0:n0 + MXU_COLS_V7X], preferred_element_type=F32)
        accs[n0] = part if c == 0 else accs[n0] + part
        if i % stride == 0:
            g = i // stride
            sl = slice(g * LANES_V7X, (g + 1) * LANES_V7X)
            a = acc_scr[:, sl]
            gi = g_prev + g
            is_norm = (gi < G_VA) | ((gi >= G_QM) & (gi < G_GM))
            is_silu = ((gi >= G_GA) & (gi < G_VB)) | ((gi >= G_GB) & (gi < G_QM)) | (gi >= G_GM)
            normed = _rms_rows(a) * gain_ref[:, sl]
            res = jnp.where(is_norm, normed, jnp.where(is_silu, a * _sigmoid(a), a))
            p_ref[0, g] = res.astype(BF16)
            ties[i + stride - 1] = res
    for n0, acc in accs.items():
        acc_scr[:, n0:n0 + MXU_COLS_V7X] = acc
    for src, dst in zip(cast_src, cast_dst):
        dst[...] = src[...].astype(BF16)


def _cast_blocks(rows, n_grid):
    blocks = 1
    while blocks * 2 <= n_grid and rows % (blocks * 2 * 16) == 0:
        blocks *= 2
    return blocks


def _in_proj(x, norm_gain, w_in_bf, gain_all, cast_weights):
    B, S, D = x.shape
    n_t = S // TM_IN
    n_j = w_in_bf.shape[1] // TN_IN
    n_steps = B * n_t * n_j
    groups = TN_IN // LANES_V7X
    vb_cols = (G_GB - G_VB) * LANES_V7X

    def cur(f):
        f = jnp.minimum(f, n_steps - 1)
        return f // (n_t * n_j), (f // n_j) % n_t, f % n_j

    def prev(f):
        return cur(jnp.maximum(f - 1, 0))

    def cast_spec(w):
        blocks = _cast_blocks(w.shape[0], n_steps + 1)
        return pl.BlockSpec((w.shape[0] // blocks, w.shape[1]),
                            lambda f: (jnp.minimum(f, blocks - 1), 0))

    cast_specs = [cast_spec(w) for w in cast_weights]
    outs = pl.pallas_call(
        functools.partial(_in_proj_kernel, n_steps=n_steps, n_j=n_j, n_cast=len(cast_weights)),
        grid=(n_steps + 1,),
        in_specs=[
            pl.BlockSpec((1, TM_IN, D), lambda f: (cur(f)[0], cur(f)[1], 0)),
            pl.BlockSpec((1, D), lambda f: (0, 0)),
            pl.BlockSpec((D, TN_IN), lambda f: (0, cur(f)[2])),
            pl.BlockSpec((1, TN_IN), lambda f: (0, prev(f)[2])),
        ] + cast_specs,
        out_specs=[
            pl.BlockSpec((1, groups, TM_IN, LANES_V7X),
                         lambda f: (prev(f)[0], prev(f)[2], prev(f)[1], 0)),
            pl.BlockSpec((1, TM_IN, vb_cols), lambda f: (prev(f)[0], prev(f)[1], 0)),
        ] + cast_specs,
        out_shape=[
            jax.ShapeDtypeStruct((B, N_GROUPS, S, LANES_V7X), BF16),
            jax.ShapeDtypeStruct((B, S, vb_cols), F32),
        ] + [jax.ShapeDtypeStruct(w.shape, BF16) for w in cast_weights],
        scratch_shapes=[pltpu.VMEM((TM_IN, D), BF16), pltpu.VMEM((TM_IN, TN_IN), F32)],
        compiler_params=_params(("arbitrary",)),
        name="in_proj",
    )(x, norm_gain, w_in_bf, gain_all, *cast_weights)
    return outs[0], outs[1], outs[2:]


def _mem_kv_kernel(mem_ref, mg_ref, w_ref, kg_ref, o_ref):
    mh = (_rms_rows(mem_ref[0]) * mg_ref[...]).astype(BF16)
    kv = jnp.dot(mh, w_ref[...], preferred_element_type=F32)
    for g in range(M_HEADS):
        sl = slice(g * HEAD_DIM, (g + 1) * HEAD_DIM)
        o_ref[0, g] = (_rms_rows(kv[:, sl]) * kg_ref[:, sl]).astype(BF16)
    for g in range(M_HEADS, 2 * M_HEADS):
        o_ref[0, g] = kv[:, g * HEAD_DIM:(g + 1) * HEAD_DIM].astype(BF16)


def _mem_kv(mem, mem_norm_gain, w_kv_bf, k_gain):
    B, N, D = mem.shape
    W = w_kv_bf.shape[1]
    return pl.pallas_call(
        _mem_kv_kernel,
        grid=(B,),
        in_specs=[
            pl.BlockSpec((1, N, D), lambda b: (b, 0, 0)),
            pl.BlockSpec((1, D), lambda b: (0, 0)),
            pl.BlockSpec((D, W), lambda b: (0, 0)),
            pl.BlockSpec((1, W // 2), lambda b: (0, 0)),
        ],
        out_specs=pl.BlockSpec((1, 2 * M_HEADS, N, HEAD_DIM), lambda b: (b, 0, 0, 0)),
        out_shape=jax.ShapeDtypeStruct((B, 2 * M_HEADS, N, HEAD_DIM), BF16),
        compiler_params=_params(("arbitrary",)),
        name="mem_kv",
    )(mem, mem_norm_gain, w_kv_bf, k_gain)


_BAND = N_KT * TQ
_RWRAP = _BAND + TQ


def _rel_bias_kernel(r_ref, o_ref):
    row = lax.broadcasted_iota(jnp.int32, (TQ, _RWRAP), 0)
    row_b = lax.broadcasted_iota(jnp.int32, (TQ, _BAND), 0)
    col_b = lax.broadcasted_iota(jnp.int32, (TQ, _BAND), 1)
    lo = jnp.where(row_b < CHUNK, 0, CHUNK)
    valid = (col_b >= lo) & (col_b < lo + _BAND - CHUNK)
    for h in range(A_HEADS):
        t = jnp.broadcast_to(r_ref[h:h + 1, :], (TQ, _RWRAP))
        for k in range(7):
            t = jnp.where(((row >> k) & 1) == 1, pltpu.roll(t, 1 << k, 1), t)
        o_ref[h] = jnp.where(valid, t[:, :_BAND] * LOG2_E, NEG_INF)


def _rel_bias_tiles(rel_bias):
    H = rel_bias.shape[0]
    edge = jnp.broadcast_to(rel_bias[:, 2 * REL_CLIP:], (H, 2 * REL_CLIP))
    mid = jnp.flip(rel_bias[:, 2 * REL_CLIP + 1 - (_BAND - REL_CLIP):], axis=1)
    r_ext = jnp.concatenate([edge[:, :REL_CLIP], mid, edge[:, :_RWRAP - _BAND]], axis=1)
    return pl.pallas_call(
        _rel_bias_kernel,
        out_shape=jax.ShapeDtypeStruct((H, TQ, _BAND), F32),
        compiler_params=pltpu.CompilerParams(vmem_limit_bytes=VMEM_LIMIT_V7X),
        name="rel_bias",
    )(r_ext)


def _attention_pipeline(n_groups, tiles, scores_fn, values_fn, s_scr, p_scr, l_scr):
    rows, width = l_scr.shape[1:]
    s_scr[...] = jnp.zeros(s_scr.shape, F32)
    p_scr[...] = jnp.zeros(p_scr.shape, BF16)
    l_scr[...] = jnp.ones(l_scr.shape, F32)

    def body(j, carry):
        g3 = jnp.clip(j - 2, 0, n_groups - 1)
        for u in range(tiles):
            values_fn(g3, u, p_scr[u], l_scr[u])
        for u in range(tiles):
            s = s_scr[u]
            p = jnp.exp2(s - jnp.max(s, axis=-1, keepdims=True))
            l_scr[u] = jnp.broadcast_to(jnp.sum(p, axis=-1, keepdims=True), (rows, width))
            p_scr[u] = p.astype(BF16)
        g1 = jnp.minimum(j, n_groups - 1)
        for u in range(tiles):
            s_scr[u] = scores_fn(g1, u)
        return carry

    lax.fori_loop(0, n_groups + 2, body, 0)


def _attention_scratch(tiles, rows, cols):
    return [pltpu.VMEM((tiles, rows, cols), F32), pltpu.VMEM((tiles, rows, cols), BF16),
            pltpu.VMEM((tiles, rows, HEAD_DIM), F32)]


_QK_DIMS = (((1,), (1,)), ((), ()))


def _band_attn_kernel(q_ref, k_ref, v_ref, g_ref, bias_ref, o_ref, *scratch):
    groups_per_head = q_ref.shape[2] // (TQ * ATTN_TILES_PER_STEP)
    col = lax.broadcasted_iota(jnp.int32, (TQ, _BAND), 1)

    def coords(group, u):
        head = group // groups_per_head
        qt = (group % groups_per_head) * ATTN_TILES_PER_STEP + u
        return head, qt, pl.multiple_of(qt * TQ, TQ)

    def band(ref, head, qt):
        tiles = []
        for a in range(N_KT):
            k0 = pl.multiple_of(jnp.maximum(qt - (N_KT - 1) + a, 0) * TQ, TQ)
            tiles.append(ref[0, head, pl.ds(k0, TQ), :])
        return jnp.concatenate(tiles, axis=0)

    def scores(group, u):
        head, qt, q0 = coords(group, u)
        s = lax.dot_general(q_ref[0, head, pl.ds(q0, TQ), :], band(k_ref, head, qt), _QK_DIMS,
                            preferred_element_type=F32)
        s = s + bias_ref[head]
        return jnp.where(col >= (N_KT - 1 - qt) * TQ, s, NEG_INF)

    def values(group, u, p, l):
        head, qt, q0 = coords(group, u)
        o = jnp.dot(p, band(v_ref, head, qt), preferred_element_type=F32) / l
        o_ref[0, head, pl.ds(q0, TQ), :] = (
            o * g_ref[0, head, pl.ds(q0, TQ), :].astype(F32)).astype(BF16)

    _attention_pipeline(ATTN_HEADS_PER_STEP * groups_per_head, ATTN_TILES_PER_STEP,
                        scores, values, *scratch)


def _band_attn(p, bias):
    B, _, S, _ = p.shape
    hb = ATTN_HEADS_PER_STEP
    blk = (1, hb, S, HEAD_DIM)

    def seg(g_first):
        return pl.BlockSpec(blk, lambda b, h: (b, g_first // hb + h, 0, 0))

    return pl.pallas_call(
        _band_attn_kernel,
        grid=(B, A_HEADS // hb),
        in_specs=[seg(G_QA), seg(G_KA), seg(G_VA), seg(G_GA),
                  pl.BlockSpec((hb, TQ, _BAND), lambda b, h: (h, 0, 0))],
        out_specs=seg(0),
        out_shape=jax.ShapeDtypeStruct((B, A_HEADS, S, HEAD_DIM), BF16),
        scratch_shapes=_attention_scratch(ATTN_TILES_PER_STEP, TQ, _BAND),
        compiler_params=_params(("parallel", "arbitrary")),
        name="band_attn",
    )(p, p, p, p, bias)


def _mem_attn_kernel(q_ref, g_ref, kv_ref, o_ref, *scratch):
    groups_per_head = q_ref.shape[2] // (TQ_MEM * MEM_TILES_PER_STEP)

    def coords(group, u):
        head = group // groups_per_head
        qt = (group % groups_per_head) * MEM_TILES_PER_STEP + u
        return head, pl.multiple_of(qt * TQ_MEM, TQ_MEM)

    def scores(group, u):
        head, q0 = coords(group, u)
        return lax.dot_general(q_ref[0, head, pl.ds(q0, TQ_MEM), :], kv_ref[0, head], _QK_DIMS,
                               preferred_element_type=F32)

    def values(group, u, p, l):
        head, q0 = coords(group, u)
        o = jnp.dot(p, kv_ref[0, M_HEADS + head], preferred_element_type=F32) / l
        o_ref[0, head, pl.ds(q0, TQ_MEM), :] = (
            o * g_ref[0, head, pl.ds(q0, TQ_MEM), :].astype(F32)).astype(BF16)

    _attention_pipeline(M_HEADS * groups_per_head, MEM_TILES_PER_STEP,
                        scores, values, *scratch)


def _mem_attn(p, memkv):
    B, _, S, _ = p.shape
    N = memkv.shape[2]
    blk = (1, M_HEADS, S, HEAD_DIM)
    return pl.pallas_call(
        _mem_attn_kernel,
        grid=(B,),
        in_specs=[
            pl.BlockSpec(blk, lambda b: (b, G_QM // M_HEADS, 0, 0)),
            pl.BlockSpec(blk, lambda b: (b, G_GM // M_HEADS, 0, 0)),
            pl.BlockSpec((1, 2 * M_HEADS, N, HEAD_DIM), lambda b: (b, 0, 0, 0)),
        ],
        out_specs=pl.BlockSpec(blk, lambda b: (b, 0, 0, 0)),
        out_shape=jax.ShapeDtypeStruct((B, M_HEADS, S, HEAD_DIM), BF16),
        scratch_shapes=_attention_scratch(MEM_TILES_PER_STEP, TQ_MEM, N),
        compiler_params=_params(("arbitrary",)),
        name="mem_attn",
    )(p, p, memkv)


def _pool_mix_kernel(vb_ref, prev_ref, g_ref, pw_ref, ps_ref, o_ref):
    t = pl.program_id(1)
    cur = vb_ref[0]
    prev = jnp.where(t > 0, prev_ref[0], 0.0)
    pos = t * TR_POOL + lax.broadcasted_iota(jnp.int32, (TR_POOL, 1), 0)
    for g, w in enumerate(POOL_WINDOWS):
        sl = slice(g * LANES_V7X, (g + 1) * LANES_V7X)
        acc = jnp.concatenate([prev[:, sl], cur[:, sl]], axis=0)
        d = 1
        while d < w:
            acc = acc + pltpu.roll(acc, d, 0)
            d *= 2
        cnt = jnp.minimum(pos + 1, w).astype(F32)
        pooled = acc[POOL_HALO:, :] / cnt - cur[:, sl]
        mixed = jnp.dot(pooled.astype(BF16), pw_ref[g], preferred_element_type=F32)
        o_ref[0, g] = (mixed * ps_ref[:, sl] * g_ref[0, g].astype(F32)).astype(BF16)


def _pool_mix(vb, p, pool_w_bf, pool_scale):
    B, S, P = vb.shape
    G = len(POOL_WINDOWS)
    halo_blocks = TR_POOL // POOL_HALO
    return pl.pallas_call(
        _pool_mix_kernel,
        grid=(B, S // TR_POOL),
        in_specs=[
            pl.BlockSpec((1, TR_POOL, P), lambda b, t: (b, t, 0)),
            pl.BlockSpec((1, POOL_HALO, P), lambda b, t: (b, jnp.maximum(t * halo_blocks - 1, 0), 0)),
            pl.BlockSpec((1, G, TR_POOL, LANES_V7X), lambda b, t: (b, G_GB // G, t, 0)),
            pl.BlockSpec((G, LANES_V7X, LANES_V7X), lambda b, t: (0, 0, 0)),
            pl.BlockSpec((1, P), lambda b, t: (0, 0)),
        ],
        out_specs=pl.BlockSpec((1, G, TR_POOL, LANES_V7X), lambda b, t: (b, 0, t, 0)),
        out_shape=jax.ShapeDtypeStruct((B, G, S, LANES_V7X), BF16),
        compiler_params=_params(("parallel", "arbitrary")),
        name="pool_mix",
    )(vb, vb, p, pool_w_bf, pool_scale)


def _merge_out_kernel(x_ref, ng_ref, oa_ref, ob_ref, om_ref,
                      wg0_ref, wg1_ref, wg2_ref, b0_ref, b1_ref, b2_ref,
                      wa_ref, wb_ref, wm_ref, wo_ref, out_ref, h_scr):
    n = pl.program_id(2)

    @pl.when(n == 0)
    def _():
        x = x_ref[0]
        h_scr[...] = (_rms_rows(x) * ng_ref[...]).astype(BF16)
        out_ref[0] = x

    h = h_scr[...]

    def branch(o_ref, w_ref, wg_ref, b_ref):
        o = jnp.concatenate([o_ref[0, g] for g in range(o_ref.shape[1])], axis=-1)
        gate = _sigmoid(jnp.dot(h, wg_ref[...], preferred_element_type=F32) + b_ref[...])
        return gate * jnp.dot(o, w_ref[...], preferred_element_type=F32)

    y = (branch(oa_ref, wa_ref, wg0_ref, b0_ref)
         + branch(ob_ref, wb_ref, wg1_ref, b1_ref)
         + branch(om_ref, wm_ref, wg2_ref, b2_ref))
    out_ref[0] += jnp.dot(y.astype(BF16), wo_ref[...], preferred_element_type=F32)


def _merge_out(x, norm_gain, oga, ogb, ogm, w_merge_bf, b_merge, wa_bf, wb_bf, wm_bf, wo_bf):
    B, S, D = x.shape
    n_c = D // CN_OUT

    def gate_w(r):
        return pl.BlockSpec((D, CN_OUT), lambda b, t, n: (0, r * n_c + n))

    def gate_b(r):
        return pl.BlockSpec((1, CN_OUT), lambda b, t, n: (0, r * n_c + n))

    def heads(nh):
        return pl.BlockSpec((1, nh, TM_OUT, HEAD_DIM), lambda b, t, n: (b, 0, t, 0))

    def cols(rows):
        return pl.BlockSpec((rows, CN_OUT), lambda b, t, n: (0, n))

    return pl.pallas_call(
        _merge_out_kernel,
        grid=(B, S // TM_OUT, n_c),
        in_specs=[
            pl.BlockSpec((1, TM_OUT, D), lambda b, t, n: (b, t, 0)),
            pl.BlockSpec((1, D), lambda b, t, n: (0, 0)),
            heads(oga.shape[1]), heads(ogb.shape[1]), heads(ogm.shape[1]),
            gate_w(0), gate_w(1), gate_w(2), gate_b(0), gate_b(1), gate_b(2),
            cols(wa_bf.shape[0]), cols(wb_bf.shape[0]), cols(wm_bf.shape[0]),
            pl.BlockSpec((CN_OUT, D), lambda b, t, n: (n, 0)),
        ],
        out_specs=pl.BlockSpec((1, TM_OUT, D), lambda b, t, n: (b, t, 0)),
        out_shape=jax.ShapeDtypeStruct((B, S, D), F32),
        scratch_shapes=[pltpu.VMEM((TM_OUT, D), BF16)],
        compiler_params=_params(("parallel", "arbitrary", "arbitrary")),
        name="merge_out",
    )(x, norm_gain, oga, ogb, ogm, w_merge_bf, w_merge_bf, w_merge_bf,
      b_merge, b_merge, b_merge, wa_bf, wb_bf, wm_bf, wo_bf)


def kernel(x, mem, norm_gain, mem_norm_gain, w_in, w_merge, b_merge, a_q_gain, a_k_gain,
           a_rel_bias, pool_w, pool_scale, w_mem_kv, m_q_gain, m_k_gain,
           w_branch_a, w_branch_b, w_branch_m, w_out):
    depth = w_in.shape[0]
    scale = HEAD_DIM ** -0.5 * LOG2_E
    for l in range(depth):
        a_w, p_w, m_w = w_branch_a.shape[1], w_branch_b.shape[1], w_branch_m.shape[1]
        gain_all = jnp.concatenate([
            a_q_gain[l].reshape(1, a_w) * scale, a_k_gain[l].reshape(1, a_w),
            jnp.ones((1, 2 * a_w + 2 * p_w), F32),
            m_q_gain[l].reshape(1, m_w) * scale, jnp.ones((1, m_w), F32)], axis=1)
        ng = norm_gain[l:l + 1]

        p, vb, (wg_bf, wo_bf, wa_bf, wb_bf, wm_bf, wkv_bf) = _in_proj(
            x, ng, w_in[l].astype(BF16), gain_all,
            [w_merge[l], w_out[l], w_branch_a[l], w_branch_b[l], w_branch_m[l], w_mem_kv[l]])
        memkv = _mem_kv(mem, mem_norm_gain[l:l + 1], wkv_bf, m_k_gain[l].reshape(1, m_w))
        bias = _rel_bias_tiles(a_rel_bias[l])
        oga = _band_attn(p, bias)
        ogm = _mem_attn(p, memkv)
        ogb = _pool_mix(vb, p, pool_w[l].astype(BF16), pool_scale[l:l + 1])
        x = _merge_out(x, ng, oga, ogb, ogm, wg_bf, b_merge[l:l + 1], wa_bf, wb_bf, wm_bf, wo_bf)
    return x
```

```python
import functools

import jax
import jax.numpy as jnp
from jax import lax
from jax.experimental import pallas as pl
from jax.experimental.pallas import tpu as pltpu

F32 = jnp.float32
BF16 = jnp.bfloat16

CHUNK = 64
N_LEFT_CHUNKS = 8
A_HEADS = 8
M_HEADS = 4
HEAD_DIM = 128
REL_CLIP = 256
POOL_WINDOWS = (2, 4, 8, 16)
EPS = 1e-6
NEG_INF = -1e30
LOG2_E = 1.4426950408889634

LANES_V7X = 128
MXU_COLS_V7X = 256
VMEM_LIMIT_V7X = 60000 * 1024

TM_IN = 1024
TN_IN = 1536
IN_K_CHUNKS = 8
TQ = 128
N_KT = 5
ATTN_TILES_PER_STEP = 4
ATTN_HEADS_PER_STEP = 4
TQ_MEM = 256
MEM_TILES_PER_STEP = 2
TR_POOL = 2048
POOL_HALO = 16
TM_OUT = 512
CN_OUT = 512

G_QA, G_KA, G_VA, G_GA, G_VB, G_GB, G_QM, G_GM = 0, 8, 16, 24, 32, 36, 40, 44
N_GROUPS = 48


def _params(sem):
    return pltpu.CompilerParams(dimension_semantics=sem, vmem_limit_bytes=VMEM_LIMIT_V7X)


def _rms_rows(t):
    return t * lax.rsqrt(jnp.mean(t * t, axis=-1, keepdims=True) + EPS)


def _sigmoid(z):
    return 1.0 / (1.0 + jnp.exp(-z))


def _tied_zero(src, shape, dtype):
    rows = src.shape[0]
    folded = jnp.sum(src.reshape(rows // 8, 8, src.shape[1]), axis=0)
    bits = pltpu.bitcast(folded, jnp.uint32)
    zero = pltpu.bitcast((bits >> 16) >> 16, F32)
    return jnp.tile(zero, (shape[0] // 8, shape[1] // src.shape[1])).astype(dtype)


def _in_proj_kernel(*refs, n_steps, n_j, n_cast):
    x_ref, ng_ref, w_ref, gain_ref = refs[:4]
    cast_src = refs[4:4 + n_cast]
    p_ref, vb_ref = refs[4 + n_cast:6 + n_cast]
    cast_dst = refs[6 + n_cast:6 + 2 * n_cast]
    h_scr, acc_scr = refs[6 + 2 * n_cast:]
    f = pl.program_id(0)
    groups = TN_IN // LANES_V7X
    j_cur = jnp.minimum(f, n_steps - 1) % n_j
    g_prev = (jnp.maximum(f - 1, 0) % n_j) * groups

    @pl.when(f == 0)
    def _():
        acc_scr[...] = jnp.zeros(acc_scr.shape, F32)

    @pl.when((j_cur == 0) & (f < n_steps))
    def _():
        h_scr[...] = (_rms_rows(x_ref[0]) * ng_ref[...]).astype(BF16)

    @pl.when(g_prev == G_VB // groups * groups)
    def _():
        vb0 = G_VB % groups * LANES_V7X
        vb_ref[0] = acc_scr[:, vb0:vb0 + vb_ref.shape[2]]

    k_chunk = h_scr.shape[1] // IN_K_CHUNKS
    pieces = [(n0, c) for n0 in range(0, TN_IN, MXU_COLS_V7X) for c in range(IN_K_CHUNKS)]
    stride = len(pieces) // groups
    accs = {}
    ties = {}
    for i, (n0, c) in enumerate(pieces):
        ks = slice(c * k_chunk, (c + 1) * k_chunk)
        lhs = h_scr[:, ks]
        if i in ties:
            zero = _tied_zero(ties.pop(i), (16, k_chunk), BF16)
            lhs = jnp.concatenate([lhs[:16] + zero, lhs[16:]], axis=0)
        part = jnp.dot(lhs, w_ref[ks, n0:n0 + MXU_COLS_V7X], preferred_element_type=F32)
        accs[n0] = part if c == 0 else accs[n0] + part
        if i % stride == 0:
            g = i // stride
            sl = slice(g * LANES_V7X, (g + 1) * LANES_V7X)
            a = acc_scr[:, sl]
            gi = g_prev + g
            is_norm = (gi < G_VA) | ((gi >= G_QM) & (gi < G_GM))
            is_silu = ((gi >= G_GA) & (gi < G_VB)) | ((gi >= G_GB) & (gi < G_QM)) | (gi >= G_GM)
            normed = _rms_rows(a) * gain_ref[:, sl]
            res = jnp.where(is_norm, normed, jnp.where(is_silu, a * _sigmoid(a), a))
            p_ref[0, g] = res.astype(BF16)
            ties[i + stride - 1] = res
    for n0, acc in accs.items():
        acc_scr[:, n0:n0 + MXU_COLS_V7X] = acc
    for src, dst in zip(cast_src, cast_dst):
        dst[...] = src[...].astype(BF16)


def _cast_blocks(rows, n_grid):
    blocks = 1
    while blocks * 2 <= n_grid and rows % (blocks * 2 * 16) == 0:
        blocks *= 2
    return blocks


def _in_proj(x, norm_gain, w_in_bf, gain_all, cast_weights):
    B, S, D = x.shape
    n_t = S // TM_IN
    n_j = w_in_bf.shape[1] // TN_IN
    n_steps = B * n_t * n_j
    groups = TN_IN // LANES_V7X
    vb_cols = (G_GB - G_VB) * LANES_V7X

    def cur(f):
        f = jnp.minimum(f, n_steps - 1)
        return f // (n_t * n_j), (f // n_j) % n_t, f % n_j

    def prev(f):
        return cur(jnp.maximum(f - 1, 0))

    def cast_spec(w):
        blocks = _cast_blocks(w.shape[0], n_steps + 1)
        return pl.BlockSpec((w.shape[0] // blocks, w.shape[1]),
                            lambda f: (jnp.minimum(f, blocks - 1), 0))

    cast_specs = [cast_spec(w) for w in cast_weights]
    outs = pl.pallas_call(
        functools.partial(_in_proj_kernel, n_steps=n_steps, n_j=n_j, n_cast=len(cast_weights)),
        grid=(n_steps + 1,),
        in_specs=[
            pl.BlockSpec((1, TM_IN, D), lambda f: (cur(f)[0], cur(f)[1], 0)),
            pl.BlockSpec((1, D), lambda f: (0, 0)),
            pl.BlockSpec((D, TN_IN), lambda f: (0, cur(f)[2])),
            pl.BlockSpec((1, TN_IN), lambda f: (0, prev(f)[2])),
        ] + cast_specs,
        out_specs=[
            pl.BlockSpec((1, groups, TM_IN, LANES_V7X),
                         lambda f: (prev(f)[0], prev(f)[2], prev(f)[1], 0)),
            pl.BlockSpec((1, TM_IN, vb_cols), lambda f: (prev(f)[0], prev(f)[1], 0)),
        ] + cast_specs,
        out_shape=[
            jax.ShapeDtypeStruct((B, N_GROUPS, S, LANES_V7X), BF16),
            jax.ShapeDtypeStruct((B, S, vb_cols), F32),
        ] + [jax.ShapeDtypeStruct(w.shape, BF16) for w in cast_weights],
        scratch_shapes=[pltpu.VMEM((TM_IN, D), BF16), pltpu.VMEM((TM_IN, TN_IN), F32)],
        compiler_params=_params(("arbitrary",)),
        name="in_proj",
    )(x, norm_gain, w_in_bf, gain_all, *cast_weights)
    return outs[0], outs[1], outs[2:]


def _mem_kv_kernel(mem_ref, mg_ref, w_ref, kg_ref, o_ref):
    mh = (_rms_rows(mem_ref[0]) * mg_ref[...]).astype(BF16)
    kv = jnp.dot(mh, w_ref[...], preferred_element_type=F32)
    for g in range(M_HEADS):
        sl = slice(g * HEAD_DIM, (g + 1) * HEAD_DIM)
        o_ref[0, g] = (_rms_rows(kv[:, sl]) * kg_ref[:, sl]).astype(BF16)
    for g in range(M_HEADS, 2 * M_HEADS):
        o_ref[0, g] = kv[:, g * HEAD_DIM:(g + 1) * HEAD_DIM].astype(BF16)


def _mem_kv(mem, mem_norm_gain, w_kv_bf, k_gain):
    B, N, D = mem.shape
    W = w_kv_bf.shape[1]
    return pl.pallas_call(
        _mem_kv_kernel,
        grid=(B,),
        in_specs=[
            pl.BlockSpec((1, N, D), lambda b: (b, 0, 0)),
            pl.BlockSpec((1, D), lambda b: (0, 0)),
            pl.BlockSpec((D, W), lambda b: (0, 0)),
            pl.BlockSpec((1, W // 2), lambda b: (0, 0)),
        ],
        out_specs=pl.BlockSpec((1, 2 * M_HEADS, N, HEAD_DIM), lambda b: (b, 0, 0, 0)),
        out_shape=jax.ShapeDtypeStruct((B, 2 * M_HEADS, N, HEAD_DIM), BF16),
        compiler_params=_params(("arbitrary",)),
        name="mem_kv",
    )(mem, mem_norm_gain, w_kv_bf, k_gain)


_BAND = N_KT * TQ
_RWRAP = _BAND + TQ


def _rel_bias_kernel(r_ref, o_ref):
    row = lax.broadcasted_iota(jnp.int32, (TQ, _RWRAP), 0)
    row_b = lax.broadcasted_iota(jnp.int32, (TQ, _BAND), 0)
    col_b = lax.broadcasted_iota(jnp.int32, (TQ, _BAND), 1)
    lo = jnp.where(row_b < CHUNK, 0, CHUNK)
    valid = (col_b >= lo) & (col_b < lo + _BAND - CHUNK)
    for h in range(A_HEADS):
        t = jnp.broadcast_to(r_ref[h:h + 1, :], (TQ, _RWRAP))
        for k in range(7):
            t = jnp.where(((row >> k) & 1) == 1, pltpu.roll(t, 1 << k, 1), t)
        o_ref[h] = jnp.where(valid, t[:, :_BAND] * LOG2_E, NEG_INF)


def _rel_bias_tiles(rel_bias):
    H = rel_bias.shape[0]
    edge = jnp.broadcast_to(rel_bias[:, 2 * REL_CLIP:], (H, 2 * REL_CLIP))
    mid = jnp.flip(rel_bias[:, 2 * REL_CLIP + 1 - (_BAND - REL_CLIP):], axis=1)
    r_ext = jnp.concatenate([edge[:, :REL_CLIP], mid, edge[:, :_RWRAP - _BAND]], axis=1)
    return pl.pallas_call(
        _rel_bias_kernel,
        out_shape=jax.ShapeDtypeStruct((H, TQ, _BAND), F32),
        compiler_params=pltpu.CompilerParams(vmem_limit_bytes=VMEM_LIMIT_V7X),
        name="rel_bias",
    )(r_ext)


def _attention_pipeline(n_groups, tiles, scores_fn, values_fn, s_scr, p_scr, l_scr):
    rows, width = l_scr.shape[1:]
    s_scr[...] = jnp.zeros(s_scr.shape, F32)
    p_scr[...] = jnp.zeros(p_scr.shape, BF16)
    l_scr[...] = jnp.ones(l_scr.shape, F32)

    def body(j, carry):
        g3 = jnp.clip(j - 2, 0, n_groups - 1)
        for u in range(tiles):
            values_fn(g3, u, p_scr[u], l_scr[u])
        for u in range(tiles):
            s = s_scr[u]
            p = jnp.exp2(s - jnp.max(s, axis=-1, keepdims=True))
            l_scr[u] = jnp.broadcast_to(jnp.sum(p, axis=-1, keepdims=True), (rows, width))
            p_scr[u] = p.astype(BF16)
        g1 = jnp.minimum(j, n_groups - 1)
        for u in range(tiles):
            s_scr[u] = scores_fn(g1, u)
        return carry

    lax.fori_loop(0, n_groups + 2, body, 0)


def _attention_scratch(tiles, rows, cols):
    return [pltpu.VMEM((tiles, rows, cols), F32), pltpu.VMEM((tiles, rows, cols), BF16),
            pltpu.VMEM((tiles, rows, HEAD_DIM), F32)]


_QK_DIMS = (((1,), (1,)), ((), ()))


def _band_attn_kernel(q_ref, k_ref, v_ref, g_ref, bias_ref, o_ref, *scratch):
    groups_per_head = q_ref.shape[2] // (TQ * ATTN_TILES_PER_STEP)
    col = lax.broadcasted_iota(jnp.int32, (TQ, _BAND), 1)

    def coords(group, u):
        head = group // groups_per_head
        qt = (group % groups_per_head) * ATTN_TILES_PER_STEP + u
        return head, qt, pl.multiple_of(qt * TQ, TQ)

    def band(ref, head, qt):
        tiles = []
        for a in range(N_KT):
            k0 = pl.multiple_of(jnp.maximum(qt - (N_KT - 1) + a, 0) * TQ, TQ)
            tiles.append(ref[0, head, pl.ds(k0, TQ), :])
        return jnp.concatenate(tiles, axis=0)

    def scores(group, u):
        head, qt, q0 = coords(group, u)
        s = lax.dot_general(q_ref[0, head, pl.ds(q0, TQ), :], band(k_ref, head, qt), _QK_DIMS,
                            preferred_element_type=F32)
        s = s + bias_ref[head]
        return jnp.where(col >= (N_KT - 1 - qt) * TQ, s, NEG_INF)

    def values(group, u, p, l):
        head, qt, q0 = coords(group, u)
        o = jnp.dot(p, band(v_ref, head, qt), preferred_element_type=F32) / l
        o_ref[0, head, pl.ds(q0, TQ), :] = (
            o * g_ref[0, head, pl.ds(q0, TQ), :].astype(F32)).astype(BF16)

    _attention_pipeline(ATTN_HEADS_PER_STEP * groups_per_head, ATTN_TILES_PER_STEP,
                        scores, values, *scratch)


def _band_attn(p, bias):
    B, _, S, _ = p.shape
    hb = ATTN_HEADS_PER_STEP
    blk = (1, hb, S, HEAD_DIM)

    def seg(g_first):
        return pl.BlockSpec(blk, lambda b, h: (b, g_first // hb + h, 0, 0))

    return pl.pallas_call(
        _band_attn_kernel,
        grid=(B, A_HEADS // hb),
        in_specs=[seg(G_QA), seg(G_KA), seg(G_VA), seg(G_GA),
                  pl.BlockSpec((hb, TQ, _BAND), lambda b, h: (h, 0, 0))],
        out_specs=seg(0),
        out_shape=jax.ShapeDtypeStruct((B, A_HEADS, S, HEAD_DIM), BF16),
        scratch_shapes=_attention_scratch(ATTN_TILES_PER_STEP, TQ, _BAND),
        compiler_params=_params(("parallel", "arbitrary")),
        name="band_attn",
    )(p, p, p, p, bias)


def _mem_attn_kernel(q_ref, g_ref, kv_ref, o_ref, *scratch):
    groups_per_head = q_ref.shape[2] // (TQ_MEM * MEM_TILES_PER_STEP)

    def coords(group, u):
        head = group // groups_per_head
        qt = (group % groups_per_head) * MEM_TILES_PER_STEP + u
        return head, pl.multiple_of(qt * TQ_MEM, TQ_MEM)

    def scores(group, u):
        head, q0 = coords(group, u)
        return lax.dot_general(q_ref[0, head, pl.ds(q0, TQ_MEM), :], kv_ref[0, head], _QK_DIMS,
                               preferred_element_type=F32)

    def values(group, u, p, l):
        head, q0 = coords(group, u)
        o = jnp.dot(p, kv_ref[0, M_HEADS + head], preferred_element_type=F32) / l
        o_ref[0, head, pl.ds(q0, TQ_MEM), :] = (
            o * g_ref[0, head, pl.ds(q0, TQ_MEM), :].astype(F32)).astype(BF16)

    _attention_pipeline(M_HEADS * groups_per_head, MEM_TILES_PER_STEP,
                        scores, values, *scratch)


def _mem_attn(p, memkv):
    B, _, S, _ = p.shape
    N = memkv.shape[2]
    blk = (1, M_HEADS, S, HEAD_DIM)
    return pl.pallas_call(
        _mem_attn_kernel,
        grid=(B,),
        in_specs=[
            pl.BlockSpec(blk, lambda b: (b, G_QM // M_HEADS, 0, 0)),
            pl.BlockSpec(blk, lambda b: (b, G_GM // M_HEADS, 0, 0)),
            pl.BlockSpec((1, 2 * M_HEADS, N, HEAD_DIM), lambda b: (b, 0, 0, 0)),
        ],
        out_specs=pl.BlockSpec(blk, lambda b: (b, 0, 0, 0)),
        out_shape=jax.ShapeDtypeStruct((B, M_HEADS, S, HEAD_DIM), BF16),
        scratch_shapes=_attention_scratch(MEM_TILES_PER_STEP, TQ_MEM, N),
        compiler_params=_params(("arbitrary",)),
        name="mem_attn",
    )(p, p, memkv)


def _pool_mix_kernel(vb_ref, prev_ref, g_ref, pw_ref, ps_ref, o_ref):
    t = pl.program_id(1)
    cur = vb_ref[0]
    prev = jnp.where(t > 0, prev_ref[0], 0.0)
    pos = t * TR_POOL + lax.broadcasted_iota(jnp.int32, (TR_POOL, 1), 0)
    for g, w in enumerate(POOL_WINDOWS):
        sl = slice(g * LANES_V7X, (g + 1) * LANES_V7X)
        acc = jnp.concatenate([prev[:, sl], cur[:, sl]], axis=0)
        d = 1
        while d < w:
            acc = acc + pltpu.roll(acc, d, 0)
            d *= 2
        cnt = jnp.minimum(pos + 1, w).astype(F32)
        pooled = acc[POOL_HALO:, :] / cnt - cur[:, sl]
        mixed = jnp.dot(pooled.astype(BF16), pw_ref[g], preferred_element_type=F32)
        o_ref[0, g] = (mixed * ps_ref[:, sl] * g_ref[0, g].astype(F32)).astype(BF16)


def _pool_mix(vb, p, pool_w_bf, pool_scale):
    B, S, P = vb.shape
    G = len(POOL_WINDOWS)
    halo_blocks = TR_POOL // POOL_HALO
    return pl.pallas_call(
        _pool_mix_kernel,
        grid=(B, S // TR_POOL),
        in_specs=[
            pl.BlockSpec((1, TR_POOL, P), lambda b, t: (b, t, 0)),
            pl.BlockSpec((1, POOL_HALO, P), lambda b, t: (b, jnp.maximum(t * halo_blocks - 1, 0), 0)),
            pl.BlockSpec((1, G, TR_POOL, LANES_V7X), lambda b, t: (b, G_GB // G, t, 0)),
            pl.BlockSpec((G, LANES_V7X, LANES_V7X), lambda b, t: (0, 0, 0)),
            pl.BlockSpec((1, P), lambda b, t: (0, 0)),
        ],
        out_specs=pl.BlockSpec((1, G, TR_POOL, LANES_V7X), lambda b, t: (b, 0, t, 0)),
        out_shape=jax.ShapeDtypeStruct((B, G, S, LANES_V7X), BF16),
        compiler_params=_params(("parallel", "arbitrary")),
        name="pool_mix",
    )(vb, vb, p, pool_w_bf, pool_scale)


def _merge_out_kernel(x_ref, ng_ref, oa_ref, ob_ref, om_ref,
                      wg0_ref, wg1_ref, wg2_ref, b0_ref, b1_ref, b2_ref,
                      wa_ref, wb_ref, wm_ref, wo_ref, out_ref, h_scr):
    n = pl.program_id(2)

    @pl.when(n == 0)
    def _():
        x = x_ref[0]
        h_scr[...] = (_rms_rows(x) * ng_ref[...]).astype(BF16)
        out_ref[0] = x

    h = h_scr[...]

    def branch(o_ref, w_ref, wg_ref, b_ref):
        o = jnp.concatenate([o_ref[0, g] for g in range(o_ref.shape[1])], axis=-1)
        gate = _sigmoid(jnp.dot(h, wg_ref[...], preferred_element_type=F32) + b_ref[...])
        return gate * jnp.dot(o, w_ref[...], preferred_element_type=F32)

    y = (branch(oa_ref, wa_ref, wg0_ref, b0_ref)
         + branch(ob_ref, wb_ref, wg1_ref, b1_ref)
         + branch(om_ref, wm_ref, wg2_ref, b2_ref))
    out_ref[0] += jnp.dot(y.astype(BF16), wo_ref[...], preferred_element_type=F32)


def _merge_out(x, norm_gain, oga, ogb, ogm, w_merge_bf, b_merge, wa_bf, wb_bf, wm_bf, wo_bf):
    B, S, D = x.shape
    n_c = D // CN_OUT

    def gate_w(r):
        return pl.BlockSpec((D, CN_OUT), lambda b, t, n: (0, r * n_c + n))

    def gate_b(r):
        return pl.BlockSpec((1, CN_OUT), lambda b, t, n: (0, r * n_c + n))

    def heads(nh):
        return pl.BlockSpec((1, nh, TM_OUT, HEAD_DIM), lambda b, t, n: (b, 0, t, 0))

    def cols(rows):
        return pl.BlockSpec((rows, CN_OUT), lambda b, t, n: (0, n))

    return pl.pallas_call(
        _merge_out_kernel,
        grid=(B, S // TM_OUT, n_c),
        in_specs=[
            pl.BlockSpec((1, TM_OUT, D), lambda b, t, n: (b, t, 0)),
            pl.BlockSpec((1, D), lambda b, t, n: (0, 0)),
            heads(oga.shape[1]), heads(ogb.shape[1]), heads(ogm.shape[1]),
            gate_w(0), gate_w(1), gate_w(2), gate_b(0), gate_b(1), gate_b(2),
            cols(wa_bf.shape[0]), cols(wb_bf.shape[0]), cols(wm_bf.shape[0]),
            pl.BlockSpec((CN_OUT, D), lambda b, t, n: (n, 0)),
        ],
        out_specs=pl.BlockSpec((1, TM_OUT, D), lambda b, t, n: (b, t, 0)),
        out_shape=jax.ShapeDtypeStruct((B, S, D), F32),
        scratch_shapes=[pltpu.VMEM((TM_OUT, D), BF16)],
        compiler_params=_params(("parallel", "arbitrary", "arbitrary")),
        name="merge_out",
    )(x, norm_gain, oga, ogb, ogm, w_merge_bf, w_merge_bf, w_merge_bf,
      b_merge, b_merge, b_merge, wa_bf, wb_bf, wm_bf, wo_bf)


def kernel(x, mem, norm_gain, mem_norm_gain, w_in, w_merge, b_merge, a_q_gain, a_k_gain,
           a_rel_bias, pool_w, pool_scale, w_mem_kv, m_q_gain, m_k_gain,
           w_branch_a, w_branch_b, w_branch_m, w_out):
    depth = w_in.shape[0]
    scale = HEAD_DIM ** -0.5 * LOG2_E
    for l in range(depth):
        a_w, p_w, m_w = w_branch_a.shape[1], w_branch_b.shape[1], w_branch_m.shape[1]
        gain_all = jnp.concatenate([
            a_q_gain[l].reshape(1, a_w) * scale, a_k_gain[l].reshape(1, a_w),
            jnp.ones((1, 2 * a_w + 2 * p_w), F32),
            m_q_gain[l].reshape(1, m_w) * scale, jnp.ones((1, m_w), F32)], axis=1)
        ng = norm_gain[l:l + 1]

        p, vb, (wg_bf, wo_bf, wa_bf, wb_bf, wm_bf, wkv_bf) = _in_proj(
            x, ng, w_in[l].astype(BF16), gain_all,
            [w_merge[l], w_out[l], w_branch_a[l], w_branch_b[l], w_branch_m[l], w_mem_kv[l]])
        memkv = _mem_kv(mem, mem_norm_gain[l:l + 1], wkv_bf, m_k_gain[l].reshape(1, m_w))
        bias = _rel_bias_tiles(a_rel_bias[l])
        oga = _band_attn(p, bias)
        ogm = _mem_attn(p, memkv)
        ogb = _pool_mix(vb, p, pool_w[l].astype(BF16), pool_scale[l:l + 1])
        x = _merge_out(x, ng, oga, ogb, ogm, wg_bf, b_merge[l:l + 1], wa_bf, wb_bf, wm_bf, wo_bf)
    return x
```

```python
import functools

import jax
import jax.numpy as jnp
from jax import lax
from jax.experimental import pallas as pl
from jax.experimental.pallas import tpu as pltpu

F32 = jnp.float32
BF16 = jnp.bfloat16

CHUNK = 64
N_LEFT_CHUNKS = 8
A_HEADS = 8
M_HEADS = 4
HEAD_DIM = 128
REL_CLIP = 256
POOL_WINDOWS = (2, 4, 8, 16)
EPS = 1e-6
NEG_INF = -1e30
LOG2_E = 1.4426950408889634

LANES_V7X = 128
MXU_COLS_V7X = 256
VMEM_LIMIT_V7X = 60000 * 1024

TM_IN = 1024
TN_IN = 1536
IN_K_CHUNKS = 8
TQ = 128
N_KT = 5
ATTN_TILES_PER_STEP = 4
ATTN_HEADS_PER_STEP = 4
TQ_MEM = 256
MEM_TILES_PER_STEP = 2
TR_POOL = 2048
POOL_HALO = 16
TM_OUT = 512
CN_OUT = 512

G_QA, G_KA, G_VA, G_GA, G_VB, G_GB, G_QM, G_GM = 0, 8, 16, 24, 32, 36, 40, 44
N_GROUPS = 48


def _params(sem):
    return pltpu.CompilerParams(dimension_semantics=sem, vmem_limit_bytes=VMEM_LIMIT_V7X)


def _rms_rows(t):
    return t * lax.rsqrt(jnp.mean(t * t, axis=-1, keepdims=True) + EPS)


def _sigmoid(z):
    return 1.0 / (1.0 + jnp.exp(-z))


def _fold_rows(src):
    return jnp.sum(src.reshape(src.shape[0] // 8, 8, src.shape[1]), axis=0)


def _tied_zero(folded, shape, dtype):
    bits = pltpu.bitcast(folded, jnp.uint32)
    zero = pltpu.bitcast((bits >> 16) >> 16, F32)
    return jnp.tile(zero, (shape[0] // 8, shape[1] // folded.shape[1])).astype(dtype)


def _in_proj_kernel(*refs, n_steps, n_j, n_cast):
    x_ref, ng_ref, w_ref, gain_ref = refs[:4]
    cast_src = refs[4:4 + n_cast]
    p_ref, vb_ref, h_ref = refs[4 + n_cast:7 + n_cast]
    cast_dst = refs[7 + n_cast:7 + 2 * n_cast]
    h_scr, acc_scr = refs[7 + 2 * n_cast:]
    f = pl.program_id(0)
    groups = TN_IN // LANES_V7X
    k_chunk = h_scr.shape[2]
    j_cur = jnp.minimum(f, n_steps - 1) % n_j
    g_prev = (jnp.maximum(f - 1, 0) % n_j) * groups

    @pl.when(f == 0)
    def _():
        acc_scr[...] = jnp.zeros(acc_scr.shape, F32)

    @pl.when((j_cur == 0) & (f < n_steps))
    def _():
        h = (_rms_rows(x_ref[0]) * ng_ref[...]).astype(BF16)
        for c in range(IN_K_CHUNKS):
            h_scr[c] = h[:, c * k_chunk:(c + 1) * k_chunk]

    @pl.when(g_prev == G_VB // groups * groups)
    def _():
        vb0 = G_VB % groups * LANES_V7X
        vb_ref[0] = acc_scr[:, vb0:vb0 + vb_ref.shape[2]]

    pieces = [(n0, c) for n0 in range(0, TN_IN, MXU_COLS_V7X) for c in range(IN_K_CHUNKS)]
    stride = len(pieces) // groups
    accs = {}
    ties = {}
    for i, (n0, c) in enumerate(pieces):
        ks = slice(c * k_chunk, (c + 1) * k_chunk)
        lhs = h_scr[c]
        if i in ties:
            zero = _tied_zero(ties.pop(i), (16, k_chunk), BF16)
            lhs = jnp.concatenate([lhs[:16] + zero, lhs[16:]], axis=0)
        part = jnp.dot(lhs, w_ref[ks, n0:n0 + MXU_COLS_V7X], preferred_element_type=F32)
        accs[n0] = part if c == 0 else accs[n0] + part
        if i % stride == 0:
            g = i // stride
            sl = slice(g * LANES_V7X, (g + 1) * LANES_V7X)
            a = acc_scr[:, sl]
            gi = g_prev + g
            is_norm = (gi < G_VA) | ((gi >= G_QM) & (gi < G_GM))
            is_silu = ((gi >= G_GA) & (gi < G_VB)) | ((gi >= G_GB) & (gi < G_QM)) | (gi >= G_GM)
            normed = _rms_rows(a) * gain_ref[:, sl]
            res = jnp.where(is_norm, normed, jnp.where(is_silu, a * _sigmoid(a), a))
            p_ref[0, g] = res.astype(BF16)
            ties[i + stride - 1] = _fold_rows(res)
    for n0, acc in accs.items():
        acc_scr[:, n0:n0 + MXU_COLS_V7X] = acc
    for src, dst in zip(cast_src, cast_dst):
        dst[...] = src[...].astype(BF16)
    per_step = IN_K_CHUNKS // n_j
    h_ref[0] = jnp.concatenate([h_scr[j_cur * per_step + i] for i in range(per_step)], axis=1)


def _cast_blocks(rows, n_grid):
    blocks = 1
    while blocks * 2 <= n_grid and rows % (blocks * 2 * 16) == 0:
        blocks *= 2
    return blocks


def _in_proj(x, norm_gain, w_in_bf, gain_all, cast_weights):
    B, S, D = x.shape
    n_t = S // TM_IN
    n_j = w_in_bf.shape[1] // TN_IN
    n_steps = B * n_t * n_j
    groups = TN_IN // LANES_V7X
    vb_cols = (G_GB - G_VB) * LANES_V7X

    def cur(f):
        f = jnp.minimum(f, n_steps - 1)
        return f // (n_t * n_j), (f // n_j) % n_t, f % n_j

    def prev(f):
        return cur(jnp.maximum(f - 1, 0))

    def cast_spec(w):
        blocks = _cast_blocks(w.shape[0], n_steps + 1)
        return pl.BlockSpec((w.shape[0] // blocks, w.shape[1]),
                            lambda f: (jnp.minimum(f, blocks - 1), 0))

    cast_specs = [cast_spec(w) for w in cast_weights]
    outs = pl.pallas_call(
        functools.partial(_in_proj_kernel, n_steps=n_steps, n_j=n_j, n_cast=len(cast_weights)),
        grid=(n_steps + 1,),
        in_specs=[
            pl.BlockSpec((1, TM_IN, D), lambda f: (cur(f)[0], cur(f)[1], 0)),
            pl.BlockSpec((1, D), lambda f: (0, 0)),
            pl.BlockSpec((D, TN_IN), lambda f: (0, cur(f)[2])),
            pl.BlockSpec((1, TN_IN), lambda f: (0, prev(f)[2])),
        ] + cast_specs,
        out_specs=[
            pl.BlockSpec((1, groups, TM_IN, LANES_V7X),
                         lambda f: (prev(f)[0], prev(f)[2], prev(f)[1], 0)),
            pl.BlockSpec((1, TM_IN, vb_cols), lambda f: (prev(f)[0], prev(f)[1], 0)),
            pl.BlockSpec((1, TM_IN, D // n_j), lambda f: cur(f)),
        ] + cast_specs,
        out_shape=[
            jax.ShapeDtypeStruct((B, N_GROUPS, S, LANES_V7X), BF16),
            jax.ShapeDtypeStruct((B, S, vb_cols), F32),
            jax.ShapeDtypeStruct((B, S, D), BF16),
        ] + [jax.ShapeDtypeStruct(w.shape, BF16) for w in cast_weights],
        scratch_shapes=[pltpu.VMEM((IN_K_CHUNKS, TM_IN, D // IN_K_CHUNKS), BF16),
                        pltpu.VMEM((TM_IN, TN_IN), F32)],
        compiler_params=_params(("arbitrary",)),
        name="in_proj",
    )(x, norm_gain, w_in_bf, gain_all, *cast_weights)
    return outs[0], outs[1], outs[2], outs[3:]


def _mem_kv_kernel(mem_ref, mg_ref, w_ref, kg_ref, o_ref):
    mh = (_rms_rows(mem_ref[0]) * mg_ref[...]).astype(BF16)
    kv = jnp.dot(mh, w_ref[...], preferred_element_type=F32)
    for g in range(M_HEADS):
        sl = slice(g * HEAD_DIM, (g + 1) * HEAD_DIM)
        o_ref[0, g] = (_rms_rows(kv[:, sl]) * kg_ref[:, sl]).astype(BF16)
    for g in range(M_HEADS, 2 * M_HEADS):
        o_ref[0, g] = kv[:, g * HEAD_DIM:(g + 1) * HEAD_DIM].astype(BF16)


def _mem_kv(mem, mem_norm_gain, w_kv_bf, k_gain):
    B, N, D = mem.shape
    W = w_kv_bf.shape[1]
    return pl.pallas_call(
        _mem_kv_kernel,
        grid=(B,),
        in_specs=[
            pl.BlockSpec((1, N, D), lambda b: (b, 0, 0)),
            pl.BlockSpec((1, D), lambda b: (0, 0)),
            pl.BlockSpec((D, W), lambda b: (0, 0)),
            pl.BlockSpec((1, W // 2), lambda b: (0, 0)),
        ],
        out_specs=pl.BlockSpec((1, 2 * M_HEADS, N, HEAD_DIM), lambda b: (b, 0, 0, 0)),
        out_shape=jax.ShapeDtypeStruct((B, 2 * M_HEADS, N, HEAD_DIM), BF16),
        compiler_params=_params(("arbitrary",)),
        name="mem_kv",
    )(mem, mem_norm_gain, w_kv_bf, k_gain)


_BAND = N_KT * TQ
_RWRAP = _BAND + TQ


def _rel_bias_kernel(r_ref, o_ref):
    row = lax.broadcasted_iota(jnp.int32, (TQ, _RWRAP), 0)
    row_b = lax.broadcasted_iota(jnp.int32, (TQ, _BAND), 0)
    col_b = lax.broadcasted_iota(jnp.int32, (TQ, _BAND), 1)
    lo = jnp.where(row_b < CHUNK, 0, CHUNK)
    valid = (col_b >= lo) & (col_b < lo + _BAND - CHUNK)
    for h in range(A_HEADS):
        t = jnp.broadcast_to(r_ref[h:h + 1, :], (TQ, _RWRAP))
        for k in range(7):
            t = jnp.where(((row >> k) & 1) == 1, pltpu.roll(t, 1 << k, 1), t)
        o_ref[h] = jnp.where(valid, t[:, :_BAND] * LOG2_E, NEG_INF)


def _rel_bias_tiles(rel_bias):
    H = rel_bias.shape[0]
    edge = jnp.broadcast_to(rel_bias[:, 2 * REL_CLIP:], (H, 2 * REL_CLIP))
    mid = jnp.flip(rel_bias[:, 2 * REL_CLIP + 1 - (_BAND - REL_CLIP):], axis=1)
    r_ext = jnp.concatenate([edge[:, :REL_CLIP], mid, edge[:, :_RWRAP - _BAND]], axis=1)
    return pl.pallas_call(
        _rel_bias_kernel,
        out_shape=jax.ShapeDtypeStruct((H, TQ, _BAND), F32),
        compiler_params=pltpu.CompilerParams(vmem_limit_bytes=VMEM_LIMIT_V7X),
        name="rel_bias",
    )(r_ext)


def _attention_pipeline(n_groups, tiles, scores_fn, values_fn, s_scr, p_scr, l_scr):
    rows, width = l_scr.shape[1:]
    s_scr[...] = jnp.zeros(s_scr.shape, F32)
    p_scr[...] = jnp.zeros(p_scr.shape, BF16)
    l_scr[...] = jnp.ones(l_scr.shape, F32)

    def body(j, carry):
        g3 = jnp.clip(j - 2, 0, n_groups - 1)
        for u in range(tiles):
            values_fn(g3, u, p_scr[u], l_scr[u])
        for u in range(tiles):
            s = s_scr[u]
            p = jnp.exp2(s - jnp.max(s, axis=-1, keepdims=True))
            l_scr[u] = jnp.broadcast_to(jnp.sum(p, axis=-1, keepdims=True), (rows, width))
            p_scr[u] = p.astype(BF16)
        g1 = jnp.minimum(j, n_groups - 1)
        for u in range(tiles):
            s_scr[u] = scores_fn(g1, u)
        return carry

    lax.fori_loop(0, n_groups + 2, body, 0)


def _attention_scratch(tiles, rows, cols):
    return [pltpu.VMEM((tiles, rows, cols), F32), pltpu.VMEM((tiles, rows, cols), BF16),
            pltpu.VMEM((tiles, rows, HEAD_DIM), F32)]


_QK_DIMS = (((1,), (1,)), ((), ()))


def _band_attn_kernel(q_ref, k_ref, v_ref, g_ref, bias_ref, o_ref, *scratch):
    groups_per_head = q_ref.shape[2] // (TQ * ATTN_TILES_PER_STEP)
    col = lax.broadcasted_iota(jnp.int32, (TQ, _BAND), 1)

    def coords(group, u):
        head = group // groups_per_head
        qt = (group % groups_per_head) * ATTN_TILES_PER_STEP + u
        return head, qt, pl.multiple_of(qt * TQ, TQ)

    def band(ref, head, qt):
        tiles = []
        for a in range(N_KT):
            k0 = pl.multiple_of(jnp.maximum(qt - (N_KT - 1) + a, 0) * TQ, TQ)
            tiles.append(ref[0, head, pl.ds(k0, TQ), :])
        return jnp.concatenate(tiles, axis=0)

    def scores(group, u):
        head, qt, q0 = coords(group, u)
        s = lax.dot_general(q_ref[0, head, pl.ds(q0, TQ), :], band(k_ref, head, qt), _QK_DIMS,
                            preferred_element_type=F32)
        s = s + bias_ref[head]
        return jnp.where(col >= (N_KT - 1 - qt) * TQ, s, NEG_INF)

    def values(group, u, p, l):
        head, qt, q0 = coords(group, u)
        o = jnp.dot(p, band(v_ref, head, qt), preferred_element_type=F32) / l
        o_ref[0, head, pl.ds(q0, TQ), :] = (
            o * g_ref[0, head, pl.ds(q0, TQ), :].astype(F32)).astype(BF16)

    _attention_pipeline(ATTN_HEADS_PER_STEP * groups_per_head, ATTN_TILES_PER_STEP,
                        scores, values, *scratch)


def _band_attn(p, bias):
    B, _, S, _ = p.shape
    hb = ATTN_HEADS_PER_STEP
    blk = (1, hb, S, HEAD_DIM)

    def seg(g_first):
        return pl.BlockSpec(blk, lambda b, h: (b, g_first // hb + h, 0, 0))

    return pl.pallas_call(
        _band_attn_kernel,
        grid=(B, A_HEADS // hb),
        in_specs=[seg(G_QA), seg(G_KA), seg(G_VA), seg(G_GA),
                  pl.BlockSpec((hb, TQ, _BAND), lambda b, h: (h, 0, 0))],
        out_specs=seg(0),
        out_shape=jax.ShapeDtypeStruct((B, A_HEADS, S, HEAD_DIM), BF16),
        scratch_shapes=_attention_scratch(ATTN_TILES_PER_STEP, TQ, _BAND),
        compiler_params=_params(("parallel", "arbitrary")),
        name="band_attn",
    )(p, p, p, p, bias)


def _mem_attn_kernel(q_ref, g_ref, kv_ref, o_ref, *scratch):
    groups_per_head = q_ref.shape[2] // (TQ_MEM * MEM_TILES_PER_STEP)

    def coords(group, u):
        head = group // groups_per_head
        qt = (group % groups_per_head) * MEM_TILES_PER_STEP + u
        return head, pl.multiple_of(qt * TQ_MEM, TQ_MEM)

    def scores(group, u):
        head, q0 = coords(group, u)
        return lax.dot_general(q_ref[0, head, pl.ds(q0, TQ_MEM), :], kv_ref[0, head], _QK_DIMS,
                               preferred_element_type=F32)

    def values(group, u, p, l):
        head, q0 = coords(group, u)
        o = jnp.dot(p, kv_ref[0, M_HEADS + head], preferred_element_type=F32) / l
        o_ref[0, head, pl.ds(q0, TQ_MEM), :] = (
            o * g_ref[0, head, pl.ds(q0, TQ_MEM), :].astype(F32)).astype(BF16)

    _attention_pipeline(M_HEADS * groups_per_head, MEM_TILES_PER_STEP,
                        scores, values, *scratch)


def _mem_attn(p, memkv):
    B, _, S, _ = p.shape
    N = memkv.shape[2]
    blk = (1, M_HEADS, S, HEAD_DIM)
    return pl.pallas_call(
        _mem_attn_kernel,
        grid=(B,),
        in_specs=[
            pl.BlockSpec(blk, lambda b: (b, G_QM // M_HEADS, 0, 0)),
            pl.BlockSpec(blk, lambda b: (b, G_GM // M_HEADS, 0, 0)),
            pl.BlockSpec((1, 2 * M_HEADS, N, HEAD_DIM), lambda b: (b, 0, 0, 0)),
        ],
        out_specs=pl.BlockSpec(blk, lambda b: (b, 0, 0, 0)),
        out_shape=jax.ShapeDtypeStruct((B, M_HEADS, S, HEAD_DIM), BF16),
        scratch_shapes=_attention_scratch(MEM_TILES_PER_STEP, TQ_MEM, N),
        compiler_params=_params(("arbitrary",)),
        name="mem_attn",
    )(p, p, memkv)


def _pool_mix_kernel(vb_ref, prev_ref, g_ref, pw_ref, ps_ref, o_ref):
    t = pl.program_id(1)
    cur = vb_ref[0]
    prev = jnp.where(t > 0, prev_ref[0], 0.0)
    pos = t * TR_POOL + lax.broadcasted_iota(jnp.int32, (TR_POOL, 1), 0)
    for g, w in enumerate(POOL_WINDOWS):
        sl = slice(g * LANES_V7X, (g + 1) * LANES_V7X)
        acc = jnp.concatenate([prev[:, sl], cur[:, sl]], axis=0)
        d = 1
        while d < w:
            acc = acc + pltpu.roll(acc, d, 0)
            d *= 2
        cnt = jnp.minimum(pos + 1, w).astype(F32)
        pooled = acc[POOL_HALO:, :] / cnt - cur[:, sl]
        mixed = jnp.dot(pooled.astype(BF16), pw_ref[g], preferred_element_type=F32)
        o_ref[0, g] = (mixed * ps_ref[:, sl] * g_ref[0, g].astype(F32)).astype(BF16)


def _pool_mix(vb, p, pool_w_bf, pool_scale):
    B, S, P = vb.shape
    G = len(POOL_WINDOWS)
    halo_blocks = TR_POOL // POOL_HALO
    return pl.pallas_call(
        _pool_mix_kernel,
        grid=(B, S // TR_POOL),
        in_specs=[
            pl.BlockSpec((1, TR_POOL, P), lambda b, t: (b, t, 0)),
            pl.BlockSpec((1, POOL_HALO, P), lambda b, t: (b, jnp.maximum(t * halo_blocks - 1, 0), 0)),
            pl.BlockSpec((1, G, TR_POOL, LANES_V7X), lambda b, t: (b, G_GB // G, t, 0)),
            pl.BlockSpec((G, LANES_V7X, LANES_V7X), lambda b, t: (0, 0, 0)),
            pl.BlockSpec((1, P), lambda b, t: (0, 0)),
        ],
        out_specs=pl.BlockSpec((1, G, TR_POOL, LANES_V7X), lambda b, t: (b, 0, t, 0)),
        out_shape=jax.ShapeDtypeStruct((B, G, S, LANES_V7X), BF16),
        compiler_params=_params(("parallel", "arbitrary")),
        name="pool_mix",
    )(vb, vb, p, pool_w_bf, pool_scale)


def _merge_out_kernel(x_ref, h_ref, oa_ref, ob_ref, om_ref,
                      wg0_ref, wg1_ref, wg2_ref, b0_ref, b1_ref, b2_ref,
                      wa_ref, wb_ref, wm_ref, wo_ref, out_ref):
    n = pl.program_id(2)

    @pl.when(n == 0)
    def _():
        out_ref[0] = x_ref[0]

    h = h_ref[0]

    def branch(o_ref, w_ref, wg_ref, b_ref):
        o = jnp.concatenate([o_ref[0, g] for g in range(o_ref.shape[1])], axis=-1)
        gate = _sigmoid(jnp.dot(h, wg_ref[...], preferred_element_type=F32) + b_ref[...])
        return gate * jnp.dot(o, w_ref[...], preferred_element_type=F32)

    y = (branch(oa_ref, wa_ref, wg0_ref, b0_ref)
         + branch(ob_ref, wb_ref, wg1_ref, b1_ref)
         + branch(om_ref, wm_ref, wg2_ref, b2_ref))
    out_ref[0] += jnp.dot(y.astype(BF16), wo_ref[...], preferred_element_type=F32)


def _merge_out(x, h, oga, ogb, ogm, w_merge_bf, b_merge, wa_bf, wb_bf, wm_bf, wo_bf):
    B, S, D = x.shape
    n_c = D // CN_OUT

    def gate_w(r):
        return pl.BlockSpec((D, CN_OUT), lambda b, t, n: (0, r * n_c + n))

    def gate_b(r):
        return pl.BlockSpec((1, CN_OUT), lambda b, t, n: (0, r * n_c + n))

    def heads(nh):
        return pl.BlockSpec((1, nh, TM_OUT, HEAD_DIM), lambda b, t, n: (b, 0, t, 0))

    def cols(rows):
        return pl.BlockSpec((rows, CN_OUT), lambda b, t, n: (0, n))

    rows = pl.BlockSpec((1, TM_OUT, D), lambda b, t, n: (b, t, 0))
    return pl.pallas_call(
        _merge_out_kernel,
        grid=(B, S // TM_OUT, n_c),
        in_specs=[
            rows, rows,
            heads(oga.shape[1]), heads(ogb.shape[1]), heads(ogm.shape[1]),
            gate_w(0), gate_w(1), gate_w(2), gate_b(0), gate_b(1), gate_b(2),
            cols(wa_bf.shape[0]), cols(wb_bf.shape[0]), cols(wm_bf.shape[0]),
            pl.BlockSpec((CN_OUT, D), lambda b, t, n: (n, 0)),
        ],
        out_specs=rows,
        out_shape=jax.ShapeDtypeStruct((B, S, D), F32),
        compiler_params=_params(("parallel", "arbitrary", "arbitrary")),
        name="merge_out",
    )(x, h, oga, ogb, ogm, w_merge_bf, w_merge_bf, w_merge_bf,
      b_merge, b_merge, b_merge, wa_bf, wb_bf, wm_bf, wo_bf)


def kernel(x, mem, norm_gain, mem_norm_gain, w_in, w_merge, b_merge, a_q_gain, a_k_gain,
           a_rel_bias, pool_w, pool_scale, w_mem_kv, m_q_gain, m_k_gain,
           w_branch_a, w_branch_b, w_branch_m, w_out):
    depth = w_in.shape[0]
    scale = HEAD_DIM ** -0.5 * LOG2_E
    for l in range(depth):
        a_w, p_w, m_w = w_branch_a.shape[1], w_branch_b.shape[1], w_branch_m.shape[1]
        gain_all = jnp.concatenate([
            a_q_gain[l].reshape(1, a_w) * scale, a_k_gain[l].reshape(1, a_w),
            jnp.ones((1, 2 * a_w + 2 * p_w), F32),
            m_q_gain[l].reshape(1, m_w) * scale, jnp.ones((1, m_w), F32)], axis=1)

        p, vb, h, (wg_bf, wo_bf, wa_bf, wb_bf, wm_bf, wkv_bf) = _in_proj(
            x, norm_gain[l:l + 1], w_in[l].astype(BF16), gain_all,
            [w_merge[l], w_out[l], w_branch_a[l], w_branch_b[l], w_branch_m[l], w_mem_kv[l]])
        memkv = _mem_kv(mem, mem_norm_gain[l:l + 1], wkv_bf, m_k_gain[l].reshape(1, m_w))
        bias = _rel_bias_tiles(a_rel_bias[l])
        oga = _band_attn(p, bias)
        ogm = _mem_attn(p, memkv)
        ogb = _pool_mix(vb, p, pool_w[l].astype(BF16), pool_scale[l:l + 1])
        x = _merge_out(x, h, oga, ogb, ogm, wg_bf, b_merge[l:l + 1], wa_bf, wb_bf, wm_bf, wo_bf)
    return x
```

```python
import functools

import jax
import jax.numpy as jnp
from jax import lax
from jax.experimental import pallas as pl
from jax.experimental.pallas import tpu as pltpu

F32 = jnp.float32
BF16 = jnp.bfloat16

CHUNK = 64
N_LEFT_CHUNKS = 8
A_HEADS = 8
M_HEADS = 4
HEAD_DIM = 128
REL_CLIP = 256
POOL_WINDOWS = (2, 4, 8, 16)
EPS = 1e-6
NEG_INF = -1e30
LOG2_E = 1.4426950408889634

LANES_V7X = 128
MXU_COLS_V7X = 256
VMEM_LIMIT_V7X = 60000 * 1024

TM_IN = 1024
TN_IN = 1536
IN_K_CHUNKS = 8
TQ = 128
N_KT = 5
ATTN_TILES_PER_STEP = 4
ATTN_HEADS_PER_STEP = 4
TQ_MEM = 256
MEM_TILES_PER_STEP = 2
TR_POOL = 2048
POOL_HALO = 16
TM_OUT = 512
CN_OUT = 512

G_QA, G_KA, G_VA, G_GA, G_VB, G_GB, G_QM, G_GM = 0, 8, 16, 24, 32, 36, 40, 44
N_GROUPS = 48


def _params(sem):
    return pltpu.CompilerParams(dimension_semantics=sem, vmem_limit_bytes=VMEM_LIMIT_V7X)


def _rms_rows(t):
    return t * lax.rsqrt(jnp.mean(t * t, axis=-1, keepdims=True) + EPS)


def _sigmoid(z):
    return 1.0 / (1.0 + jnp.exp(-z))


def _fold_rows(src):
    return jnp.sum(src.reshape(src.shape[0] // 8, 8, src.shape[1]), axis=0)


def _tied_zero(folded, shape, dtype):
    bits = pltpu.bitcast(folded, jnp.uint32)
    zero = pltpu.bitcast((bits >> 16) >> 16, F32)
    return jnp.tile(zero, (shape[0] // 8, shape[1] // folded.shape[1])).astype(dtype)


def _in_proj_kernel(*refs, n_steps, n_j, n_cast):
    x_ref, ng_ref, w_ref, gain_ref = refs[:4]
    cast_src = refs[4:4 + n_cast]
    p_ref, vb_ref, h_ref = refs[4 + n_cast:7 + n_cast]
    cast_dst = refs[7 + n_cast:7 + 2 * n_cast]
    h_scr, acc_scr = refs[7 + 2 * n_cast:]
    f = pl.program_id(0)
    groups = TN_IN // LANES_V7X
    k_chunk = h_scr.shape[2]
    j_cur = jnp.minimum(f, n_steps - 1) % n_j
    g_prev = (jnp.maximum(f - 1, 0) % n_j) * groups

    @pl.when(f == 0)
    def _():
        acc_scr[...] = jnp.zeros(acc_scr.shape, F32)

    @pl.when((j_cur == 0) & (f < n_steps))
    def _():
        h = (_rms_rows(x_ref[0]) * ng_ref[...]).astype(BF16)
        for c in range(IN_K_CHUNKS):
            h_scr[c] = h[:, c * k_chunk:(c + 1) * k_chunk]

    @pl.when(g_prev == G_VB // groups * groups)
    def _():
        vb0 = G_VB % groups
        for g in range(vb_ref.shape[2] // LANES_V7X):
            vb_ref[0, :, g * LANES_V7X:(g + 1) * LANES_V7X] = acc_scr[vb0 + g]

    pieces = [(n0, c) for n0 in range(0, TN_IN, MXU_COLS_V7X) for c in range(IN_K_CHUNKS)]
    stride = len(pieces) // groups
    accs = {}
    ties = {}
    for i, (n0, c) in enumerate(pieces):
        ks = slice(c * k_chunk, (c + 1) * k_chunk)
        lhs = h_scr[c]
        if i in ties:
            zero = _tied_zero(ties.pop(i), (16, k_chunk), BF16)
            lhs = jnp.concatenate([lhs[:16] + zero, lhs[16:]], axis=0)
        part = jnp.dot(lhs, w_ref[ks, n0:n0 + MXU_COLS_V7X], preferred_element_type=F32)
        accs[n0] = part if c == 0 else accs[n0] + part
        if i % stride == 0:
            g = i // stride
            sl = slice(g * LANES_V7X, (g + 1) * LANES_V7X)
            a = acc_scr[g]
            gi = g_prev + g
            is_norm = (gi < G_VA) | ((gi >= G_QM) & (gi < G_GM))
            is_silu = ((gi >= G_GA) & (gi < G_VB)) | ((gi >= G_GB) & (gi < G_QM)) | (gi >= G_GM)
            normed = _rms_rows(a) * gain_ref[:, sl]
            res = jnp.where(is_norm, normed, jnp.where(is_silu, a * _sigmoid(a), a))
            p_ref[0, g] = res.astype(BF16)
            ties[i + stride - 1] = _fold_rows(res)
    for n0, acc in accs.items():
        for gg in range(MXU_COLS_V7X // LANES_V7X):
            acc_scr[n0 // LANES_V7X + gg] = acc[:, gg * LANES_V7X:(gg + 1) * LANES_V7X]
    for src, dst in zip(cast_src, cast_dst):
        dst[...] = src[...].astype(BF16)
    per_step = IN_K_CHUNKS // n_j
    for i in range(per_step):
        h_ref[0, i] = h_scr[j_cur * per_step + i]


def _cast_blocks(rows, n_grid):
    blocks = 1
    while blocks * 2 <= n_grid and rows % (blocks * 2 * 16) == 0:
        blocks *= 2
    return blocks


def _in_proj(x, norm_gain, w_in_bf, gain_all, cast_weights):
    B, S, D = x.shape
    n_t = S // TM_IN
    n_j = w_in_bf.shape[1] // TN_IN
    n_steps = B * n_t * n_j
    groups = TN_IN // LANES_V7X
    vb_cols = (G_GB - G_VB) * LANES_V7X
    k_chunk = D // IN_K_CHUNKS

    def cur(f):
        f = jnp.minimum(f, n_steps - 1)
        return f // (n_t * n_j), (f // n_j) % n_t, f % n_j

    def prev(f):
        return cur(jnp.maximum(f - 1, 0))

    def cast_spec(w):
        blocks = _cast_blocks(w.shape[0], n_steps + 1)
        return pl.BlockSpec((w.shape[0] // blocks, w.shape[1]),
                            lambda f: (jnp.minimum(f, blocks - 1), 0))

    cast_specs = [cast_spec(w) for w in cast_weights]
    outs = pl.pallas_call(
        functools.partial(_in_proj_kernel, n_steps=n_steps, n_j=n_j, n_cast=len(cast_weights)),
        grid=(n_steps + 1,),
        in_specs=[
            pl.BlockSpec((1, TM_IN, D), lambda f: (cur(f)[0], cur(f)[1], 0)),
            pl.BlockSpec((1, D), lambda f: (0, 0)),
            pl.BlockSpec((D, TN_IN), lambda f: (0, cur(f)[2])),
            pl.BlockSpec((1, TN_IN), lambda f: (0, prev(f)[2])),
        ] + cast_specs,
        out_specs=[
            pl.BlockSpec((1, groups, TM_IN, LANES_V7X),
                         lambda f: (prev(f)[0], prev(f)[2], prev(f)[1], 0)),
            pl.BlockSpec((1, TM_IN, vb_cols), lambda f: (prev(f)[0], prev(f)[1], 0)),
            pl.BlockSpec((1, IN_K_CHUNKS // n_j, TM_IN, k_chunk),
                         lambda f: (cur(f)[0], cur(f)[2], cur(f)[1], 0)),
        ] + cast_specs,
        out_shape=[
            jax.ShapeDtypeStruct((B, N_GROUPS, S, LANES_V7X), BF16),
            jax.ShapeDtypeStruct((B, S, vb_cols), F32),
            jax.ShapeDtypeStruct((B, IN_K_CHUNKS, S, k_chunk), BF16),
        ] + [jax.ShapeDtypeStruct(w.shape, BF16) for w in cast_weights],
        scratch_shapes=[pltpu.VMEM((IN_K_CHUNKS, TM_IN, k_chunk), BF16),
                        pltpu.VMEM((groups, TM_IN, LANES_V7X), F32)],
        compiler_params=_params(("arbitrary",)),
        name="in_proj",
    )(x, norm_gain, w_in_bf, gain_all, *cast_weights)
    return outs[0], outs[1], outs[2], outs[3:]


def _mem_kv_kernel(mem_ref, mg_ref, w_ref, kg_ref, o_ref):
    mh = (_rms_rows(mem_ref[0]) * mg_ref[...]).astype(BF16)
    kv = jnp.dot(mh, w_ref[...], preferred_element_type=F32)
    for g in range(M_HEADS):
        sl = slice(g * HEAD_DIM, (g + 1) * HEAD_DIM)
        o_ref[0, g] = (_rms_rows(kv[:, sl]) * kg_ref[:, sl]).astype(BF16)
    for g in range(M_HEADS, 2 * M_HEADS):
        o_ref[0, g] = kv[:, g * HEAD_DIM:(g + 1) * HEAD_DIM].astype(BF16)


def _mem_kv(mem, mem_norm_gain, w_kv_bf, k_gain):
    B, N, D = mem.shape
    W = w_kv_bf.shape[1]
    return pl.pallas_call(
        _mem_kv_kernel,
        grid=(B,),
        in_specs=[
            pl.BlockSpec((1, N, D), lambda b: (b, 0, 0)),
            pl.BlockSpec((1, D), lambda b: (0, 0)),
            pl.BlockSpec((D, W), lambda b: (0, 0)),
            pl.BlockSpec((1, W // 2), lambda b: (0, 0)),
        ],
        out_specs=pl.BlockSpec((1, 2 * M_HEADS, N, HEAD_DIM), lambda b: (b, 0, 0, 0)),
        out_shape=jax.ShapeDtypeStruct((B, 2 * M_HEADS, N, HEAD_DIM), BF16),
        compiler_params=_params(("arbitrary",)),
        name="mem_kv",
    )(mem, mem_norm_gain, w_kv_bf, k_gain)


_BAND = N_KT * TQ
_RWRAP = _BAND + TQ


def _rel_bias_kernel(r_ref, o_ref):
    row = lax.broadcasted_iota(jnp.int32, (TQ, _RWRAP), 0)
    row_b = lax.broadcasted_iota(jnp.int32, (TQ, _BAND), 0)
    col_b = lax.broadcasted_iota(jnp.int32, (TQ, _BAND), 1)
    lo = jnp.where(row_b < CHUNK, 0, CHUNK)
    valid = (col_b >= lo) & (col_b < lo + _BAND - CHUNK)
    for h in range(A_HEADS):
        t = jnp.broadcast_to(r_ref[h:h + 1, :], (TQ, _RWRAP))
        for k in range(7):
            t = jnp.where(((row >> k) & 1) == 1, pltpu.roll(t, 1 << k, 1), t)
        o_ref[h] = jnp.where(valid, t[:, :_BAND] * LOG2_E, NEG_INF)


def _rel_bias_tiles(rel_bias):
    H = rel_bias.shape[0]
    edge = jnp.broadcast_to(rel_bias[:, 2 * REL_CLIP:], (H, 2 * REL_CLIP))
    mid = jnp.flip(rel_bias[:, 2 * REL_CLIP + 1 - (_BAND - REL_CLIP):], axis=1)
    r_ext = jnp.concatenate([edge[:, :REL_CLIP], mid, edge[:, :_RWRAP - _BAND]], axis=1)
    return pl.pallas_call(
        _rel_bias_kernel,
        out_shape=jax.ShapeDtypeStruct((H, TQ, _BAND), F32),
        compiler_params=pltpu.CompilerParams(vmem_limit_bytes=VMEM_LIMIT_V7X),
        name="rel_bias",
    )(r_ext)


def _attention_pipeline(n_groups, tiles, scores_fn, values_fn, s_scr, p_scr, l_scr):
    rows, width = l_scr.shape[1:]
    s_scr[...] = jnp.zeros(s_scr.shape, F32)
    p_scr[...] = jnp.zeros(p_scr.shape, BF16)
    l_scr[...] = jnp.ones(l_scr.shape, F32)

    def body(j, carry):
        g3 = jnp.clip(j - 2, 0, n_groups - 1)
        for u in range(tiles):
            values_fn(g3, u, p_scr[u], l_scr[u])
        for u in range(tiles):
            s = s_scr[u]
            p = jnp.exp2(s - jnp.max(s, axis=-1, keepdims=True))
            l_scr[u] = jnp.broadcast_to(jnp.sum(p, axis=-1, keepdims=True), (rows, width))
            p_scr[u] = p.astype(BF16)
        g1 = jnp.minimum(j, n_groups - 1)
        for u in range(tiles):
            s_scr[u] = scores_fn(g1, u)
        return carry

    lax.fori_loop(0, n_groups + 2, body, 0)


def _attention_scratch(tiles, rows, cols):
    return [pltpu.VMEM((tiles, rows, cols), F32), pltpu.VMEM((tiles, rows, cols), BF16),
            pltpu.VMEM((tiles, rows, HEAD_DIM), F32)]


_QK_DIMS = (((1,), (1,)), ((), ()))


def _band_attn_kernel(q_ref, k_ref, v_ref, g_ref, bias_ref, o_ref, *scratch):
    groups_per_head = q_ref.shape[2] // (TQ * ATTN_TILES_PER_STEP)
    col = lax.broadcasted_iota(jnp.int32, (TQ, _BAND), 1)

    def coords(group, u):
        head = group // groups_per_head
        qt = (group % groups_per_head) * ATTN_TILES_PER_STEP + u
        return head, qt, pl.multiple_of(qt * TQ, TQ)

    def band(ref, head, qt):
        tiles = []
        for a in range(N_KT):
            k0 = pl.multiple_of(jnp.maximum(qt - (N_KT - 1) + a, 0) * TQ, TQ)
            tiles.append(ref[0, head, pl.ds(k0, TQ), :])
        return jnp.concatenate(tiles, axis=0)

    def scores(group, u):
        head, qt, q0 = coords(group, u)
        s = lax.dot_general(q_ref[0, head, pl.ds(q0, TQ), :], band(k_ref, head, qt), _QK_DIMS,
                            preferred_element_type=F32)
        s = s + bias_ref[head]
        return jnp.where(col >= (N_KT - 1 - qt) * TQ, s, NEG_INF)

    def values(group, u, p, l):
        head, qt, q0 = coords(group, u)
        o = jnp.dot(p, band(v_ref, head, qt), preferred_element_type=F32) / l
        o_ref[0, head, pl.ds(q0, TQ), :] = (
            o * g_ref[0, head, pl.ds(q0, TQ), :].astype(F32)).astype(BF16)

    _attention_pipeline(ATTN_HEADS_PER_STEP * groups_per_head, ATTN_TILES_PER_STEP,
                        scores, values, *scratch)


def _band_attn(p, bias):
    B, _, S, _ = p.shape
    hb = ATTN_HEADS_PER_STEP
    blk = (1, hb, S, HEAD_DIM)

    def seg(g_first):
        return pl.BlockSpec(blk, lambda b, h: (b, g_first // hb + h, 0, 0))

    return pl.pallas_call(
        _band_attn_kernel,
        grid=(B, A_HEADS // hb),
        in_specs=[seg(G_QA), seg(G_KA), seg(G_VA), seg(G_GA),
                  pl.BlockSpec((hb, TQ, _BAND), lambda b, h: (h, 0, 0))],
        out_specs=seg(0),
        out_shape=jax.ShapeDtypeStruct((B, A_HEADS, S, HEAD_DIM), BF16),
        scratch_shapes=_attention_scratch(ATTN_TILES_PER_STEP, TQ, _BAND),
        compiler_params=_params(("parallel", "arbitrary")),
        name="band_attn",
    )(p, p, p, p, bias)


def _mem_attn_kernel(q_ref, g_ref, kv_ref, o_ref, *scratch):
    groups_per_head = q_ref.shape[2] // (TQ_MEM * MEM_TILES_PER_STEP)

    def coords(group, u):
        head = group // groups_per_head
        qt = (group % groups_per_head) * MEM_TILES_PER_STEP + u
        return head, pl.multiple_of(qt * TQ_MEM, TQ_MEM)

    def scores(group, u):
        head, q0 = coords(group, u)
        return lax.dot_general(q_ref[0, head, pl.ds(q0, TQ_MEM), :], kv_ref[0, head], _QK_DIMS,
                               preferred_element_type=F32)

    def values(group, u, p, l):
        head, q0 = coords(group, u)
        o = jnp.dot(p, kv_ref[0, M_HEADS + head], preferred_element_type=F32) / l
        o_ref[0, head, pl.ds(q0, TQ_MEM), :] = (
            o * g_ref[0, head, pl.ds(q0, TQ_MEM), :].astype(F32)).astype(BF16)

    _attention_pipeline(M_HEADS * groups_per_head, MEM_TILES_PER_STEP,
                        scores, values, *scratch)


def _mem_attn(p, memkv):
    B, _, S, _ = p.shape
    N = memkv.shape[2]
    blk = (1, M_HEADS, S, HEAD_DIM)
    return pl.pallas_call(
        _mem_attn_kernel,
        grid=(B,),
        in_specs=[
            pl.BlockSpec(blk, lambda b: (b, G_QM // M_HEADS, 0, 0)),
            pl.BlockSpec(blk, lambda b: (b, G_GM // M_HEADS, 0, 0)),
            pl.BlockSpec((1, 2 * M_HEADS, N, HEAD_DIM), lambda b: (b, 0, 0, 0)),
        ],
        out_specs=pl.BlockSpec(blk, lambda b: (b, 0, 0, 0)),
        out_shape=jax.ShapeDtypeStruct((B, M_HEADS, S, HEAD_DIM), BF16),
        scratch_shapes=_attention_scratch(MEM_TILES_PER_STEP, TQ_MEM, N),
        compiler_params=_params(("arbitrary",)),
        name="mem_attn",
    )(p, p, memkv)


def _pool_mix_kernel(vb_ref, prev_ref, g_ref, pw_ref, ps_ref, o_ref):
    t = pl.program_id(1)
    cur = vb_ref[0]
    prev = jnp.where(t > 0, prev_ref[0], 0.0)
    pos = t * TR_POOL + lax.broadcasted_iota(jnp.int32, (TR_POOL, 1), 0)
    for g, w in enumerate(POOL_WINDOWS):
        sl = slice(g * LANES_V7X, (g + 1) * LANES_V7X)
        acc = jnp.concatenate([prev[:, sl], cur[:, sl]], axis=0)
        d = 1
        while d < w:
            acc = acc + pltpu.roll(acc, d, 0)
            d *= 2
        cnt = jnp.minimum(pos + 1, w).astype(F32)
        pooled = acc[POOL_HALO:, :] / cnt - cur[:, sl]
        mixed = jnp.dot(pooled.astype(BF16), pw_ref[g], preferred_element_type=F32)
        o_ref[0, g] = (mixed * ps_ref[:, sl] * g_ref[0, g].astype(F32)).astype(BF16)


def _pool_mix(vb, p, pool_w_bf, pool_scale):
    B, S, P = vb.shape
    G = len(POOL_WINDOWS)
    halo_blocks = TR_POOL // POOL_HALO
    return pl.pallas_call(
        _pool_mix_kernel,
        grid=(B, S // TR_POOL),
        in_specs=[
            pl.BlockSpec((1, TR_POOL, P), lambda b, t: (b, t, 0)),
            pl.BlockSpec((1, POOL_HALO, P), lambda b, t: (b, jnp.maximum(t * halo_blocks - 1, 0), 0)),
            pl.BlockSpec((1, G, TR_POOL, LANES_V7X), lambda b, t: (b, G_GB // G, t, 0)),
            pl.BlockSpec((G, LANES_V7X, LANES_V7X), lambda b, t: (0, 0, 0)),
            pl.BlockSpec((1, P), lambda b, t: (0, 0)),
        ],
        out_specs=pl.BlockSpec((1, G, TR_POOL, LANES_V7X), lambda b, t: (b, 0, t, 0)),
        out_shape=jax.ShapeDtypeStruct((B, G, S, LANES_V7X), BF16),
        compiler_params=_params(("parallel", "arbitrary")),
        name="pool_mix",
    )(vb, vb, p, pool_w_bf, pool_scale)


def _merge_out_kernel(x_ref, h_ref, oa_ref, ob_ref, om_ref,
                      wg0_ref, wg1_ref, wg2_ref, b0_ref, b1_ref, b2_ref,
                      wa_ref, wb_ref, wm_ref, wo_ref, out_ref):
    n = pl.program_id(2)

    @pl.when(n == 0)
    def _():
        out_ref[0] = x_ref[0]

    k_chunk = h_ref.shape[3]

    def branch(o_ref, w_ref, wg_ref, b_ref):
        o = jnp.concatenate([o_ref[0, g] for g in range(o_ref.shape[1])], axis=-1)
        pre = None
        for c in range(h_ref.shape[1]):
            part = jnp.dot(h_ref[0, c], wg_ref[c * k_chunk:(c + 1) * k_chunk, :],
                           preferred_element_type=F32)
            pre = part if pre is None else pre + part
        return _sigmoid(pre + b_ref[...]) * jnp.dot(o, w_ref[...], preferred_element_type=F32)

    y = (branch(oa_ref, wa_ref, wg0_ref, b0_ref)
         + branch(ob_ref, wb_ref, wg1_ref, b1_ref)
         + branch(om_ref, wm_ref, wg2_ref, b2_ref))
    out_ref[0] += jnp.dot(y.astype(BF16), wo_ref[...], preferred_element_type=F32)


def _merge_out(x, h, oga, ogb, ogm, w_merge_bf, b_merge, wa_bf, wb_bf, wm_bf, wo_bf):
    B, S, D = x.shape
    n_c = D // CN_OUT

    def gate_w(r):
        return pl.BlockSpec((D, CN_OUT), lambda b, t, n: (0, r * n_c + n))

    def gate_b(r):
        return pl.BlockSpec((1, CN_OUT), lambda b, t, n: (0, r * n_c + n))

    def heads(nh):
        return pl.BlockSpec((1, nh, TM_OUT, HEAD_DIM), lambda b, t, n: (b, 0, t, 0))

    def cols(rows):
        return pl.BlockSpec((rows, CN_OUT), lambda b, t, n: (0, n))

    rows = pl.BlockSpec((1, TM_OUT, D), lambda b, t, n: (b, t, 0))
    return pl.pallas_call(
        _merge_out_kernel,
        grid=(B, S // TM_OUT, n_c),
        in_specs=[
            rows, pl.BlockSpec((1, h.shape[1], TM_OUT, h.shape[3]), lambda b, t, n: (b, 0, t, 0)),
            heads(oga.shape[1]), heads(ogb.shape[1]), heads(ogm.shape[1]),
            gate_w(0), gate_w(1), gate_w(2), gate_b(0), gate_b(1), gate_b(2),
            cols(wa_bf.shape[0]), cols(wb_bf.shape[0]), cols(wm_bf.shape[0]),
            pl.BlockSpec((CN_OUT, D), lambda b, t, n: (n, 0)),
        ],
        out_specs=rows,
        out_shape=jax.ShapeDtypeStruct((B, S, D), F32),
        compiler_params=_params(("parallel", "arbitrary", "arbitrary")),
        name="merge_out",
    )(x, h, oga, ogb, ogm, w_merge_bf, w_merge_bf, w_merge_bf,
      b_merge, b_merge, b_merge, wa_bf, wb_bf, wm_bf, wo_bf)


def kernel(x, mem, norm_gain, mem_norm_gain, w_in, w_merge, b_merge, a_q_gain, a_k_gain,
           a_rel_bias, pool_w, pool_scale, w_mem_kv, m_q_gain, m_k_gain,
           w_branch_a, w_branch_b, w_branch_m, w_out):
    depth = w_in.shape[0]
    scale = HEAD_DIM ** -0.5 * LOG2_E
    for l in range(depth):
        a_w, p_w, m_w = w_branch_a.shape[1], w_branch_b.shape[1], w_branch_m.shape[1]
        gain_all = jnp.concatenate([
            a_q_gain[l].reshape(1, a_w) * scale, a_k_gain[l].reshape(1, a_w),
            jnp.ones((1, 2 * a_w + 2 * p_w), F32),
            m_q_gain[l].reshape(1, m_w) * scale, jnp.ones((1, m_w), F32)], axis=1)

        p, vb, h, (wg_bf, wo_bf, wa_bf, wb_bf, wm_bf, wkv_bf) = _in_proj(
            x, norm_gain[l:l + 1], w_in[l].astype(BF16), gain_all,
            [w_merge[l], w_out[l], w_branch_a[l], w_branch_b[l], w_branch_m[l], w_mem_kv[l]])
        memkv = _mem_kv(mem, mem_norm_gain[l:l + 1], wkv_bf, m_k_gain[l].reshape(1, m_w))
        bias = _rel_bias_tiles(a_rel_bias[l])
        oga = _band_attn(p, bias)
        ogm = _mem_attn(p, memkv)
        ogb = _pool_mix(vb, p, pool_w[l].astype(BF16), pool_scale[l:l + 1])
        x = _merge_out(x, h, oga, ogb, ogm, wg_bf, b_merge[l:l + 1], wa_bf, wb_bf, wm_bf, wo_bf)
    return x
```

```python
import functools

import jax
import jax.numpy as jnp
from jax import lax
from jax.experimental import pallas as pl
from jax.experimental.pallas import tpu as pltpu

F32 = jnp.float32
BF16 = jnp.bfloat16

CHUNK = 64
N_LEFT_CHUNKS = 8
A_HEADS = 8
M_HEADS = 4
HEAD_DIM = 128
REL_CLIP = 256
POOL_WINDOWS = (2, 4, 8, 16)
EPS = 1e-6
NEG_INF = -1e30
LOG2_E = 1.4426950408889634

LANES_V7X = 128
MXU_COLS_V7X = 256
VMEM_LIMIT_V7X = 60000 * 1024

TM_IN = 1024
TN_IN = 1536
IN_K_CHUNKS = 8
TQ = 128
N_KT = 5
ATTN_TILES_PER_STEP = 4
ATTN_HEADS_PER_STEP = 4
TQ_MEM = 256
MEM_TILES_PER_STEP = 2
TR_POOL = 2048
POOL_HALO = 16
TM_OUT = 512
CN_OUT = 512

G_QA, G_KA, G_VA, G_GA, G_VB, G_GB, G_QM, G_GM = 0, 8, 16, 24, 32, 36, 40, 44
N_GROUPS = 48


def _params(sem):
    return pltpu.CompilerParams(dimension_semantics=sem, vmem_limit_bytes=VMEM_LIMIT_V7X)


def _rms_rows(t):
    return t * lax.rsqrt(jnp.mean(t * t, axis=-1, keepdims=True) + EPS)


def _sigmoid(z):
    return 1.0 / (1.0 + jnp.exp(-z))


def _fold_rows(src):
    return jnp.sum(src.reshape(src.shape[0] // 8, 8, src.shape[1]), axis=0)


def _tied_zero(folded, shape, dtype):
    bits = pltpu.bitcast(folded, jnp.uint32)
    zero = pltpu.bitcast((bits >> 16) >> 16, F32)
    return jnp.tile(zero, (shape[0] // 8, shape[1] // folded.shape[1])).astype(dtype)


def _in_proj_kernel(*refs, n_steps, n_j, n_cast):
    x_ref, ng_ref, w_ref, gain_ref = refs[:4]
    cast_src = refs[4:4 + n_cast]
    p_ref, vb_ref, h_ref = refs[4 + n_cast:7 + n_cast]
    cast_dst = refs[7 + n_cast:7 + 2 * n_cast]
    h_scr, acc_scr = refs[7 + 2 * n_cast:]
    f = pl.program_id(0)
    groups = TN_IN // LANES_V7X
    k_chunk = h_scr.shape[2]
    j_cur = jnp.minimum(f, n_steps - 1) % n_j
    g_prev = (jnp.maximum(f - 1, 0) % n_j) * groups

    @pl.when(f == 0)
    def _():
        acc_scr[...] = jnp.zeros(acc_scr.shape, F32)

    @pl.when((j_cur == 0) & (f < n_steps))
    def _():
        h = (_rms_rows(x_ref[0]) * ng_ref[...]).astype(BF16)
        for c in range(IN_K_CHUNKS):
            h_scr[c] = h[:, c * k_chunk:(c + 1) * k_chunk]

    @pl.when(g_prev == G_VB // groups * groups)
    def _():
        vb0 = G_VB % groups
        for g in range(vb_ref.shape[2] // LANES_V7X):
            vb_ref[0, :, g * LANES_V7X:(g + 1) * LANES_V7X] = acc_scr[vb0 + g]

    pieces = [(n0, c) for n0 in range(0, TN_IN, MXU_COLS_V7X) for c in range(IN_K_CHUNKS)]
    stride = len(pieces) // groups

    def finish_group(g):
        sl = slice(g * LANES_V7X, (g + 1) * LANES_V7X)
        a = acc_scr[g]
        gi = g_prev + g
        is_norm = (gi < G_VA) | ((gi >= G_QM) & (gi < G_GM))
        is_silu = ((gi >= G_GA) & (gi < G_VB)) | ((gi >= G_GB) & (gi < G_QM)) | (gi >= G_GM)
        normed = _rms_rows(a) * gain_ref[:, sl]
        res = jnp.where(is_norm, normed, jnp.where(is_silu, a * _sigmoid(a), a))
        p_ref[0, g] = res.astype(BF16)
        return res

    def side_jobs():
        for src, dst in zip(cast_src, cast_dst):
            dst[...] = src[...].astype(BF16)
        per_step = IN_K_CHUNKS // n_j
        for i in range(per_step):
            h_ref[0, i] = h_scr[j_cur * per_step + i]

    @pl.when(f < n_steps)
    def _():
        accs = {}
        ties = {}
        for i, (n0, c) in enumerate(pieces):
            ks = slice(c * k_chunk, (c + 1) * k_chunk)
            lhs = h_scr[c]
            if i in ties:
                zero = _tied_zero(ties.pop(i), (16, k_chunk), BF16)
                lhs = jnp.concatenate([lhs[:16] + zero, lhs[16:]], axis=0)
            part = jnp.dot(lhs, w_ref[ks, n0:n0 + MXU_COLS_V7X], preferred_element_type=F32)
            accs[n0] = part if c == 0 else accs[n0] + part
            if i % stride == 0:
                ties[i + stride - 1] = _fold_rows(finish_group(i // stride))
        for n0, acc in accs.items():
            for gg in range(MXU_COLS_V7X // LANES_V7X):
                acc_scr[n0 // LANES_V7X + gg] = acc[:, gg * LANES_V7X:(gg + 1) * LANES_V7X]
        side_jobs()

    @pl.when(f == n_steps)
    def _():
        for g in range(groups):
            finish_group(g)
        side_jobs()


def _cast_blocks(rows, n_grid):
    blocks = 1
    while blocks * 2 <= n_grid and rows % (blocks * 2 * 16) == 0:
        blocks *= 2
    return blocks


def _in_proj(x, norm_gain, w_in_bf, gain_all, cast_weights):
    B, S, D = x.shape
    n_t = S // TM_IN
    n_j = w_in_bf.shape[1] // TN_IN
    n_steps = B * n_t * n_j
    groups = TN_IN // LANES_V7X
    vb_cols = (G_GB - G_VB) * LANES_V7X
    k_chunk = D // IN_K_CHUNKS

    def cur(f):
        f = jnp.minimum(f, n_steps - 1)
        return f // (n_t * n_j), (f // n_j) % n_t, f % n_j

    def prev(f):
        return cur(jnp.maximum(f - 1, 0))

    def cast_spec(w):
        blocks = _cast_blocks(w.shape[0], n_steps + 1)
        return pl.BlockSpec((w.shape[0] // blocks, w.shape[1]),
                            lambda f: (jnp.minimum(f, blocks - 1), 0))

    cast_specs = [cast_spec(w) for w in cast_weights]
    outs = pl.pallas_call(
        functools.partial(_in_proj_kernel, n_steps=n_steps, n_j=n_j, n_cast=len(cast_weights)),
        grid=(n_steps + 1,),
        in_specs=[
            pl.BlockSpec((1, TM_IN, D), lambda f: (cur(f)[0], cur(f)[1], 0)),
            pl.BlockSpec((1, D), lambda f: (0, 0)),
            pl.BlockSpec((D, TN_IN), lambda f: (0, cur(f)[2])),
            pl.BlockSpec((1, TN_IN), lambda f: (0, prev(f)[2])),
        ] + cast_specs,
        out_specs=[
            pl.BlockSpec((1, groups, TM_IN, LANES_V7X),
                         lambda f: (prev(f)[0], prev(f)[2], prev(f)[1], 0)),
            pl.BlockSpec((1, TM_IN, vb_cols), lambda f: (prev(f)[0], prev(f)[1], 0)),
            pl.BlockSpec((1, IN_K_CHUNKS // n_j, TM_IN, k_chunk),
                         lambda f: (cur(f)[0], cur(f)[2], cur(f)[1], 0)),
        ] + cast_specs,
        out_shape=[
            jax.ShapeDtypeStruct((B, N_GROUPS, S, LANES_V7X), BF16),
            jax.ShapeDtypeStruct((B, S, vb_cols), F32),
            jax.ShapeDtypeStruct((B, IN_K_CHUNKS, S, k_chunk), BF16),
        ] + [jax.ShapeDtypeStruct(w.shape, BF16) for w in cast_weights],
        scratch_shapes=[pltpu.VMEM((IN_K_CHUNKS, TM_IN, k_chunk), BF16),
                        pltpu.VMEM((groups, TM_IN, LANES_V7X), F32)],
        compiler_params=_params(("arbitrary",)),
        name="in_proj",
    )(x, norm_gain, w_in_bf, gain_all, *cast_weights)
    return outs[0], outs[1], outs[2], outs[3:]


def _mem_kv_kernel(mem_ref, mg_ref, w_ref, kg_ref, o_ref):
    mh = (_rms_rows(mem_ref[0]) * mg_ref[...]).astype(BF16)
    kv = jnp.dot(mh, w_ref[...], preferred_element_type=F32)
    for g in range(M_HEADS):
        sl = slice(g * HEAD_DIM, (g + 1) * HEAD_DIM)
        o_ref[0, g] = (_rms_rows(kv[:, sl]) * kg_ref[:, sl]).astype(BF16)
    for g in range(M_HEADS, 2 * M_HEADS):
        o_ref[0, g] = kv[:, g * HEAD_DIM:(g + 1) * HEAD_DIM].astype(BF16)


def _mem_kv(mem, mem_norm_gain, w_kv_bf, k_gain):
    B, N, D = mem.shape
    W = w_kv_bf.shape[1]
    return pl.pallas_call(
        _mem_kv_kernel,
        grid=(B,),
        in_specs=[
            pl.BlockSpec((1, N, D), lambda b: (b, 0, 0)),
            pl.BlockSpec((1, D), lambda b: (0, 0)),
            pl.BlockSpec((D, W), lambda b: (0, 0)),
            pl.BlockSpec((1, W // 2), lambda b: (0, 0)),
        ],
        out_specs=pl.BlockSpec((1, 2 * M_HEADS, N, HEAD_DIM), lambda b: (b, 0, 0, 0)),
        out_shape=jax.ShapeDtypeStruct((B, 2 * M_HEADS, N, HEAD_DIM), BF16),
        compiler_params=_params(("arbitrary",)),
        name="mem_kv",
    )(mem, mem_norm_gain, w_kv_bf, k_gain)


_BAND = N_KT * TQ
_RWRAP = _BAND + TQ


def _rel_bias_kernel(r_ref, o_ref):
    row = lax.broadcasted_iota(jnp.int32, (TQ, _RWRAP), 0)
    row_b = lax.broadcasted_iota(jnp.int32, (TQ, _BAND), 0)
    col_b = lax.broadcasted_iota(jnp.int32, (TQ, _BAND), 1)
    lo = jnp.where(row_b < CHUNK, 0, CHUNK)
    valid = (col_b >= lo) & (col_b < lo + _BAND - CHUNK)
    for h in range(A_HEADS):
        t = jnp.broadcast_to(r_ref[h:h + 1, :], (TQ, _RWRAP))
        for k in range(7):
            t = jnp.where(((row >> k) & 1) == 1, pltpu.roll(t, 1 << k, 1), t)
        o_ref[h] = jnp.where(valid, t[:, :_BAND] * LOG2_E, NEG_INF)


def _rel_bias_tiles(rel_bias):
    H = rel_bias.shape[0]
    edge = jnp.broadcast_to(rel_bias[:, 2 * REL_CLIP:], (H, 2 * REL_CLIP))
    mid = jnp.flip(rel_bias[:, 2 * REL_CLIP + 1 - (_BAND - REL_CLIP):], axis=1)
    r_ext = jnp.concatenate([edge[:, :REL_CLIP], mid, edge[:, :_RWRAP - _BAND]], axis=1)
    return pl.pallas_call(
        _rel_bias_kernel,
        out_shape=jax.ShapeDtypeStruct((H, TQ, _BAND), F32),
        compiler_params=pltpu.CompilerParams(vmem_limit_bytes=VMEM_LIMIT_V7X),
        name="rel_bias",
    )(r_ext)


def _attention_pipeline(n_groups, tiles, scores_fn, values_fn, s_scr, p_scr, l_scr):
    rows, width = l_scr.shape[1:]

    def values(group):
        group = jnp.asarray(group, jnp.int32)
        for u in range(tiles):
            values_fn(group, u, p_scr[u], l_scr[u])

    def softmax():
        for u in range(tiles):
            s = s_scr[u]
            p = jnp.exp2(s - jnp.max(s, axis=-1, keepdims=True))
            l_scr[u] = jnp.broadcast_to(jnp.sum(p, axis=-1, keepdims=True), (rows, width))
            p_scr[u] = p.astype(BF16)

    def scores(group):
        group = jnp.asarray(group, jnp.int32)
        for u in range(tiles):
            s_scr[u] = scores_fn(group, u)

    def body(j, carry):
        values(j - 2)
        softmax()
        scores(j)
        return carry

    scores(0)
    softmax()
    scores(1)
    lax.fori_loop(2, n_groups, body, 0)
    values(n_groups - 2)
    softmax()
    values(n_groups - 1)


def _attention_scratch(tiles, rows, cols):
    return [pltpu.VMEM((tiles, rows, cols), F32), pltpu.VMEM((tiles, rows, cols), BF16),
            pltpu.VMEM((tiles, rows, HEAD_DIM), F32)]


_QK_DIMS = (((1,), (1,)), ((), ()))


def _band_attn_kernel(q_ref, k_ref, v_ref, g_ref, bias_ref, o_ref, *scratch):
    groups_per_head = q_ref.shape[2] // (TQ * ATTN_TILES_PER_STEP)
    col = lax.broadcasted_iota(jnp.int32, (TQ, _BAND), 1)

    def coords(group, u):
        head = group // groups_per_head
        qt = (group % groups_per_head) * ATTN_TILES_PER_STEP + u
        return head, qt, pl.multiple_of(qt * TQ, TQ)

    def band(ref, head, qt):
        tiles = []
        for a in range(N_KT):
            k0 = pl.multiple_of(jnp.maximum(qt - (N_KT - 1) + a, 0) * TQ, TQ)
            tiles.append(ref[0, head, pl.ds(k0, TQ), :])
        return jnp.concatenate(tiles, axis=0)

    def scores(group, u):
        head, qt, q0 = coords(group, u)
        s = lax.dot_general(q_ref[0, head, pl.ds(q0, TQ), :], band(k_ref, head, qt), _QK_DIMS,
                            preferred_element_type=F32)
        s = s + bias_ref[head]
        return jnp.where(col >= (N_KT - 1 - qt) * TQ, s, NEG_INF)

    def values(group, u, p, l):
        head, qt, q0 = coords(group, u)
        o = jnp.dot(p, band(v_ref, head, qt), preferred_element_type=F32) / l
        o_ref[0, head, pl.ds(q0, TQ), :] = (
            o * g_ref[0, head, pl.ds(q0, TQ), :].astype(F32)).astype(BF16)

    _attention_pipeline(ATTN_HEADS_PER_STEP * groups_per_head, ATTN_TILES_PER_STEP,
                        scores, values, *scratch)


def _band_attn(p, bias):
    B, _, S, _ = p.shape
    hb = ATTN_HEADS_PER_STEP
    blk = (1, hb, S, HEAD_DIM)

    def seg(g_first):
        return pl.BlockSpec(blk, lambda b, h: (b, g_first // hb + h, 0, 0))

    return pl.pallas_call(
        _band_attn_kernel,
        grid=(B, A_HEADS // hb),
        in_specs=[seg(G_QA), seg(G_KA), seg(G_VA), seg(G_GA),
                  pl.BlockSpec((hb, TQ, _BAND), lambda b, h: (h, 0, 0))],
        out_specs=seg(0),
        out_shape=jax.ShapeDtypeStruct((B, A_HEADS, S, HEAD_DIM), BF16),
        scratch_shapes=_attention_scratch(ATTN_TILES_PER_STEP, TQ, _BAND),
        compiler_params=_params(("parallel", "arbitrary")),
        name="band_attn",
    )(p, p, p, p, bias)


def _mem_attn_kernel(q_ref, g_ref, kv_ref, o_ref, *scratch):
    groups_per_head = q_ref.shape[2] // (TQ_MEM * MEM_TILES_PER_STEP)

    def coords(group, u):
        head = group // groups_per_head
        qt = (group % groups_per_head) * MEM_TILES_PER_STEP + u
        return head, pl.multiple_of(qt * TQ_MEM, TQ_MEM)

    def scores(group, u):
        head, q0 = coords(group, u)
        return lax.dot_general(q_ref[0, head, pl.ds(q0, TQ_MEM), :], kv_ref[0, head], _QK_DIMS,
                               preferred_element_type=F32)

    def values(group, u, p, l):
        head, q0 = coords(group, u)
        o = jnp.dot(p, kv_ref[0, M_HEADS + head], preferred_element_type=F32) / l
        o_ref[0, head, pl.ds(q0, TQ_MEM), :] = (
            o * g_ref[0, head, pl.ds(q0, TQ_MEM), :].astype(F32)).astype(BF16)

    _attention_pipeline(M_HEADS * groups_per_head, MEM_TILES_PER_STEP,
                        scores, values, *scratch)


def _mem_attn(p, memkv):
    B, _, S, _ = p.shape
    N = memkv.shape[2]
    blk = (1, M_HEADS, S, HEAD_DIM)
    return pl.pallas_call(
        _mem_attn_kernel,
        grid=(B,),
        in_specs=[
            pl.BlockSpec(blk, lambda b: (b, G_QM // M_HEADS, 0, 0)),
            pl.BlockSpec(blk, lambda b: (b, G_GM // M_HEADS, 0, 0)),
            pl.BlockSpec((1, 2 * M_HEADS, N, HEAD_DIM), lambda b: (b, 0, 0, 0)),
        ],
        out_specs=pl.BlockSpec(blk, lambda b: (b, 0, 0, 0)),
        out_shape=jax.ShapeDtypeStruct((B, M_HEADS, S, HEAD_DIM), BF16),
        scratch_shapes=_attention_scratch(MEM_TILES_PER_STEP, TQ_MEM, N),
        compiler_params=_params(("arbitrary",)),
        name="mem_attn",
    )(p, p, memkv)


def _pool_mix_kernel(vb_ref, prev_ref, g_ref, pw_ref, ps_ref, o_ref):
    t = pl.program_id(1)
    cur = vb_ref[0]
    prev = jnp.where(t > 0, prev_ref[0], 0.0)
    pos = t * TR_POOL + lax.broadcasted_iota(jnp.int32, (TR_POOL, 1), 0)
    for g, w in enumerate(POOL_WINDOWS):
        sl = slice(g * LANES_V7X, (g + 1) * LANES_V7X)
        acc = jnp.concatenate([prev[:, sl], cur[:, sl]], axis=0)
        d = 1
        while d < w:
            acc = acc + pltpu.roll(acc, d, 0)
            d *= 2
        cnt = jnp.minimum(pos + 1, w).astype(F32)
        pooled = acc[POOL_HALO:, :] / cnt - cur[:, sl]
        mixed = jnp.dot(pooled.astype(BF16), pw_ref[g], preferred_element_type=F32)
        o_ref[0, g] = (mixed * ps_ref[:, sl] * g_ref[0, g].astype(F32)).astype(BF16)


def _pool_mix(vb, p, pool_w_bf, pool_scale):
    B, S, P = vb.shape
    G = len(POOL_WINDOWS)
    halo_blocks = TR_POOL // POOL_HALO
    return pl.pallas_call(
        _pool_mix_kernel,
        grid=(B, S // TR_POOL),
        in_specs=[
            pl.BlockSpec((1, TR_POOL, P), lambda b, t: (b, t, 0)),
            pl.BlockSpec((1, POOL_HALO, P), lambda b, t: (b, jnp.maximum(t * halo_blocks - 1, 0), 0)),
            pl.BlockSpec((1, G, TR_POOL, LANES_V7X), lambda b, t: (b, G_GB // G, t, 0)),
            pl.BlockSpec((G, LANES_V7X, LANES_V7X), lambda b, t: (0, 0, 0)),
            pl.BlockSpec((1, P), lambda b, t: (0, 0)),
        ],
        out_specs=pl.BlockSpec((1, G, TR_POOL, LANES_V7X), lambda b, t: (b, 0, t, 0)),
        out_shape=jax.ShapeDtypeStruct((B, G, S, LANES_V7X), BF16),
        compiler_params=_params(("parallel", "arbitrary")),
        name="pool_mix",
    )(vb, vb, p, pool_w_bf, pool_scale)


def _merge_out_kernel(x_ref, h_ref, oa_ref, ob_ref, om_ref,
                      wg0_ref, wg1_ref, wg2_ref, b0_ref, b1_ref, b2_ref,
                      wa_ref, wb_ref, wm_ref, wo_ref, out_ref):
    n = pl.program_id(2)

    @pl.when(n == 0)
    def _():
        out_ref[0] = x_ref[0]

    k_chunk = h_ref.shape[3]

    def branch(o_ref, w_ref, wg_ref, b_ref):
        o = jnp.concatenate([o_ref[0, g] for g in range(o_ref.shape[1])], axis=-1)
        pre = None
        for c in range(h_ref.shape[1]):
            part = jnp.dot(h_ref[0, c], wg_ref[c * k_chunk:(c + 1) * k_chunk, :],
                           preferred_element_type=F32)
            pre = part if pre is None else pre + part
        return _sigmoid(pre + b_ref[...]) * jnp.dot(o, w_ref[...], preferred_element_type=F32)

    y = (branch(oa_ref, wa_ref, wg0_ref, b0_ref)
         + branch(ob_ref, wb_ref, wg1_ref, b1_ref)
         + branch(om_ref, wm_ref, wg2_ref, b2_ref))
    out_ref[0] += jnp.dot(y.astype(BF16), wo_ref[...], preferred_element_type=F32)


def _merge_out(x, h, oga, ogb, ogm, w_merge_bf, b_merge, wa_bf, wb_bf, wm_bf, wo_bf):
    B, S, D = x.shape
    n_c = D // CN_OUT

    def gate_w(r):
        return pl.BlockSpec((D, CN_OUT), lambda b, t, n: (0, r * n_c + n))

    def gate_b(r):
        return pl.BlockSpec((1, CN_OUT), lambda b, t, n: (0, r * n_c + n))

    def heads(nh):
        return pl.BlockSpec((1, nh, TM_OUT, HEAD_DIM), lambda b, t, n: (b, 0, t, 0))

    def cols(rows):
        return pl.BlockSpec((rows, CN_OUT), lambda b, t, n: (0, n))

    rows = pl.BlockSpec((1, TM_OUT, D), lambda b, t, n: (b, t, 0))
    return pl.pallas_call(
        _merge_out_kernel,
        grid=(B, S // TM_OUT, n_c),
        in_specs=[
            rows, pl.BlockSpec((1, h.shape[1], TM_OUT, h.shape[3]), lambda b, t, n: (b, 0, t, 0)),
            heads(oga.shape[1]), heads(ogb.shape[1]), heads(ogm.shape[1]),
            gate_w(0), gate_w(1), gate_w(2), gate_b(0), gate_b(1), gate_b(2),
            cols(wa_bf.shape[0]), cols(wb_bf.shape[0]), cols(wm_bf.shape[0]),
            pl.BlockSpec((CN_OUT, D), lambda b, t, n: (n, 0)),
        ],
        out_specs=rows,
        out_shape=jax.ShapeDtypeStruct((B, S, D), F32),
        compiler_params=_params(("parallel", "arbitrary", "arbitrary")),
        name="merge_out",
    )(x, h, oga, ogb, ogm, w_merge_bf, w_merge_bf, w_merge_bf,
      b_merge, b_merge, b_merge, wa_bf, wb_bf, wm_bf, wo_bf)


def kernel(x, mem, norm_gain, mem_norm_gain, w_in, w_merge, b_merge, a_q_gain, a_k_gain,
           a_rel_bias, pool_w, pool_scale, w_mem_kv, m_q_gain, m_k_gain,
           w_branch_a, w_branch_b, w_branch_m, w_out):
    depth = w_in.shape[0]
    scale = HEAD_DIM ** -0.5 * LOG2_E
    for l in range(depth):
        a_w, p_w, m_w = w_branch_a.shape[1], w_branch_b.shape[1], w_branch_m.shape[1]
        gain_all = jnp.concatenate([
            a_q_gain[l].reshape(1, a_w) * scale, a_k_gain[l].reshape(1, a_w),
            jnp.ones((1, 2 * a_w + 2 * p_w), F32),
            m_q_gain[l].reshape(1, m_w) * scale, jnp.ones((1, m_w), F32)], axis=1)

        p, vb, h, (wg_bf, wo_bf, wa_bf, wb_bf, wm_bf, wkv_bf) = _in_proj(
            x, norm_gain[l:l + 1], w_in[l].astype(BF16), gain_all,
            [w_merge[l], w_out[l], w_branch_a[l], w_branch_b[l], w_branch_m[l], w_mem_kv[l]])
        memkv = _mem_kv(mem, mem_norm_gain[l:l + 1], wkv_bf, m_k_gain[l].reshape(1, m_w))
        bias = _rel_bias_tiles(a_rel_bias[l])
        oga = _band_attn(p, bias)
        ogm = _mem_attn(p, memkv)
        ogb = _pool_mix(vb, p, pool_w[l].astype(BF16), pool_scale[l:l + 1])
        x = _merge_out(x, h, oga, ogb, ogm, wg_bf, b_merge[l:l + 1], wa_bf, wb_bf, wm_bf, wo_bf)
    return x
```

```python
import functools

import jax
import jax.numpy as jnp
from jax import lax
from jax.experimental import pallas as pl
from jax.experimental.pallas import tpu as pltpu

F32 = jnp.float32
BF16 = jnp.bfloat16

CHUNK = 64
N_LEFT_CHUNKS = 8
A_HEADS = 8
M_HEADS = 4
HEAD_DIM = 128
REL_CLIP = 256
POOL_WINDOWS = (2, 4, 8, 16)
EPS = 1e-6
NEG_INF = -1e30
LOG2_E = 1.4426950408889634

LANES_V7X = 128
MXU_COLS_V7X = 256
VMEM_LIMIT_V7X = 60000 * 1024

TM_IN = 1024
TN_IN = 1536
IN_K_CHUNKS = 8
TQ = 128
N_KT = 5
ATTN_TILES_PER_STEP = 4
ATTN_HEADS_PER_STEP = 4
TQ_MEM = 256
MEM_TILES_PER_STEP = 2
TR_POOL = 2048
POOL_HALO = 16
TM_OUT = 512
CN_OUT = 512

G_QA, G_KA, G_VA, G_GA, G_VB, G_GB, G_QM, G_GM = 0, 8, 16, 24, 32, 36, 40, 44
N_GROUPS = 48


def _params(sem):
    return pltpu.CompilerParams(dimension_semantics=sem, vmem_limit_bytes=VMEM_LIMIT_V7X)


def _rms_rows(t):
    return t * lax.rsqrt(jnp.mean(t * t, axis=-1, keepdims=True) + EPS)


def _sigmoid(z):
    return 1.0 / (1.0 + jnp.exp(-z))


def _fold_rows(src):
    return jnp.sum(src.reshape(src.shape[0] // 8, 8, src.shape[1]), axis=0)


def _tied_zero(folded, shape, dtype):
    bits = pltpu.bitcast(folded, jnp.uint32)
    zero = pltpu.bitcast((bits >> 16) >> 16, F32)
    return jnp.tile(zero, (shape[0] // 8, shape[1] // folded.shape[1])).astype(dtype)


def _in_proj_kernel(*refs, n_steps, n_j, n_cast):
    x_ref, ng_ref, w_ref, gain_ref = refs[:4]
    cast_src = refs[4:4 + n_cast]
    p_ref, vb_ref, h_ref = refs[4 + n_cast:7 + n_cast]
    cast_dst = refs[7 + n_cast:7 + 2 * n_cast]
    h_scr, acc_scr = refs[7 + 2 * n_cast:]
    f = pl.program_id(0)
    groups = TN_IN // LANES_V7X
    k_chunk = h_scr.shape[2]
    j_cur = jnp.minimum(f, n_steps - 1) % n_j
    g_prev = (jnp.maximum(f - 1, 0) % n_j) * groups

    @pl.when(f == 0)
    def _():
        acc_scr[...] = jnp.zeros(acc_scr.shape, F32)

    @pl.when((j_cur == 0) & (f < n_steps))
    def _():
        h = (_rms_rows(x_ref[0]) * ng_ref[...]).astype(BF16)
        for c in range(IN_K_CHUNKS):
            h_scr[c] = h[:, c * k_chunk:(c + 1) * k_chunk]

    @pl.when(g_prev == G_VB // groups * groups)
    def _():
        vb0 = G_VB % groups
        for g in range(vb_ref.shape[2] // LANES_V7X):
            vb_ref[0, :, g * LANES_V7X:(g + 1) * LANES_V7X] = acc_scr[vb0 + g]

    pieces = [(n0, c) for n0 in range(0, TN_IN, MXU_COLS_V7X) for c in range(IN_K_CHUNKS)]
    stride = len(pieces) // groups

    def finish_group(g):
        sl = slice(g * LANES_V7X, (g + 1) * LANES_V7X)
        a = acc_scr[g]
        gi = g_prev + g
        is_norm = (gi < G_VA) | ((gi >= G_QM) & (gi < G_GM))
        is_silu = ((gi >= G_GA) & (gi < G_VB)) | ((gi >= G_GB) & (gi < G_QM)) | (gi >= G_GM)
        normed = _rms_rows(a) * gain_ref[:, sl]
        res = jnp.where(is_norm, normed, jnp.where(is_silu, a * _sigmoid(a), a))
        p_ref[0, g] = res.astype(BF16)
        return res

    def side_jobs():
        for src, dst in zip(cast_src, cast_dst):
            dst[...] = src[...].astype(BF16)
        per_step = IN_K_CHUNKS // n_j
        for i in range(per_step):
            h_ref[0, i] = h_scr[j_cur * per_step + i]

    @pl.when(f < n_steps)
    def _():
        accs = {}
        ties = {}
        for i, (n0, c) in enumerate(pieces):
            ks = slice(c * k_chunk, (c + 1) * k_chunk)
            lhs = h_scr[c]
            if i in ties:
                zero = _tied_zero(ties.pop(i), (16, k_chunk), BF16)
                lhs = jnp.concatenate([lhs[:16] + zero, lhs[16:]], axis=0)
            part = jnp.dot(lhs, w_ref[ks, n0:n0 + MXU_COLS_V7X], preferred_element_type=F32)
            accs[n0] = part if c == 0 else accs[n0] + part
            if i % stride == 0:
                ties[i + stride - 1] = _fold_rows(finish_group(i // stride))
        for n0, acc in accs.items():
            for gg in range(MXU_COLS_V7X // LANES_V7X):
                acc_scr[n0 // LANES_V7X + gg] = acc[:, gg * LANES_V7X:(gg + 1) * LANES_V7X]
        side_jobs()

    @pl.when(f == n_steps)
    def _():
        for g in range(groups):
            finish_group(g)
        side_jobs()


def _cast_blocks(rows, n_grid):
    blocks = 1
    while blocks * 2 <= n_grid and rows % (blocks * 2 * 16) == 0:
        blocks *= 2
    return blocks


def _in_proj(x, norm_gain, w_in_bf, gain_all, cast_weights):
    B, S, D = x.shape
    n_t = S // TM_IN
    n_j = w_in_bf.shape[1] // TN_IN
    n_steps = B * n_t * n_j
    groups = TN_IN // LANES_V7X
    vb_cols = (G_GB - G_VB) * LANES_V7X
    k_chunk = D // IN_K_CHUNKS

    def cur(f):
        f = jnp.minimum(f, n_steps - 1)
        return f // (n_t * n_j), (f // n_j) % n_t, f % n_j

    def prev(f):
        return cur(jnp.maximum(f - 1, 0))

    def cast_spec(w):
        blocks = _cast_blocks(w.shape[0], n_steps + 1)
        return pl.BlockSpec((w.shape[0] // blocks, w.shape[1]),
                            lambda f: (jnp.minimum(f, blocks - 1), 0))

    cast_specs = [cast_spec(w) for w in cast_weights]
    outs = pl.pallas_call(
        functools.partial(_in_proj_kernel, n_steps=n_steps, n_j=n_j, n_cast=len(cast_weights)),
        grid=(n_steps + 1,),
        in_specs=[
            pl.BlockSpec((1, TM_IN, D), lambda f: (cur(f)[0], cur(f)[1], 0)),
            pl.BlockSpec((1, D), lambda f: (0, 0)),
            pl.BlockSpec((D, TN_IN), lambda f: (0, cur(f)[2])),
            pl.BlockSpec((1, TN_IN), lambda f: (0, prev(f)[2])),
        ] + cast_specs,
        out_specs=[
            pl.BlockSpec((1, groups, TM_IN, LANES_V7X),
                         lambda f: (prev(f)[0], prev(f)[2], prev(f)[1], 0)),
            pl.BlockSpec((1, TM_IN, vb_cols), lambda f: (prev(f)[0], prev(f)[1], 0)),
            pl.BlockSpec((1, IN_K_CHUNKS // n_j, TM_IN, k_chunk),
                         lambda f: (cur(f)[0], cur(f)[2], cur(f)[1], 0)),
        ] + cast_specs,
        out_shape=[
            jax.ShapeDtypeStruct((B, N_GROUPS, S, LANES_V7X), BF16),
            jax.ShapeDtypeStruct((B, S, vb_cols), F32),
            jax.ShapeDtypeStruct((B, IN_K_CHUNKS, S, k_chunk), BF16),
        ] + [jax.ShapeDtypeStruct(w.shape, BF16) for w in cast_weights],
        scratch_shapes=[pltpu.VMEM((IN_K_CHUNKS, TM_IN, k_chunk), BF16),
                        pltpu.VMEM((groups, TM_IN, LANES_V7X), F32)],
        compiler_params=_params(("arbitrary",)),
        name="in_proj",
    )(x, norm_gain, w_in_bf, gain_all, *cast_weights)
    return outs[0], outs[1], outs[2], outs[3:]


def _mem_kv_kernel(mem_ref, mg_ref, w_ref, kg_ref, o_ref):
    mh = (_rms_rows(mem_ref[0]) * mg_ref[...]).astype(BF16)
    kv = jnp.dot(mh, w_ref[...], preferred_element_type=F32)
    for g in range(M_HEADS):
        sl = slice(g * HEAD_DIM, (g + 1) * HEAD_DIM)
        o_ref[0, g] = (_rms_rows(kv[:, sl]) * kg_ref[:, sl]).astype(BF16)
    for g in range(M_HEADS, 2 * M_HEADS):
        o_ref[0, g] = kv[:, g * HEAD_DIM:(g + 1) * HEAD_DIM].astype(BF16)


def _mem_kv(mem, mem_norm_gain, w_kv_bf, k_gain):
    B, N, D = mem.shape
    W = w_kv_bf.shape[1]
    return pl.pallas_call(
        _mem_kv_kernel,
        grid=(B,),
        in_specs=[
            pl.BlockSpec((1, N, D), lambda b: (b, 0, 0)),
            pl.BlockSpec((1, D), lambda b: (0, 0)),
            pl.BlockSpec((D, W), lambda b: (0, 0)),
            pl.BlockSpec((1, W // 2), lambda b: (0, 0)),
        ],
        out_specs=pl.BlockSpec((1, 2 * M_HEADS, N, HEAD_DIM), lambda b: (b, 0, 0, 0)),
        out_shape=jax.ShapeDtypeStruct((B, 2 * M_HEADS, N, HEAD_DIM), BF16),
        compiler_params=_params(("arbitrary",)),
        name="mem_kv",
    )(mem, mem_norm_gain, w_kv_bf, k_gain)


_BAND = N_KT * TQ
_RWRAP = _BAND + TQ


def _rel_bias_kernel(r_ref, o_ref):
    row = lax.broadcasted_iota(jnp.int32, (TQ, _RWRAP), 0)
    row_b = lax.broadcasted_iota(jnp.int32, (TQ, _BAND), 0)
    col_b = lax.broadcasted_iota(jnp.int32, (TQ, _BAND), 1)
    lo = jnp.where(row_b < CHUNK, 0, CHUNK)
    valid = (col_b >= lo) & (col_b < lo + _BAND - CHUNK)
    for h in range(A_HEADS):
        t = jnp.broadcast_to(r_ref[h:h + 1, :], (TQ, _RWRAP))
        for k in range(7):
            t = jnp.where(((row >> k) & 1) == 1, pltpu.roll(t, 1 << k, 1), t)
        o_ref[h] = jnp.where(valid, t[:, :_BAND] * LOG2_E, NEG_INF)


def _rel_bias_tiles(rel_bias):
    H = rel_bias.shape[0]
    edge = jnp.broadcast_to(rel_bias[:, 2 * REL_CLIP:], (H, 2 * REL_CLIP))
    mid = jnp.flip(rel_bias[:, 2 * REL_CLIP + 1 - (_BAND - REL_CLIP):], axis=1)
    r_ext = jnp.concatenate([edge[:, :REL_CLIP], mid, edge[:, :_RWRAP - _BAND]], axis=1)
    return pl.pallas_call(
        _rel_bias_kernel,
        out_shape=jax.ShapeDtypeStruct((H, TQ, _BAND), F32),
        compiler_params=pltpu.CompilerParams(vmem_limit_bytes=VMEM_LIMIT_V7X),
        name="rel_bias",
    )(r_ext)


def _attention_pipeline(n_groups, tiles, scores_fn, values_fn, s_scr, p_scr, l_scr):
    rows, width = l_scr.shape[1:]

    def values(group):
        group = jnp.asarray(group, jnp.int32)
        for u in range(tiles):
            values_fn(group, u, p_scr[u], l_scr[u])

    def softmax():
        for u in range(tiles):
            s = s_scr[u]
            p = jnp.exp2(s - jnp.max(s, axis=-1, keepdims=True))
            l_scr[u] = jnp.broadcast_to(jnp.sum(p, axis=-1, keepdims=True), (rows, width))
            p_scr[u] = p.astype(BF16)

    def scores(group):
        group = jnp.asarray(group, jnp.int32)
        for u in range(tiles):
            s_scr[u] = scores_fn(group, u)

    def body(j, carry):
        values(j - 2)
        softmax()
        scores(j)
        return carry

    scores(0)
    softmax()
    scores(1)
    lax.fori_loop(2, n_groups, body, 0)
    values(n_groups - 2)
    softmax()
    values(n_groups - 1)


def _attention_scratch(tiles, rows, cols):
    return [pltpu.VMEM((tiles, rows, cols), F32), pltpu.VMEM((tiles, rows, cols), BF16),
            pltpu.VMEM((tiles, rows, HEAD_DIM), F32)]


_QK_DIMS = (((1,), (1,)), ((), ()))


def _band_attn_kernel(q_ref, k_ref, v_ref, g_ref, bias_ref, o_ref, *scratch):
    groups_per_head = q_ref.shape[2] // (TQ * ATTN_TILES_PER_STEP)
    col = lax.broadcasted_iota(jnp.int32, (TQ, _BAND), 1)

    def coords(group, u):
        head = group // groups_per_head
        qt = (group % groups_per_head) * ATTN_TILES_PER_STEP + u
        return head, qt, pl.multiple_of(qt * TQ, TQ)

    def band(ref, head, qt):
        tiles = []
        for a in range(N_KT):
            k0 = pl.multiple_of(jnp.maximum(qt - (N_KT - 1) + a, 0) * TQ, TQ)
            tiles.append(ref[0, head, pl.ds(k0, TQ), :])
        return jnp.concatenate(tiles, axis=0)

    def scores(group, u):
        head, qt, q0 = coords(group, u)
        s = lax.dot_general(q_ref[0, head, pl.ds(q0, TQ), :], band(k_ref, head, qt), _QK_DIMS,
                            preferred_element_type=F32)
        s = s + bias_ref[head]
        return jnp.where(col >= (N_KT - 1 - qt) * TQ, s, NEG_INF)

    def values(group, u, p, l):
        head, qt, q0 = coords(group, u)
        o = jnp.dot(p, band(v_ref, head, qt), preferred_element_type=F32) / l
        o_ref[0, head, pl.ds(q0, TQ), :] = (
            o * g_ref[0, head, pl.ds(q0, TQ), :].astype(F32)).astype(BF16)

    _attention_pipeline(ATTN_HEADS_PER_STEP * groups_per_head, ATTN_TILES_PER_STEP,
                        scores, values, *scratch)


def _band_attn(p, bias):
    B, _, S, _ = p.shape
    hb = ATTN_HEADS_PER_STEP
    blk = (1, hb, S, HEAD_DIM)

    def seg(g_first):
        return pl.BlockSpec(blk, lambda b, h: (b, g_first // hb + h, 0, 0))

    return pl.pallas_call(
        _band_attn_kernel,
        grid=(B, A_HEADS // hb),
        in_specs=[seg(G_QA), seg(G_KA), seg(G_VA), seg(G_GA),
                  pl.BlockSpec((hb, TQ, _BAND), lambda b, h: (h, 0, 0))],
        out_specs=seg(0),
        out_shape=jax.ShapeDtypeStruct((B, A_HEADS, S, HEAD_DIM), BF16),
        scratch_shapes=_attention_scratch(ATTN_TILES_PER_STEP, TQ, _BAND),
        compiler_params=_params(("parallel", "arbitrary")),
        name="band_attn",
    )(p, p, p, p, bias)


def _mem_attn_kernel(q_ref, g_ref, kv_ref, o_ref, *scratch):
    groups_per_head = q_ref.shape[2] // (TQ_MEM * MEM_TILES_PER_STEP)

    def coords(group, u):
        head = group // groups_per_head
        qt = (group % groups_per_head) * MEM_TILES_PER_STEP + u
        return head, pl.multiple_of(qt * TQ_MEM, TQ_MEM)

    def scores(group, u):
        head, q0 = coords(group, u)
        return lax.dot_general(q_ref[0, head, pl.ds(q0, TQ_MEM), :], kv_ref[0, head], _QK_DIMS,
                               preferred_element_type=F32)

    def values(group, u, p, l):
        head, q0 = coords(group, u)
        o = jnp.dot(p, kv_ref[0, M_HEADS + head], preferred_element_type=F32) / l
        o_ref[0, head, pl.ds(q0, TQ_MEM), :] = (
            o * g_ref[0, head, pl.ds(q0, TQ_MEM), :].astype(F32)).astype(BF16)

    _attention_pipeline(M_HEADS * groups_per_head, MEM_TILES_PER_STEP,
                        scores, values, *scratch)


def _mem_attn(p, memkv):
    B, _, S, _ = p.shape
    N = memkv.shape[2]
    blk = (1, M_HEADS, S, HEAD_DIM)
    return pl.pallas_call(
        _mem_attn_kernel,
        grid=(B,),
        in_specs=[
            pl.BlockSpec(blk, lambda b: (b, G_QM // M_HEADS, 0, 0)),
            pl.BlockSpec(blk, lambda b: (b, G_GM // M_HEADS, 0, 0)),
            pl.BlockSpec((1, 2 * M_HEADS, N, HEAD_DIM), lambda b: (b, 0, 0, 0)),
        ],
        out_specs=pl.BlockSpec(blk, lambda b: (b, 0, 0, 0)),
        out_shape=jax.ShapeDtypeStruct((B, M_HEADS, S, HEAD_DIM), BF16),
        scratch_shapes=_attention_scratch(MEM_TILES_PER_STEP, TQ_MEM, N),
        compiler_params=_params(("arbitrary",)),
        name="mem_attn",
    )(p, p, memkv)


def _pool_mix_kernel(vb_ref, prev_ref, g_ref, pw_ref, ps_ref, o_ref):
    t = pl.program_id(1)
    cur = vb_ref[0]
    prev = jnp.where(t > 0, prev_ref[0], 0.0)
    head_pos = lax.broadcasted_iota(jnp.int32, (POOL_HALO, 1), 0) + 1
    for g, w in enumerate(POOL_WINDOWS):
        assert w & (w - 1) == 0 and w <= POOL_HALO
        sl = slice(g * LANES_V7X, (g + 1) * LANES_V7X)
        acc = jnp.concatenate([prev[:, sl], cur[:, sl]], axis=0)
        d = 1
        while d < w:
            acc = acc + pltpu.roll(acc, d, 0)
            d *= 2
        sums = acc[POOL_HALO:, :]
        head_cnt = jnp.where(t == 0, jnp.minimum(head_pos, w), w).astype(F32)
        mean = jnp.concatenate([sums[:POOL_HALO] / head_cnt, sums[POOL_HALO:] * (1.0 / w)], axis=0)
        pooled = mean - cur[:, sl]
        mixed = jnp.dot(pooled.astype(BF16), pw_ref[g], preferred_element_type=F32)
        o_ref[0, g] = (mixed * ps_ref[:, sl] * g_ref[0, g].astype(F32)).astype(BF16)


def _pool_mix(vb, p, pool_w_bf, pool_scale):
    B, S, P = vb.shape
    G = len(POOL_WINDOWS)
    halo_blocks = TR_POOL // POOL_HALO
    return pl.pallas_call(
        _pool_mix_kernel,
        grid=(B, S // TR_POOL),
        in_specs=[
            pl.BlockSpec((1, TR_POOL, P), lambda b, t: (b, t, 0)),
            pl.BlockSpec((1, POOL_HALO, P), lambda b, t: (b, jnp.maximum(t * halo_blocks - 1, 0), 0)),
            pl.BlockSpec((1, G, TR_POOL, LANES_V7X), lambda b, t: (b, G_GB // G, t, 0)),
            pl.BlockSpec((G, LANES_V7X, LANES_V7X), lambda b, t: (0, 0, 0)),
            pl.BlockSpec((1, P), lambda b, t: (0, 0)),
        ],
        out_specs=pl.BlockSpec((1, G, TR_POOL, LANES_V7X), lambda b, t: (b, 0, t, 0)),
        out_shape=jax.ShapeDtypeStruct((B, G, S, LANES_V7X), BF16),
        compiler_params=_params(("parallel", "arbitrary")),
        name="pool_mix",
    )(vb, vb, p, pool_w_bf, pool_scale)


def _merge_out_kernel(x_ref, h_ref, oa_ref, ob_ref, om_ref,
                      wg0_ref, wg1_ref, wg2_ref, b0_ref, b1_ref, b2_ref,
                      wa_ref, wb_ref, wm_ref, wo_ref, out_ref):
    n = pl.program_id(2)

    @pl.when(n == 0)
    def _():
        out_ref[0] = x_ref[0]

    k_chunk = h_ref.shape[3]

    def branch(o_ref, w_ref, wg_ref, b_ref):
        o = jnp.concatenate([o_ref[0, g] for g in range(o_ref.shape[1])], axis=-1)
        pre = None
        for c in range(h_ref.shape[1]):
            part = jnp.dot(h_ref[0, c], wg_ref[c * k_chunk:(c + 1) * k_chunk, :],
                           preferred_element_type=F32)
            pre = part if pre is None else pre + part
        return _sigmoid(pre + b_ref[...]) * jnp.dot(o, w_ref[...], preferred_element_type=F32)

    y = (branch(oa_ref, wa_ref, wg0_ref, b0_ref)
         + branch(ob_ref, wb_ref, wg1_ref, b1_ref)
         + branch(om_ref, wm_ref, wg2_ref, b2_ref))
    out_ref[0] += jnp.dot(y.astype(BF16), wo_ref[...], preferred_element_type=F32)


def _merge_out(x, h, oga, ogb, ogm, w_merge_bf, b_merge, wa_bf, wb_bf, wm_bf, wo_bf):
    B, S, D = x.shape
    n_c = D // CN_OUT

    def gate_w(r):
        return pl.BlockSpec((D, CN_OUT), lambda b, t, n: (0, r * n_c + n))

    def gate_b(r):
        return pl.BlockSpec((1, CN_OUT), lambda b, t, n: (0, r * n_c + n))

    def heads(nh):
        return pl.BlockSpec((1, nh, TM_OUT, HEAD_DIM), lambda b, t, n: (b, 0, t, 0))

    def cols(rows):
        return pl.BlockSpec((rows, CN_OUT), lambda b, t, n: (0, n))

    rows = pl.BlockSpec((1, TM_OUT, D), lambda b, t, n: (b, t, 0))
    return pl.pallas_call(
        _merge_out_kernel,
        grid=(B, S // TM_OUT, n_c),
        in_specs=[
            rows, pl.BlockSpec((1, h.shape[1], TM_OUT, h.shape[3]), lambda b, t, n: (b, 0, t, 0)),
            heads(oga.shape[1]), heads(ogb.shape[1]), heads(ogm.shape[1]),
            gate_w(0), gate_w(1), gate_w(2), gate_b(0), gate_b(1), gate_b(2),
            cols(wa_bf.shape[0]), cols(wb_bf.shape[0]), cols(wm_bf.shape[0]),
            pl.BlockSpec((CN_OUT, D), lambda b, t, n: (n, 0)),
        ],
        out_specs=rows,
        out_shape=jax.ShapeDtypeStruct((B, S, D), F32),
        compiler_params=_params(("parallel", "arbitrary", "arbitrary")),
        name="merge_out",
    )(x, h, oga, ogb, ogm, w_merge_bf, w_merge_bf, w_merge_bf,
      b_merge, b_merge, b_merge, wa_bf, wb_bf, wm_bf, wo_bf)


def kernel(x, mem, norm_gain, mem_norm_gain, w_in, w_merge, b_merge, a_q_gain, a_k_gain,
           a_rel_bias, pool_w, pool_scale, w_mem_kv, m_q_gain, m_k_gain,
           w_branch_a, w_branch_b, w_branch_m, w_out):
    depth = w_in.shape[0]
    scale = HEAD_DIM ** -0.5 * LOG2_E
    for l in range(depth):
        a_w, p_w, m_w = w_branch_a.shape[1], w_branch_b.shape[1], w_branch_m.shape[1]
        gain_all = jnp.concatenate([
            a_q_gain[l].reshape(1, a_w) * scale, a_k_gain[l].reshape(1, a_w),
            jnp.ones((1, 2 * a_w + 2 * p_w), F32),
            m_q_gain[l].reshape(1, m_w) * scale, jnp.ones((1, m_w), F32)], axis=1)

        p, vb, h, (wg_bf, wo_bf, wa_bf, wb_bf, wm_bf, wkv_bf) = _in_proj(
            x, norm_gain[l:l + 1], w_in[l].astype(BF16), gain_all,
            [w_merge[l], w_out[l], w_branch_a[l], w_branch_b[l], w_branch_m[l], w_mem_kv[l]])
        memkv = _mem_kv(mem, mem_norm_gain[l:l + 1], wkv_bf, m_k_gain[l].reshape(1, m_w))
        bias = _rel_bias_tiles(a_rel_bias[l])
        oga = _band_attn(p, bias)
        ogm = _mem_attn(p, memkv)
        ogb = _pool_mix(vb, p, pool_w[l].astype(BF16), pool_scale[l:l + 1])
        x = _merge_out(x, h, oga, ogb, ogm, wg_bf, b_merge[l:l + 1], wa_bf, wb_bf, wm_bf, wo_bf)
    return x
```

```python
import functools

import jax
import jax.numpy as jnp
from jax import lax
from jax.experimental import pallas as pl
from jax.experimental.pallas import tpu as pltpu

F32 = jnp.float32
BF16 = jnp.bfloat16

CHUNK = 64
N_LEFT_CHUNKS = 8
A_HEADS = 8
M_HEADS = 4
HEAD_DIM = 128
REL_CLIP = 256
POOL_WINDOWS = (2, 4, 8, 16)
EPS = 1e-6
NEG_INF = -1e30
LOG2_E = 1.4426950408889634

LANES_V7X = 128
F32_SUBLANES_V7X = 8
BF16_SUBLANES_V7X = 16
MXU_COLS_V7X = 256
VMEM_LIMIT_V7X = 60000 * 1024

TM_IN = 1024
TN_IN = 1536
IN_K_CHUNKS = 8
TQ = 128
N_KT = N_LEFT_CHUNKS * CHUNK // TQ + 1
ATTN_TILES_PER_STEP = 4
ATTN_HEADS_PER_STEP = 4
TQ_MEM = 256
MEM_TILES_PER_STEP = 2
TR_POOL = 2048
POOL_HALO = 16
TM_OUT = 512
CN_OUT = 512

G_QA, G_KA, G_VA, G_GA, G_VB, G_GB, G_QM, G_GM = 0, 8, 16, 24, 32, 36, 40, 44
N_GROUPS = 48


def _params(sem):
    return pltpu.CompilerParams(dimension_semantics=sem, vmem_limit_bytes=VMEM_LIMIT_V7X)


def _rms_rows(t):
    return t * lax.rsqrt(jnp.mean(t * t, axis=-1, keepdims=True) + EPS)


def _sigmoid(z):
    return 1.0 / (1.0 + jnp.exp(-z))


def _fold_rows(src):
    sub = F32_SUBLANES_V7X
    return jnp.sum(src.reshape(src.shape[0] // sub, sub, src.shape[1]), axis=0)


def _tied_zero(folded, shape, dtype):
    bits = pltpu.bitcast(folded, jnp.uint32)
    zero = pltpu.bitcast((bits >> 16) >> 16, F32)
    return jnp.tile(zero, (shape[0] // F32_SUBLANES_V7X, shape[1] // folded.shape[1])).astype(dtype)


def _in_proj_kernel(*refs, n_steps, n_j, n_cast):
    x_ref, ng_ref, w_ref, gain_ref = refs[:4]
    cast_src = refs[4:4 + n_cast]
    p_ref, vb_ref, h_ref = refs[4 + n_cast:7 + n_cast]
    cast_dst = refs[7 + n_cast:7 + 2 * n_cast]
    h_scr, acc_scr = refs[7 + 2 * n_cast:]
    f = pl.program_id(0)
    groups = TN_IN // LANES_V7X
    k_chunk = h_scr.shape[2]
    j_cur = jnp.minimum(f, n_steps - 1) % n_j
    g_prev = (jnp.maximum(f - 1, 0) % n_j) * groups

    @pl.when(f == 0)
    def _():
        acc_scr[...] = jnp.zeros(acc_scr.shape, F32)

    @pl.when((j_cur == 0) & (f < n_steps))
    def _():
        h = (_rms_rows(x_ref[0]) * ng_ref[...]).astype(BF16)
        for c in range(IN_K_CHUNKS):
            h_scr[c] = h[:, c * k_chunk:(c + 1) * k_chunk]

    @pl.when(g_prev == G_VB // groups * groups)
    def _():
        vb0 = G_VB % groups
        for g in range(vb_ref.shape[2] // LANES_V7X):
            vb_ref[0, :, g * LANES_V7X:(g + 1) * LANES_V7X] = acc_scr[vb0 + g]

    pieces = [(n0, c) for n0 in range(0, TN_IN, MXU_COLS_V7X) for c in range(IN_K_CHUNKS)]
    stride = len(pieces) // groups

    def finish_group(g):
        sl = slice(g * LANES_V7X, (g + 1) * LANES_V7X)
        a = acc_scr[g]
        gi = g_prev + g
        is_norm = (gi < G_VA) | ((gi >= G_QM) & (gi < G_GM))
        is_silu = ((gi >= G_GA) & (gi < G_VB)) | ((gi >= G_GB) & (gi < G_QM)) | (gi >= G_GM)
        normed = _rms_rows(a) * gain_ref[:, sl]
        res = jnp.where(is_norm, normed, jnp.where(is_silu, a * _sigmoid(a), a))
        p_ref[0, g] = res.astype(BF16)
        return res

    def side_jobs():
        for src, dst in zip(cast_src, cast_dst):
            dst[...] = src[...].astype(BF16)
        per_step = IN_K_CHUNKS // n_j
        for i in range(per_step):
            h_ref[0, i] = h_scr[j_cur * per_step + i]

    @pl.when(f < n_steps)
    def _():
        accs = {}
        ties = {}
        for i, (n0, c) in enumerate(pieces):
            ks = slice(c * k_chunk, (c + 1) * k_chunk)
            lhs = h_scr[c]
            if i in ties:
                top = BF16_SUBLANES_V7X
                zero = _tied_zero(ties.pop(i), (top, k_chunk), BF16)
                lhs = jnp.concatenate([lhs[:top] + zero, lhs[top:]], axis=0)
            part = jnp.dot(lhs, w_ref[ks, n0:n0 + MXU_COLS_V7X], preferred_element_type=F32)
            accs[n0] = part if c == 0 else accs[n0] + part
            if i % stride == 0:
                ties[i + stride - 1] = _fold_rows(finish_group(i // stride))
        for n0, acc in accs.items():
            for gg in range(MXU_COLS_V7X // LANES_V7X):
                acc_scr[n0 // LANES_V7X + gg] = acc[:, gg * LANES_V7X:(gg + 1) * LANES_V7X]
        side_jobs()

    @pl.when(f == n_steps)
    def _():
        for g in range(groups):
            finish_group(g)
        side_jobs()


def _cast_blocks(rows, n_grid):
    blocks = 1
    while blocks * 2 <= n_grid and rows % (blocks * 2 * BF16_SUBLANES_V7X) == 0:
        blocks *= 2
    return blocks


def _in_proj(x, norm_gain, w_in_bf, gain_all, cast_weights):
    B, S, D = x.shape
    n_t = S // TM_IN
    n_j = w_in_bf.shape[1] // TN_IN
    n_steps = B * n_t * n_j
    groups = TN_IN // LANES_V7X
    vb_cols = (G_GB - G_VB) * LANES_V7X
    k_chunk = D // IN_K_CHUNKS

    def cur(f):
        f = jnp.minimum(f, n_steps - 1)
        return f // (n_t * n_j), (f // n_j) % n_t, f % n_j

    def prev(f):
        return cur(jnp.maximum(f - 1, 0))

    def cast_spec(w):
        blocks = _cast_blocks(w.shape[0], n_steps + 1)
        return pl.BlockSpec((w.shape[0] // blocks, w.shape[1]),
                            lambda f: (jnp.minimum(f, blocks - 1), 0))

    cast_specs = [cast_spec(w) for w in cast_weights]
    outs = pl.pallas_call(
        functools.partial(_in_proj_kernel, n_steps=n_steps, n_j=n_j, n_cast=len(cast_weights)),
        grid=(n_steps + 1,),
        in_specs=[
            pl.BlockSpec((1, TM_IN, D), lambda f: (cur(f)[0], cur(f)[1], 0)),
            pl.BlockSpec((1, D), lambda f: (0, 0)),
            pl.BlockSpec((D, TN_IN), lambda f: (0, cur(f)[2])),
            pl.BlockSpec((1, TN_IN), lambda f: (0, prev(f)[2])),
        ] + cast_specs,
        out_specs=[
            pl.BlockSpec((1, groups, TM_IN, LANES_V7X),
                         lambda f: (prev(f)[0], prev(f)[2], prev(f)[1], 0)),
            pl.BlockSpec((1, TM_IN, vb_cols), lambda f: (prev(f)[0], prev(f)[1], 0)),
            pl.BlockSpec((1, IN_K_CHUNKS // n_j, TM_IN, k_chunk),
                         lambda f: (cur(f)[0], cur(f)[2], cur(f)[1], 0)),
        ] + cast_specs,
        out_shape=[
            jax.ShapeDtypeStruct((B, N_GROUPS, S, LANES_V7X), BF16),
            jax.ShapeDtypeStruct((B, S, vb_cols), F32),
            jax.ShapeDtypeStruct((B, IN_K_CHUNKS, S, k_chunk), BF16),
        ] + [jax.ShapeDtypeStruct(w.shape, BF16) for w in cast_weights],
        scratch_shapes=[pltpu.VMEM((IN_K_CHUNKS, TM_IN, k_chunk), BF16),
                        pltpu.VMEM((groups, TM_IN, LANES_V7X), F32)],
        compiler_params=_params(("arbitrary",)),
        name="in_proj",
    )(x, norm_gain, w_in_bf, gain_all, *cast_weights)
    return outs[0], outs[1], outs[2], outs[3:]


def _mem_kv_kernel(mem_ref, mg_ref, w_ref, kg_ref, o_ref):
    mh = (_rms_rows(mem_ref[0]) * mg_ref[...]).astype(BF16)
    kv = jnp.dot(mh, w_ref[...], preferred_element_type=F32)
    for g in range(M_HEADS):
        sl = slice(g * HEAD_DIM, (g + 1) * HEAD_DIM)
        o_ref[0, g] = (_rms_rows(kv[:, sl]) * kg_ref[:, sl]).astype(BF16)
    for g in range(M_HEADS, 2 * M_HEADS):
        o_ref[0, g] = kv[:, g * HEAD_DIM:(g + 1) * HEAD_DIM].astype(BF16)


def _mem_kv(mem, mem_norm_gain, w_kv_bf, k_gain):
    B, N, D = mem.shape
    W = w_kv_bf.shape[1]
    return pl.pallas_call(
        _mem_kv_kernel,
        grid=(B,),
        in_specs=[
            pl.BlockSpec((1, N, D), lambda b: (b, 0, 0)),
            pl.BlockSpec((1, D), lambda b: (0, 0)),
            pl.BlockSpec((D, W), lambda b: (0, 0)),
            pl.BlockSpec((1, W // 2), lambda b: (0, 0)),
        ],
        out_specs=pl.BlockSpec((1, 2 * M_HEADS, N, HEAD_DIM), lambda b: (b, 0, 0, 0)),
        out_shape=jax.ShapeDtypeStruct((B, 2 * M_HEADS, N, HEAD_DIM), BF16),
        compiler_params=_params(("arbitrary",)),
        name="mem_kv",
    )(mem, mem_norm_gain, w_kv_bf, k_gain)


_BAND = N_KT * TQ
_RWRAP = _BAND + TQ


def _rel_bias_kernel(r_ref, o_ref):
    row = lax.broadcasted_iota(jnp.int32, (TQ, _RWRAP), 0)
    row_b = lax.broadcasted_iota(jnp.int32, (TQ, _BAND), 0)
    col_b = lax.broadcasted_iota(jnp.int32, (TQ, _BAND), 1)
    lo = jnp.where(row_b < CHUNK, 0, CHUNK)
    valid = (col_b >= lo) & (col_b < lo + _BAND - CHUNK)
    for h in range(A_HEADS):
        t = jnp.broadcast_to(r_ref[h:h + 1, :], (TQ, _RWRAP))
        for k in range(TQ.bit_length() - 1):
            t = jnp.where(((row >> k) & 1) == 1, pltpu.roll(t, 1 << k, 1), t)
        o_ref[h] = jnp.where(valid, t[:, :_BAND] * LOG2_E, NEG_INF)


def _rel_bias_tiles(rel_bias):
    H = rel_bias.shape[0]
    edge = jnp.broadcast_to(rel_bias[:, 2 * REL_CLIP:], (H, 2 * REL_CLIP))
    mid = jnp.flip(rel_bias[:, 2 * REL_CLIP + 1 - (_BAND - REL_CLIP):], axis=1)
    r_ext = jnp.concatenate([edge[:, :REL_CLIP], mid, edge[:, :_RWRAP - _BAND]], axis=1)
    return pl.pallas_call(
        _rel_bias_kernel,
        out_shape=jax.ShapeDtypeStruct((H, TQ, _BAND), F32),
        compiler_params=pltpu.CompilerParams(vmem_limit_bytes=VMEM_LIMIT_V7X),
        name="rel_bias",
    )(r_ext)


def _attention_pipeline(n_groups, tiles, scores_fn, values_fn, s_scr, p_scr, l_scr):
    rows, width = l_scr.shape[1:]

    def values(group):
        group = jnp.asarray(group, jnp.int32)
        for u in range(tiles):
            values_fn(group, u, p_scr[u], l_scr[u])

    def softmax():
        for u in range(tiles):
            s = s_scr[u]
            p = jnp.exp2(s - jnp.max(s, axis=-1, keepdims=True))
            l_scr[u] = jnp.broadcast_to(jnp.sum(p, axis=-1, keepdims=True), (rows, width))
            p_scr[u] = p.astype(BF16)

    def scores(group):
        group = jnp.asarray(group, jnp.int32)
        for u in range(tiles):
            s_scr[u] = scores_fn(group, u)

    def body(j, carry):
        values(j - 2)
        softmax()
        scores(j)
        return carry

    scores(0)
    softmax()
    scores(1)
    lax.fori_loop(2, n_groups, body, 0)
    values(n_groups - 2)
    softmax()
    values(n_groups - 1)


def _attention_scratch(tiles, rows, cols):
    return [pltpu.VMEM((tiles, rows, cols), F32), pltpu.VMEM((tiles, rows, cols), BF16),
            pltpu.VMEM((tiles, rows, HEAD_DIM), F32)]


_QK_DIMS = (((1,), (1,)), ((), ()))


def _band_attn_kernel(q_ref, k_ref, v_ref, g_ref, bias_ref, o_ref, *scratch):
    groups_per_head = q_ref.shape[2] // (TQ * ATTN_TILES_PER_STEP)
    col = lax.broadcasted_iota(jnp.int32, (TQ, _BAND), 1)

    def coords(group, u):
        head = group // groups_per_head
        qt = (group % groups_per_head) * ATTN_TILES_PER_STEP + u
        return head, qt, pl.multiple_of(qt * TQ, TQ)

    def band(ref, head, qt):
        tiles = []
        for a in range(N_KT):
            k0 = pl.multiple_of(jnp.maximum(qt - (N_KT - 1) + a, 0) * TQ, TQ)
            tiles.append(ref[0, head, pl.ds(k0, TQ), :])
        return jnp.concatenate(tiles, axis=0)

    def scores(group, u):
        head, qt, q0 = coords(group, u)
        s = lax.dot_general(q_ref[0, head, pl.ds(q0, TQ), :], band(k_ref, head, qt), _QK_DIMS,
                            preferred_element_type=F32)
        s = s + bias_ref[head]
        return jnp.where(col >= (N_KT - 1 - qt) * TQ, s, NEG_INF)

    def values(group, u, p, l):
        head, qt, q0 = coords(group, u)
        o = jnp.dot(p, band(v_ref, head, qt), preferred_element_type=F32) / l
        o_ref[0, head, pl.ds(q0, TQ), :] = (
            o * g_ref[0, head, pl.ds(q0, TQ), :].astype(F32)).astype(BF16)

    _attention_pipeline(ATTN_HEADS_PER_STEP * groups_per_head, ATTN_TILES_PER_STEP,
                        scores, values, *scratch)


def _band_attn(p, bias):
    B, _, S, _ = p.shape
    hb = ATTN_HEADS_PER_STEP
    blk = (1, hb, S, HEAD_DIM)

    def seg(g_first):
        return pl.BlockSpec(blk, lambda b, h: (b, g_first // hb + h, 0, 0))

    return pl.pallas_call(
        _band_attn_kernel,
        grid=(B, A_HEADS // hb),
        in_specs=[seg(G_QA), seg(G_KA), seg(G_VA), seg(G_GA),
                  pl.BlockSpec((hb, TQ, _BAND), lambda b, h: (h, 0, 0))],
        out_specs=seg(0),
        out_shape=jax.ShapeDtypeStruct((B, A_HEADS, S, HEAD_DIM), BF16),
        scratch_shapes=_attention_scratch(ATTN_TILES_PER_STEP, TQ, _BAND),
        compiler_params=_params(("parallel", "arbitrary")),
        name="band_attn",
    )(p, p, p, p, bias)


def _mem_attn_kernel(q_ref, g_ref, kv_ref, o_ref, *scratch):
    groups_per_head = q_ref.shape[2] // (TQ_MEM * MEM_TILES_PER_STEP)

    def coords(group, u):
        head = group // groups_per_head
        qt = (group % groups_per_head) * MEM_TILES_PER_STEP + u
        return head, pl.multiple_of(qt * TQ_MEM, TQ_MEM)

    def scores(group, u):
        head, q0 = coords(group, u)
        return lax.dot_general(q_ref[0, head, pl.ds(q0, TQ_MEM), :], kv_ref[0, head], _QK_DIMS,
                               preferred_element_type=F32)

    def values(group, u, p, l):
        head, q0 = coords(group, u)
        o = jnp.dot(p, kv_ref[0, M_HEADS + head], preferred_element_type=F32) / l
        o_ref[0, head, pl.ds(q0, TQ_MEM), :] = (
            o * g_ref[0, head, pl.ds(q0, TQ_MEM), :].astype(F32)).astype(BF16)

    _attention_pipeline(M_HEADS * groups_per_head, MEM_TILES_PER_STEP,
                        scores, values, *scratch)


def _mem_attn(p, memkv):
    B, _, S, _ = p.shape
    N = memkv.shape[2]
    blk = (1, M_HEADS, S, HEAD_DIM)
    return pl.pallas_call(
        _mem_attn_kernel,
        grid=(B,),
        in_specs=[
            pl.BlockSpec(blk, lambda b: (b, G_QM // M_HEADS, 0, 0)),
            pl.BlockSpec(blk, lambda b: (b, G_GM // M_HEADS, 0, 0)),
            pl.BlockSpec((1, 2 * M_HEADS, N, HEAD_DIM), lambda b: (b, 0, 0, 0)),
        ],
        out_specs=pl.BlockSpec(blk, lambda b: (b, 0, 0, 0)),
        out_shape=jax.ShapeDtypeStruct((B, M_HEADS, S, HEAD_DIM), BF16),
        scratch_shapes=_attention_scratch(MEM_TILES_PER_STEP, TQ_MEM, N),
        compiler_params=_params(("arbitrary",)),
        name="mem_attn",
    )(p, p, memkv)


def _pool_mix_kernel(vb_ref, prev_ref, g_ref, pw_ref, ps_ref, o_ref):
    t = pl.program_id(1)
    cur = vb_ref[0]
    prev = jnp.where(t > 0, prev_ref[0], 0.0)
    head_pos = lax.broadcasted_iota(jnp.int32, (POOL_HALO, 1), 0) + 1
    for g, w in enumerate(POOL_WINDOWS):
        assert w & (w - 1) == 0 and w <= POOL_HALO
        sl = slice(g * LANES_V7X, (g + 1) * LANES_V7X)
        acc = jnp.concatenate([prev[:, sl], cur[:, sl]], axis=0)
        d = 1
        while d < w:
            acc = acc + pltpu.roll(acc, d, 0)
            d *= 2
        sums = acc[POOL_HALO:, :]
        head_cnt = jnp.where(t == 0, jnp.minimum(head_pos, w), w).astype(F32)
        mean = jnp.concatenate([sums[:POOL_HALO] / head_cnt, sums[POOL_HALO:] * (1.0 / w)], axis=0)
        pooled = mean - cur[:, sl]
        mixed = jnp.dot(pooled.astype(BF16), pw_ref[g], preferred_element_type=F32)
        o_ref[0, g] = (mixed * ps_ref[:, sl] * g_ref[0, g].astype(F32)).astype(BF16)


def _pool_mix(vb, p, pool_w_bf, pool_scale):
    B, S, P = vb.shape
    G = len(POOL_WINDOWS)
    halo_blocks = TR_POOL // POOL_HALO
    return pl.pallas_call(
        _pool_mix_kernel,
        grid=(B, S // TR_POOL),
        in_specs=[
            pl.BlockSpec((1, TR_POOL, P), lambda b, t: (b, t, 0)),
            pl.BlockSpec((1, POOL_HALO, P), lambda b, t: (b, jnp.maximum(t * halo_blocks - 1, 0), 0)),
            pl.BlockSpec((1, G, TR_POOL, LANES_V7X), lambda b, t: (b, G_GB // G, t, 0)),
            pl.BlockSpec((G, LANES_V7X, LANES_V7X), lambda b, t: (0, 0, 0)),
            pl.BlockSpec((1, P), lambda b, t: (0, 0)),
        ],
        out_specs=pl.BlockSpec((1, G, TR_POOL, LANES_V7X), lambda b, t: (b, 0, t, 0)),
        out_shape=jax.ShapeDtypeStruct((B, G, S, LANES_V7X), BF16),
        compiler_params=_params(("parallel", "arbitrary")),
        name="pool_mix",
    )(vb, vb, p, pool_w_bf, pool_scale)


def _merge_out_kernel(x_ref, h_ref, oa_ref, ob_ref, om_ref,
                      wg0_ref, wg1_ref, wg2_ref, b0_ref, b1_ref, b2_ref,
                      wa_ref, wb_ref, wm_ref, wo_ref, out_ref):
    n = pl.program_id(2)

    @pl.when(n == 0)
    def _():
        out_ref[0] = x_ref[0]

    k_chunk = h_ref.shape[3]

    def branch(o_ref, w_ref, wg_ref, b_ref):
        o = jnp.concatenate([o_ref[0, g] for g in range(o_ref.shape[1])], axis=-1)
        pre = None
        for c in range(h_ref.shape[1]):
            part = jnp.dot(h_ref[0, c], wg_ref[c * k_chunk:(c + 1) * k_chunk, :],
                           preferred_element_type=F32)
            pre = part if pre is None else pre + part
        return _sigmoid(pre + b_ref[...]) * jnp.dot(o, w_ref[...], preferred_element_type=F32)

    y = (branch(oa_ref, wa_ref, wg0_ref, b0_ref)
         + branch(ob_ref, wb_ref, wg1_ref, b1_ref)
         + branch(om_ref, wm_ref, wg2_ref, b2_ref))
    out_ref[0] += jnp.dot(y.astype(BF16), wo_ref[...], preferred_element_type=F32)


def _merge_out(x, h, oga, ogb, ogm, w_merge_bf, b_merge, wa_bf, wb_bf, wm_bf, wo_bf):
    B, S, D = x.shape
    n_c = D // CN_OUT

    def gate_w(r):
        return pl.BlockSpec((D, CN_OUT), lambda b, t, n: (0, r * n_c + n))

    def gate_b(r):
        return pl.BlockSpec((1, CN_OUT), lambda b, t, n: (0, r * n_c + n))

    def heads(nh):
        return pl.BlockSpec((1, nh, TM_OUT, HEAD_DIM), lambda b, t, n: (b, 0, t, 0))

    def cols(rows):
        return pl.BlockSpec((rows, CN_OUT), lambda b, t, n: (0, n))

    rows = pl.BlockSpec((1, TM_OUT, D), lambda b, t, n: (b, t, 0))
    return pl.pallas_call(
        _merge_out_kernel,
        grid=(B, S // TM_OUT, n_c),
        in_specs=[
            rows, pl.BlockSpec((1, h.shape[1], TM_OUT, h.shape[3]), lambda b, t, n: (b, 0, t, 0)),
            heads(oga.shape[1]), heads(ogb.shape[1]), heads(ogm.shape[1]),
            gate_w(0), gate_w(1), gate_w(2), gate_b(0), gate_b(1), gate_b(2),
            cols(wa_bf.shape[0]), cols(wb_bf.shape[0]), cols(wm_bf.shape[0]),
            pl.BlockSpec((CN_OUT, D), lambda b, t, n: (n, 0)),
        ],
        out_specs=rows,
        out_shape=jax.ShapeDtypeStruct((B, S, D), F32),
        compiler_params=_params(("parallel", "arbitrary", "arbitrary")),
        name="merge_out",
    )(x, h, oga, ogb, ogm, w_merge_bf, w_merge_bf, w_merge_bf,
      b_merge, b_merge, b_merge, wa_bf, wb_bf, wm_bf, wo_bf)


def kernel(x, mem, norm_gain, mem_norm_gain, w_in, w_merge, b_merge, a_q_gain, a_k_gain,
           a_rel_bias, pool_w, pool_scale, w_mem_kv, m_q_gain, m_k_gain,
           w_branch_a, w_branch_b, w_branch_m, w_out):
    depth = w_in.shape[0]
    scale = HEAD_DIM ** -0.5 * LOG2_E
    for l in range(depth):
        a_w, p_w, m_w = w_branch_a.shape[1], w_branch_b.shape[1], w_branch_m.shape[1]
        assert w_in.shape[2] == N_GROUPS * LANES_V7X and a_w == (G_KA - G_QA) * LANES_V7X
        assert p_w == (G_GB - G_VB) * LANES_V7X and m_w == (G_GM - G_QM) * LANES_V7X
        gain_all = jnp.concatenate([
            a_q_gain[l].reshape(1, a_w) * scale, a_k_gain[l].reshape(1, a_w),
            jnp.ones((1, 2 * a_w + 2 * p_w), F32),
            m_q_gain[l].reshape(1, m_w) * scale, jnp.ones((1, m_w), F32)], axis=1)

        p, vb, h, (wg_bf, wo_bf, wa_bf, wb_bf, wm_bf, wkv_bf) = _in_proj(
            x, norm_gain[l:l + 1], w_in[l].astype(BF16), gain_all,
            [w_merge[l], w_out[l], w_branch_a[l], w_branch_b[l], w_branch_m[l], w_mem_kv[l]])
        memkv = _mem_kv(mem, mem_norm_gain[l:l + 1], wkv_bf, m_k_gain[l].reshape(1, m_w))
        bias = _rel_bias_tiles(a_rel_bias[l])
        oga = _band_attn(p, bias)
        ogm = _mem_attn(p, memkv)
        ogb = _pool_mix(vb, p, pool_w[l].astype(BF16), pool_scale[l:l + 1])
        x = _merge_out(x, h, oga, ogb, ogm, wg_bf, b_merge[l:l + 1], wa_bf, wb_bf, wm_bf, wo_bf)
    return x
```

```python
import functools

import jax
import jax.numpy as jnp
from jax import lax
from jax.experimental import pallas as pl
from jax.experimental.pallas import tpu as pltpu

F32 = jnp.float32
BF16 = jnp.bfloat16

CHUNK = 64
N_LEFT_CHUNKS = 8
A_HEADS = 8
M_HEADS = 4
HEAD_DIM = 128
REL_CLIP = 256
POOL_WINDOWS = (2, 4, 8, 16)
EPS = 1e-6
NEG_INF = -1e30
LOG2_E = 1.4426950408889634

LANES_V7X = 128
F32_SUBLANES_V7X = 8
BF16_SUBLANES_V7X = 16
MXU_COLS_V7X = 256
VMEM_LIMIT_V7X = 60000 * 1024

TM_IN = 1024
TN_IN = 1536
IN_K_CHUNKS = 8
TQ = 128
N_KT = N_LEFT_CHUNKS * CHUNK // TQ + 1
ATTN_TILES_PER_STEP = 4
ATTN_HEADS_PER_STEP = 4
TQ_MEM = 256
MEM_TILES_PER_STEP = 2
TR_POOL = 2048
POOL_HALO = 16
TM_OUT = 512
CN_OUT = 512

G_QA, G_KA, G_VA, G_GA, G_VB, G_GB, G_QM, G_GM = 0, 8, 16, 24, 32, 36, 40, 44
N_GROUPS = 48


def _params(sem):
    return pltpu.CompilerParams(dimension_semantics=sem, vmem_limit_bytes=VMEM_LIMIT_V7X)


def _rms_rows(t):
    return t * lax.rsqrt(jnp.mean(t * t, axis=-1, keepdims=True) + EPS)


def _sigmoid(z):
    return 1.0 / (1.0 + jnp.exp(-z))


def _fold_rows(src):
    sub = F32_SUBLANES_V7X
    return jnp.sum(src.reshape(src.shape[0] // sub, sub, src.shape[1]), axis=0)


def _tied_zero(folded, shape, dtype):
    bits = pltpu.bitcast(folded, jnp.uint32)
    zero = pltpu.bitcast((bits >> 16) >> 16, F32)
    return jnp.tile(zero, (shape[0] // F32_SUBLANES_V7X, shape[1] // folded.shape[1])).astype(dtype)


def _in_proj_kernel(*refs, n_steps, n_j, n_cast):
    x_ref, ng_ref, w_ref, gain_ref = refs[:4]
    cast_src = refs[4:4 + n_cast]
    p_ref, vb_ref, h_ref = refs[4 + n_cast:7 + n_cast]
    cast_dst = refs[7 + n_cast:7 + 2 * n_cast]
    h_scr, acc_scr = refs[7 + 2 * n_cast:]
    f = pl.program_id(0)
    groups = TN_IN // LANES_V7X
    k_chunk = h_scr.shape[2]
    j_cur = jnp.minimum(f, n_steps - 1) % n_j
    g_prev = (jnp.maximum(f - 1, 0) % n_j) * groups

    @pl.when(f == 0)
    def _():
        acc_scr[...] = jnp.zeros(acc_scr.shape, F32)

    @pl.when((j_cur == 0) & (f < n_steps))
    def _():
        h = (_rms_rows(x_ref[0]) * ng_ref[...]).astype(BF16)
        for c in range(IN_K_CHUNKS):
            h_scr[c] = h[:, c * k_chunk:(c + 1) * k_chunk]

    @pl.when(g_prev == G_VB // groups * groups)
    def _():
        vb0 = G_VB % groups
        for g in range(vb_ref.shape[2] // LANES_V7X):
            vb_ref[0, :, g * LANES_V7X:(g + 1) * LANES_V7X] = acc_scr[vb0 + g]

    pieces = [(n0, c) for n0 in range(0, TN_IN, MXU_COLS_V7X) for c in range(IN_K_CHUNKS)]
    stride = len(pieces) // groups

    def finish_group(g):
        sl = slice(g * LANES_V7X, (g + 1) * LANES_V7X)
        a = acc_scr[g]
        gi = g_prev + g
        is_norm = (gi < G_VA) | ((gi >= G_QM) & (gi < G_GM))
        is_silu = ((gi >= G_GA) & (gi < G_VB)) | ((gi >= G_GB) & (gi < G_QM)) | (gi >= G_GM)
        normed = _rms_rows(a) * gain_ref[:, sl]
        res = jnp.where(is_norm, normed, jnp.where(is_silu, a * _sigmoid(a), a))
        p_ref[0, g] = res.astype(BF16)
        return res

    def side_jobs():
        for src, dst in zip(cast_src, cast_dst):
            dst[...] = src[...].astype(BF16)
        per_step = IN_K_CHUNKS // n_j
        for i in range(per_step):
            h_ref[0, i] = h_scr[j_cur * per_step + i]

    @pl.when(f < n_steps)
    def _():
        accs = {}
        ties = {}
        for i, (n0, c) in enumerate(pieces):
            ks = slice(c * k_chunk, (c + 1) * k_chunk)
            lhs = h_scr[c]
            if i in ties:
                top = BF16_SUBLANES_V7X
                zero = _tied_zero(ties.pop(i), (top, k_chunk), BF16)
                lhs = jnp.concatenate([lhs[:top] + zero, lhs[top:]], axis=0)
            part = jnp.dot(lhs, w_ref[ks, n0:n0 + MXU_COLS_V7X], preferred_element_type=F32)
            accs[n0] = part if c == 0 else accs[n0] + part
            if i % stride == 0:
                ties[i + stride - 1] = _fold_rows(finish_group(i // stride))
        for n0, acc in accs.items():
            for gg in range(MXU_COLS_V7X // LANES_V7X):
                acc_scr[n0 // LANES_V7X + gg] = acc[:, gg * LANES_V7X:(gg + 1) * LANES_V7X]
        side_jobs()

    @pl.when(f == n_steps)
    def _():
        for g in range(groups):
            finish_group(g)
        side_jobs()


def _cast_blocks(rows, n_grid):
    blocks = 1
    while blocks * 2 <= n_grid and rows % (blocks * 2 * BF16_SUBLANES_V7X) == 0:
        blocks *= 2
    return blocks


def _in_proj(x, norm_gain, w_in_bf, gain_all, cast_weights):
    B, S, D = x.shape
    n_t = S // TM_IN
    n_j = w_in_bf.shape[1] // TN_IN
    n_steps = B * n_t * n_j
    groups = TN_IN // LANES_V7X
    vb_cols = (G_GB - G_VB) * LANES_V7X
    k_chunk = D // IN_K_CHUNKS

    def cur(f):
        f = jnp.minimum(f, n_steps - 1)
        return f // (n_t * n_j), (f // n_j) % n_t, f % n_j

    def prev(f):
        return cur(jnp.maximum(f - 1, 0))

    def cast_spec(w):
        blocks = _cast_blocks(w.shape[0], n_steps + 1)
        return pl.BlockSpec((w.shape[0] // blocks, w.shape[1]),
                            lambda f: (jnp.minimum(f, blocks - 1), 0))

    cast_specs = [cast_spec(w) for w in cast_weights]
    outs = pl.pallas_call(
        functools.partial(_in_proj_kernel, n_steps=n_steps, n_j=n_j, n_cast=len(cast_weights)),
        grid=(n_steps + 1,),
        in_specs=[
            pl.BlockSpec((1, TM_IN, D), lambda f: (cur(f)[0], cur(f)[1], 0)),
            pl.BlockSpec((1, D), lambda f: (0, 0)),
            pl.BlockSpec((D, TN_IN), lambda f: (0, cur(f)[2])),
            pl.BlockSpec((1, TN_IN), lambda f: (0, prev(f)[2])),
        ] + cast_specs,
        out_specs=[
            pl.BlockSpec((1, groups, TM_IN, LANES_V7X),
                         lambda f: (prev(f)[0], prev(f)[2], prev(f)[1], 0)),
            pl.BlockSpec((1, TM_IN, vb_cols), lambda f: (prev(f)[0], prev(f)[1], 0)),
            pl.BlockSpec((1, IN_K_CHUNKS // n_j, TM_IN, k_chunk),
                         lambda f: (cur(f)[0], cur(f)[2], cur(f)[1], 0)),
        ] + cast_specs,
        out_shape=[
            jax.ShapeDtypeStruct((B, N_GROUPS, S, LANES_V7X), BF16),
            jax.ShapeDtypeStruct((B, S, vb_cols), F32),
            jax.ShapeDtypeStruct((B, IN_K_CHUNKS, S, k_chunk), BF16),
        ] + [jax.ShapeDtypeStruct(w.shape, BF16) for w in cast_weights],
        scratch_shapes=[pltpu.VMEM((IN_K_CHUNKS, TM_IN, k_chunk), BF16),
                        pltpu.VMEM((groups, TM_IN, LANES_V7X), F32)],
        compiler_params=_params(("arbitrary",)),
        name="in_proj",
    )(x, norm_gain, w_in_bf, gain_all, *cast_weights)
    return outs[0], outs[1], outs[2], outs[3:]


def _mem_kv_kernel(mem_ref, mg_ref, w_ref, kg_ref, o_ref):
    mh = (_rms_rows(mem_ref[0]) * mg_ref[...]).astype(BF16)
    kv = jnp.dot(mh, w_ref[...], preferred_element_type=F32)
    for g in range(M_HEADS):
        sl = slice(g * HEAD_DIM, (g + 1) * HEAD_DIM)
        o_ref[0, g] = (_rms_rows(kv[:, sl]) * kg_ref[:, sl]).astype(BF16)
    for g in range(M_HEADS, 2 * M_HEADS):
        o_ref[0, g] = kv[:, g * HEAD_DIM:(g + 1) * HEAD_DIM].astype(BF16)


def _mem_kv(mem, mem_norm_gain, w_kv_bf, k_gain):
    B, N, D = mem.shape
    W = w_kv_bf.shape[1]
    return pl.pallas_call(
        _mem_kv_kernel,
        grid=(B,),
        in_specs=[
            pl.BlockSpec((1, N, D), lambda b: (b, 0, 0)),
            pl.BlockSpec((1, D), lambda b: (0, 0)),
            pl.BlockSpec((D, W), lambda b: (0, 0)),
            pl.BlockSpec((1, W // 2), lambda b: (0, 0)),
        ],
        out_specs=pl.BlockSpec((1, 2 * M_HEADS, N, HEAD_DIM), lambda b: (b, 0, 0, 0)),
        out_shape=jax.ShapeDtypeStruct((B, 2 * M_HEADS, N, HEAD_DIM), BF16),
        compiler_params=_params(("arbitrary",)),
        name="mem_kv",
    )(mem, mem_norm_gain, w_kv_bf, k_gain)


_BAND = N_KT * TQ
_RWRAP = _BAND + TQ


def _rel_bias_kernel(r_ref, o_ref):
    row = lax.broadcasted_iota(jnp.int32, (TQ, _RWRAP), 0)
    row_b = lax.broadcasted_iota(jnp.int32, (TQ, _BAND), 0)
    col_b = lax.broadcasted_iota(jnp.int32, (TQ, _BAND), 1)
    lo = jnp.where(row_b < CHUNK, 0, CHUNK)
    valid = (col_b >= lo) & (col_b < lo + _BAND - CHUNK)
    for h in range(A_HEADS):
        t = jnp.broadcast_to(r_ref[h:h + 1, :], (TQ, _RWRAP))
        for k in range(TQ.bit_length() - 1):
            t = jnp.where(((row >> k) & 1) == 1, pltpu.roll(t, 1 << k, 1), t)
        o_ref[h] = jnp.where(valid, t[:, :_BAND] * LOG2_E, NEG_INF)


def _rel_bias_tiles(rel_bias):
    H = rel_bias.shape[0]
    edge = jnp.broadcast_to(rel_bias[:, 2 * REL_CLIP:], (H, 2 * REL_CLIP))
    mid = jnp.flip(rel_bias[:, 2 * REL_CLIP + 1 - (_BAND - REL_CLIP):], axis=1)
    r_ext = jnp.concatenate([edge[:, :REL_CLIP], mid, edge[:, :_RWRAP - _BAND]], axis=1)
    return pl.pallas_call(
        _rel_bias_kernel,
        out_shape=jax.ShapeDtypeStruct((H, TQ, _BAND), F32),
        compiler_params=pltpu.CompilerParams(vmem_limit_bytes=VMEM_LIMIT_V7X),
        name="rel_bias",
    )(r_ext)


def _attention_pipeline(n_groups, tiles, scores_fn, values_fn, s_scr, p_scr, l_scr):
    rows, width = l_scr.shape[1:]
    lane_tiles = s_scr.shape[1]

    def put(ref, u, val):
        for a in range(lane_tiles):
            ref[u, a] = val[:, a * width:(a + 1) * width]

    def get(ref, u):
        return jnp.concatenate([ref[u, a] for a in range(lane_tiles)], axis=1)

    def values(group):
        group = jnp.asarray(group, jnp.int32)
        for u in range(tiles):
            values_fn(group, u, get(p_scr, u), l_scr[u])

    def softmax():
        for u in range(tiles):
            s = get(s_scr, u)
            p = jnp.exp2(s - jnp.max(s, axis=-1, keepdims=True))
            l_scr[u] = jnp.broadcast_to(jnp.sum(p, axis=-1, keepdims=True), (rows, width))
            put(p_scr, u, p.astype(BF16))

    def scores(group):
        group = jnp.asarray(group, jnp.int32)
        for u in range(tiles):
            put(s_scr, u, scores_fn(group, u))

    def body(j, carry):
        values(j - 2)
        softmax()
        scores(j)
        return carry

    scores(0)
    softmax()
    scores(1)
    lax.fori_loop(2, n_groups, body, 0)
    values(n_groups - 2)
    softmax()
    values(n_groups - 1)


def _attention_scratch(tiles, rows, cols):
    lane_tiles = cols // HEAD_DIM
    return [pltpu.VMEM((tiles, lane_tiles, rows, HEAD_DIM), F32),
            pltpu.VMEM((tiles, lane_tiles, rows, HEAD_DIM), BF16),
            pltpu.VMEM((tiles, rows, HEAD_DIM), F32)]


_QK_DIMS = (((1,), (1,)), ((), ()))


def _band_attn_kernel(q_ref, k_ref, v_ref, g_ref, bias_ref, o_ref, *scratch):
    groups_per_head = q_ref.shape[2] // (TQ * ATTN_TILES_PER_STEP)
    col = lax.broadcasted_iota(jnp.int32, (TQ, _BAND), 1)

    def coords(group, u):
        head = group // groups_per_head
        qt = (group % groups_per_head) * ATTN_TILES_PER_STEP + u
        return head, qt, pl.multiple_of(qt * TQ, TQ)

    def band(ref, head, qt):
        tiles = []
        for a in range(N_KT):
            k0 = pl.multiple_of(jnp.maximum(qt - (N_KT - 1) + a, 0) * TQ, TQ)
            tiles.append(ref[0, head, pl.ds(k0, TQ), :])
        return jnp.concatenate(tiles, axis=0)

    def scores(group, u):
        head, qt, q0 = coords(group, u)
        s = lax.dot_general(q_ref[0, head, pl.ds(q0, TQ), :], band(k_ref, head, qt), _QK_DIMS,
                            preferred_element_type=F32)
        s = s + bias_ref[head]
        return jnp.where(col >= (N_KT - 1 - qt) * TQ, s, NEG_INF)

    def values(group, u, p, l):
        head, qt, q0 = coords(group, u)
        o = jnp.dot(p, band(v_ref, head, qt), preferred_element_type=F32) / l
        o_ref[0, head, pl.ds(q0, TQ), :] = (
            o * g_ref[0, head, pl.ds(q0, TQ), :].astype(F32)).astype(BF16)

    _attention_pipeline(ATTN_HEADS_PER_STEP * groups_per_head, ATTN_TILES_PER_STEP,
                        scores, values, *scratch)


def _band_attn(p, bias):
    B, _, S, _ = p.shape
    hb = ATTN_HEADS_PER_STEP
    blk = (1, hb, S, HEAD_DIM)

    def seg(g_first):
        return pl.BlockSpec(blk, lambda b, h: (b, g_first // hb + h, 0, 0))

    return pl.pallas_call(
        _band_attn_kernel,
        grid=(B, A_HEADS // hb),
        in_specs=[seg(G_QA), seg(G_KA), seg(G_VA), seg(G_GA),
                  pl.BlockSpec((hb, TQ, _BAND), lambda b, h: (h, 0, 0))],
        out_specs=seg(0),
        out_shape=jax.ShapeDtypeStruct((B, A_HEADS, S, HEAD_DIM), BF16),
        scratch_shapes=_attention_scratch(ATTN_TILES_PER_STEP, TQ, _BAND),
        compiler_params=_params(("parallel", "arbitrary")),
        name="band_attn",
    )(p, p, p, p, bias)


def _mem_attn_kernel(q_ref, g_ref, kv_ref, o_ref, *scratch):
    groups_per_head = q_ref.shape[2] // (TQ_MEM * MEM_TILES_PER_STEP)

    def coords(group, u):
        head = group // groups_per_head
        qt = (group % groups_per_head) * MEM_TILES_PER_STEP + u
        return head, pl.multiple_of(qt * TQ_MEM, TQ_MEM)

    def scores(group, u):
        head, q0 = coords(group, u)
        return lax.dot_general(q_ref[0, head, pl.ds(q0, TQ_MEM), :], kv_ref[0, head], _QK_DIMS,
                               preferred_element_type=F32)

    def values(group, u, p, l):
        head, q0 = coords(group, u)
        o = jnp.dot(p, kv_ref[0, M_HEADS + head], preferred_element_type=F32) / l
        o_ref[0, head, pl.ds(q0, TQ_MEM), :] = (
            o * g_ref[0, head, pl.ds(q0, TQ_MEM), :].astype(F32)).astype(BF16)

    _attention_pipeline(M_HEADS * groups_per_head, MEM_TILES_PER_STEP,
                        scores, values, *scratch)


def _mem_attn(p, memkv):
    B, _, S, _ = p.shape
    N = memkv.shape[2]
    blk = (1, M_HEADS, S, HEAD_DIM)
    return pl.pallas_call(
        _mem_attn_kernel,
        grid=(B,),
        in_specs=[
            pl.BlockSpec(blk, lambda b: (b, G_QM // M_HEADS, 0, 0)),
            pl.BlockSpec(blk, lambda b: (b, G_GM // M_HEADS, 0, 0)),
            pl.BlockSpec((1, 2 * M_HEADS, N, HEAD_DIM), lambda b: (b, 0, 0, 0)),
        ],
        out_specs=pl.BlockSpec(blk, lambda b: (b, 0, 0, 0)),
        out_shape=jax.ShapeDtypeStruct((B, M_HEADS, S, HEAD_DIM), BF16),
        scratch_shapes=_attention_scratch(MEM_TILES_PER_STEP, TQ_MEM, N),
        compiler_params=_params(("arbitrary",)),
        name="mem_attn",
    )(p, p, memkv)


def _pool_mix_kernel(vb_ref, prev_ref, g_ref, pw_ref, ps_ref, o_ref):
    t = pl.program_id(1)
    cur = vb_ref[0]
    prev = jnp.where(t > 0, prev_ref[0], 0.0)
    head_pos = lax.broadcasted_iota(jnp.int32, (POOL_HALO, 1), 0) + 1
    for g, w in enumerate(POOL_WINDOWS):
        assert w & (w - 1) == 0 and w <= POOL_HALO
        sl = slice(g * LANES_V7X, (g + 1) * LANES_V7X)
        acc = jnp.concatenate([prev[:, sl], cur[:, sl]], axis=0)
        d = 1
        while d < w:
            acc = acc + pltpu.roll(acc, d, 0)
            d *= 2
        sums = acc[POOL_HALO:, :]
        head_cnt = jnp.where(t == 0, jnp.minimum(head_pos, w), w).astype(F32)
        mean = jnp.concatenate([sums[:POOL_HALO] / head_cnt, sums[POOL_HALO:] * (1.0 / w)], axis=0)
        pooled = mean - cur[:, sl]
        mixed = jnp.dot(pooled.astype(BF16), pw_ref[g], preferred_element_type=F32)
        o_ref[0, g] = (mixed * ps_ref[:, sl] * g_ref[0, g].astype(F32)).astype(BF16)


def _pool_mix(vb, p, pool_w_bf, pool_scale):
    B, S, P = vb.shape
    G = len(POOL_WINDOWS)
    halo_blocks = TR_POOL // POOL_HALO
    return pl.pallas_call(
        _pool_mix_kernel,
        grid=(B, S // TR_POOL),
        in_specs=[
            pl.BlockSpec((1, TR_POOL, P), lambda b, t: (b, t, 0)),
            pl.BlockSpec((1, POOL_HALO, P), lambda b, t: (b, jnp.maximum(t * halo_blocks - 1, 0), 0)),
            pl.BlockSpec((1, G, TR_POOL, LANES_V7X), lambda b, t: (b, G_GB // G, t, 0)),
            pl.BlockSpec((G, LANES_V7X, LANES_V7X), lambda b, t: (0, 0, 0)),
            pl.BlockSpec((1, P), lambda b, t: (0, 0)),
        ],
        out_specs=pl.BlockSpec((1, G, TR_POOL, LANES_V7X), lambda b, t: (b, 0, t, 0)),
        out_shape=jax.ShapeDtypeStruct((B, G, S, LANES_V7X), BF16),
        compiler_params=_params(("parallel", "arbitrary")),
        name="pool_mix",
    )(vb, vb, p, pool_w_bf, pool_scale)


def _merge_out_kernel(x_ref, h_ref, oa_ref, ob_ref, om_ref,
                      wg0_ref, wg1_ref, wg2_ref, b0_ref, b1_ref, b2_ref,
                      wa_ref, wb_ref, wm_ref, wo_ref, out_ref):
    n = pl.program_id(2)

    @pl.when(n == 0)
    def _():
        out_ref[0] = x_ref[0]

    k_chunk = h_ref.shape[3]

    def branch(o_ref, w_ref, wg_ref, b_ref):
        o = jnp.concatenate([o_ref[0, g] for g in range(o_ref.shape[1])], axis=-1)
        pre = None
        for c in range(h_ref.shape[1]):
            part = jnp.dot(h_ref[0, c], wg_ref[c * k_chunk:(c + 1) * k_chunk, :],
                           preferred_element_type=F32)
            pre = part if pre is None else pre + part
        return _sigmoid(pre + b_ref[...]) * jnp.dot(o, w_ref[...], preferred_element_type=F32)

    y = (branch(oa_ref, wa_ref, wg0_ref, b0_ref)
         + branch(ob_ref, wb_ref, wg1_ref, b1_ref)
         + branch(om_ref, wm_ref, wg2_ref, b2_ref))
    out_ref[0] += jnp.dot(y.astype(BF16), wo_ref[...], preferred_element_type=F32)


def _merge_out(x, h, oga, ogb, ogm, w_merge_bf, b_merge, wa_bf, wb_bf, wm_bf, wo_bf):
    B, S, D = x.shape
    n_c = D // CN_OUT

    def gate_w(r):
        return pl.BlockSpec((D, CN_OUT), lambda b, t, n: (0, r * n_c + n))

    def gate_b(r):
        return pl.BlockSpec((1, CN_OUT), lambda b, t, n: (0, r * n_c + n))

    def heads(nh):
        return pl.BlockSpec((1, nh, TM_OUT, HEAD_DIM), lambda b, t, n: (b, 0, t, 0))

    def cols(rows):
        return pl.BlockSpec((rows, CN_OUT), lambda b, t, n: (0, n))

    rows = pl.BlockSpec((1, TM_OUT, D), lambda b, t, n: (b, t, 0))
    return pl.pallas_call(
        _merge_out_kernel,
        grid=(B, S // TM_OUT, n_c),
        in_specs=[
            rows, pl.BlockSpec((1, h.shape[1], TM_OUT, h.shape[3]), lambda b, t, n: (b, 0, t, 0)),
            heads(oga.shape[1]), heads(ogb.shape[1]), heads(ogm.shape[1]),
            gate_w(0), gate_w(1), gate_w(2), gate_b(0), gate_b(1), gate_b(2),
            cols(wa_bf.shape[0]), cols(wb_bf.shape[0]), cols(wm_bf.shape[0]),
            pl.BlockSpec((CN_OUT, D), lambda b, t, n: (n, 0)),
        ],
        out_specs=rows,
        out_shape=jax.ShapeDtypeStruct((B, S, D), F32),
        compiler_params=_params(("parallel", "arbitrary", "arbitrary")),
        name="merge_out",
    )(x, h, oga, ogb, ogm, w_merge_bf, w_merge_bf, w_merge_bf,
      b_merge, b_merge, b_merge, wa_bf, wb_bf, wm_bf, wo_bf)


def kernel(x, mem, norm_gain, mem_norm_gain, w_in, w_merge, b_merge, a_q_gain, a_k_gain,
           a_rel_bias, pool_w, pool_scale, w_mem_kv, m_q_gain, m_k_gain,
           w_branch_a, w_branch_b, w_branch_m, w_out):
    depth = w_in.shape[0]
    scale = HEAD_DIM ** -0.5 * LOG2_E
    for l in range(depth):
        a_w, p_w, m_w = w_branch_a.shape[1], w_branch_b.shape[1], w_branch_m.shape[1]
        assert w_in.shape[2] == N_GROUPS * LANES_V7X and a_w == (G_KA - G_QA) * LANES_V7X
        assert p_w == (G_GB - G_VB) * LANES_V7X and m_w == (G_GM - G_QM) * LANES_V7X
        gain_all = jnp.concatenate([
            a_q_gain[l].reshape(1, a_w) * scale, a_k_gain[l].reshape(1, a_w),
            jnp.ones((1, 2 * a_w + 2 * p_w), F32),
            m_q_gain[l].reshape(1, m_w) * scale, jnp.ones((1, m_w), F32)], axis=1)

        p, vb, h, (wg_bf, wo_bf, wa_bf, wb_bf, wm_bf, wkv_bf) = _in_proj(
            x, norm_gain[l:l + 1], w_in[l].astype(BF16), gain_all,
            [w_merge[l], w_out[l], w_branch_a[l], w_branch_b[l], w_branch_m[l], w_mem_kv[l]])
        memkv = _mem_kv(mem, mem_norm_gain[l:l + 1], wkv_bf, m_k_gain[l].reshape(1, m_w))
        bias = _rel_bias_tiles(a_rel_bias[l])
        oga = _band_attn(p, bias)
        ogm = _mem_attn(p, memkv)
        ogb = _pool_mix(vb, p, pool_w[l].astype(BF16), pool_scale[l:l + 1])
        x = _merge_out(x, h, oga, ogb, ogm, wg_bf, b_merge[l:l + 1], wa_bf, wb_bf, wm_bf, wo_bf)
    return x
```

```python
import functools

import jax
import jax.numpy as jnp
from jax import lax
from jax.experimental import pallas as pl
from jax.experimental.pallas import tpu as pltpu

F32 = jnp.float32
BF16 = jnp.bfloat16

CHUNK = 64
N_LEFT_CHUNKS = 8
A_HEADS = 8
M_HEADS = 4
HEAD_DIM = 128
REL_CLIP = 256
POOL_WINDOWS = (2, 4, 8, 16)
EPS = 1e-6
NEG_INF = -1e30
LOG2_E = 1.4426950408889634

LANES_V7X = 128
F32_SUBLANES_V7X = 8
BF16_SUBLANES_V7X = 16
MXU_COLS_V7X = 256
VMEM_LIMIT_V7X = 60000 * 1024

TM_IN = 1024
TN_IN = 1536
IN_K_CHUNKS = 8
TQ = 128
N_KT = N_LEFT_CHUNKS * CHUNK // TQ + 1
ATTN_TILES_PER_STEP = 8
ATTN_HEADS_PER_STEP = 4
TQ_MEM = 256
MEM_TILES_PER_STEP = 2
TR_POOL = 2048
POOL_HALO = 16
TM_OUT = 512
CN_OUT = 512

G_QA, G_KA, G_VA, G_GA, G_VB, G_GB, G_QM, G_GM = 0, 8, 16, 24, 32, 36, 40, 44
N_GROUPS = 48


def _params(sem):
    return pltpu.CompilerParams(dimension_semantics=sem, vmem_limit_bytes=VMEM_LIMIT_V7X)


def _rms_rows(t):
    return t * lax.rsqrt(jnp.mean(t * t, axis=-1, keepdims=True) + EPS)


def _sigmoid(z):
    return 1.0 / (1.0 + jnp.exp(-z))


def _fold_rows(src):
    sub = F32_SUBLANES_V7X
    return jnp.sum(src.reshape(src.shape[0] // sub, sub, src.shape[1]), axis=0)


def _tied_zero(folded, shape, dtype):
    bits = pltpu.bitcast(folded, jnp.uint32)
    zero = pltpu.bitcast((bits >> 16) >> 16, F32)
    return jnp.tile(zero, (shape[0] // F32_SUBLANES_V7X, shape[1] // folded.shape[1])).astype(dtype)


def _in_proj_kernel(*refs, n_steps, n_j, n_cast):
    x_ref, ng_ref, w_ref, gain_ref = refs[:4]
    cast_src = refs[4:4 + n_cast]
    p_ref, vb_ref, h_ref = refs[4 + n_cast:7 + n_cast]
    cast_dst = refs[7 + n_cast:7 + 2 * n_cast]
    h_scr, acc_scr = refs[7 + 2 * n_cast:]
    f = pl.program_id(0)
    groups = TN_IN // LANES_V7X
    k_chunk = h_scr.shape[2]
    j_cur = jnp.minimum(f, n_steps - 1) % n_j
    g_prev = (jnp.maximum(f - 1, 0) % n_j) * groups

    @pl.when(f == 0)
    def _():
        acc_scr[...] = jnp.zeros(acc_scr.shape, F32)

    @pl.when((j_cur == 0) & (f < n_steps))
    def _():
        h = (_rms_rows(x_ref[0]) * ng_ref[...]).astype(BF16)
        for c in range(IN_K_CHUNKS):
            h_scr[c] = h[:, c * k_chunk:(c + 1) * k_chunk]

    @pl.when(g_prev == G_VB // groups * groups)
    def _():
        vb0 = G_VB % groups
        for g in range(vb_ref.shape[2] // LANES_V7X):
            vb_ref[0, :, g * LANES_V7X:(g + 1) * LANES_V7X] = acc_scr[vb0 + g]

    pieces = [(n0, c) for n0 in range(0, TN_IN, MXU_COLS_V7X) for c in range(IN_K_CHUNKS)]
    stride = len(pieces) // groups

    def finish_group(g):
        sl = slice(g * LANES_V7X, (g + 1) * LANES_V7X)
        a = acc_scr[g]
        gi = g_prev + g
        is_norm = (gi < G_VA) | ((gi >= G_QM) & (gi < G_GM))
        is_silu = ((gi >= G_GA) & (gi < G_VB)) | ((gi >= G_GB) & (gi < G_QM)) | (gi >= G_GM)
        normed = _rms_rows(a) * gain_ref[:, sl]
        res = jnp.where(is_norm, normed, jnp.where(is_silu, a * _sigmoid(a), a))
        p_ref[0, g] = res.astype(BF16)
        return res

    def side_jobs():
        for src, dst in zip(cast_src, cast_dst):
            dst[...] = src[...].astype(BF16)
        per_step = IN_K_CHUNKS // n_j
        for i in range(per_step):
            h_ref[0, i] = h_scr[j_cur * per_step + i]

    @pl.when(f < n_steps)
    def _():
        accs = {}
        ties = {}
        for i, (n0, c) in enumerate(pieces):
            ks = slice(c * k_chunk, (c + 1) * k_chunk)
            lhs = h_scr[c]
            if i in ties:
                top = BF16_SUBLANES_V7X
                zero = _tied_zero(ties.pop(i), (top, k_chunk), BF16)
                lhs = jnp.concatenate([lhs[:top] + zero, lhs[top:]], axis=0)
            part = jnp.dot(lhs, w_ref[ks, n0:n0 + MXU_COLS_V7X], preferred_element_type=F32)
            accs[n0] = part if c == 0 else accs[n0] + part
            if i % stride == 0:
                ties[i + stride - 1] = _fold_rows(finish_group(i // stride))
        for n0, acc in accs.items():
            for gg in range(MXU_COLS_V7X // LANES_V7X):
                acc_scr[n0 // LANES_V7X + gg] = acc[:, gg * LANES_V7X:(gg + 1) * LANES_V7X]
        side_jobs()

    @pl.when(f == n_steps)
    def _():
        for g in range(groups):
            finish_group(g)
        side_jobs()


def _cast_blocks(rows, n_grid):
    blocks = 1
    while blocks * 2 <= n_grid and rows % (blocks * 2 * BF16_SUBLANES_V7X) == 0:
        blocks *= 2
    return blocks


def _in_proj(x, norm_gain, w_in_bf, gain_all, cast_weights):
    B, S, D = x.shape
    n_t = S // TM_IN
    n_j = w_in_bf.shape[1] // TN_IN
    n_steps = B * n_t * n_j
    groups = TN_IN // LANES_V7X
    vb_cols = (G_GB - G_VB) * LANES_V7X
    k_chunk = D // IN_K_CHUNKS

    def cur(f):
        f = jnp.minimum(f, n_steps - 1)
        return f // (n_t * n_j), (f // n_j) % n_t, f % n_j

    def prev(f):
        return cur(jnp.maximum(f - 1, 0))

    def cast_spec(w):
        blocks = _cast_blocks(w.shape[0], n_steps + 1)
        return pl.BlockSpec((w.shape[0] // blocks, w.shape[1]),
                            lambda f: (jnp.minimum(f, blocks - 1), 0))

    cast_specs = [cast_spec(w) for w in cast_weights]
    outs = pl.pallas_call(
        functools.partial(_in_proj_kernel, n_steps=n_steps, n_j=n_j, n_cast=len(cast_weights)),
        grid=(n_steps + 1,),
        in_specs=[
            pl.BlockSpec((1, TM_IN, D), lambda f: (cur(f)[0], cur(f)[1], 0)),
            pl.BlockSpec((1, D), lambda f: (0, 0)),
            pl.BlockSpec((D, TN_IN), lambda f: (0, cur(f)[2])),
            pl.BlockSpec((1, TN_IN), lambda f: (0, prev(f)[2])),
        ] + cast_specs,
        out_specs=[
            pl.BlockSpec((1, groups, TM_IN, LANES_V7X),
                         lambda f: (prev(f)[0], prev(f)[2], prev(f)[1], 0)),
            pl.BlockSpec((1, TM_IN, vb_cols), lambda f: (prev(f)[0], prev(f)[1], 0)),
            pl.BlockSpec((1, IN_K_CHUNKS // n_j, TM_IN, k_chunk),
                         lambda f: (cur(f)[0], cur(f)[2], cur(f)[1], 0)),
        ] + cast_specs,
        out_shape=[
            jax.ShapeDtypeStruct((B, N_GROUPS, S, LANES_V7X), BF16),
            jax.ShapeDtypeStruct((B, S, vb_cols), F32),
            jax.ShapeDtypeStruct((B, IN_K_CHUNKS, S, k_chunk), BF16),
        ] + [jax.ShapeDtypeStruct(w.shape, BF16) for w in cast_weights],
        scratch_shapes=[pltpu.VMEM((IN_K_CHUNKS, TM_IN, k_chunk), BF16),
                        pltpu.VMEM((groups, TM_IN, LANES_V7X), F32)],
        compiler_params=_params(("arbitrary",)),
        name="in_proj",
    )(x, norm_gain, w_in_bf, gain_all, *cast_weights)
    return outs[0], outs[1], outs[2], outs[3:]


def _mem_kv_kernel(mem_ref, mg_ref, w_ref, kg_ref, o_ref):
    mh = (_rms_rows(mem_ref[0]) * mg_ref[...]).astype(BF16)
    kv = jnp.dot(mh, w_ref[...], preferred_element_type=F32)
    for g in range(M_HEADS):
        sl = slice(g * HEAD_DIM, (g + 1) * HEAD_DIM)
        o_ref[0, g] = (_rms_rows(kv[:, sl]) * kg_ref[:, sl]).astype(BF16)
    for g in range(M_HEADS, 2 * M_HEADS):
        o_ref[0, g] = kv[:, g * HEAD_DIM:(g + 1) * HEAD_DIM].astype(BF16)


def _mem_kv(mem, mem_norm_gain, w_kv_bf, k_gain):
    B, N, D = mem.shape
    W = w_kv_bf.shape[1]
    return pl.pallas_call(
        _mem_kv_kernel,
        grid=(B,),
        in_specs=[
            pl.BlockSpec((1, N, D), lambda b: (b, 0, 0)),
            pl.BlockSpec((1, D), lambda b: (0, 0)),
            pl.BlockSpec((D, W), lambda b: (0, 0)),
            pl.BlockSpec((1, W // 2), lambda b: (0, 0)),
        ],
        out_specs=pl.BlockSpec((1, 2 * M_HEADS, N, HEAD_DIM), lambda b: (b, 0, 0, 0)),
        out_shape=jax.ShapeDtypeStruct((B, 2 * M_HEADS, N, HEAD_DIM), BF16),
        compiler_params=_params(("arbitrary",)),
        name="mem_kv",
    )(mem, mem_norm_gain, w_kv_bf, k_gain)


_BAND = N_KT * TQ
_RWRAP = _BAND + TQ


def _rel_bias_kernel(r_ref, o_ref):
    row = lax.broadcasted_iota(jnp.int32, (TQ, _RWRAP), 0)
    row_b = lax.broadcasted_iota(jnp.int32, (TQ, _BAND), 0)
    col_b = lax.broadcasted_iota(jnp.int32, (TQ, _BAND), 1)
    lo = jnp.where(row_b < CHUNK, 0, CHUNK)
    valid = (col_b >= lo) & (col_b < lo + _BAND - CHUNK)
    for h in range(A_HEADS):
        t = jnp.broadcast_to(r_ref[h:h + 1, :], (TQ, _RWRAP))
        for k in range(TQ.bit_length() - 1):
            t = jnp.where(((row >> k) & 1) == 1, pltpu.roll(t, 1 << k, 1), t)
        o_ref[h] = jnp.where(valid, t[:, :_BAND] * LOG2_E, NEG_INF)


def _rel_bias_tiles(rel_bias):
    H = rel_bias.shape[0]
    edge = jnp.broadcast_to(rel_bias[:, 2 * REL_CLIP:], (H, 2 * REL_CLIP))
    mid = jnp.flip(rel_bias[:, 2 * REL_CLIP + 1 - (_BAND - REL_CLIP):], axis=1)
    r_ext = jnp.concatenate([edge[:, :REL_CLIP], mid, edge[:, :_RWRAP - _BAND]], axis=1)
    return pl.pallas_call(
        _rel_bias_kernel,
        out_shape=jax.ShapeDtypeStruct((H, TQ, _BAND), F32),
        compiler_params=pltpu.CompilerParams(vmem_limit_bytes=VMEM_LIMIT_V7X),
        name="rel_bias",
    )(r_ext)


def _attention_pipeline(n_groups, tiles, scores_fn, values_fn, s_scr, p_scr, l_scr):
    rows, width = l_scr.shape[1:]

    def values(group):
        group = jnp.asarray(group, jnp.int32)
        for u in range(tiles):
            values_fn(group, u, p_scr[u], l_scr[u])

    def softmax():
        for u in range(tiles):
            s = s_scr[u]
            p = jnp.exp2(s - jnp.max(s, axis=-1, keepdims=True))
            l_scr[u] = jnp.broadcast_to(jnp.sum(p, axis=-1, keepdims=True), (rows, width))
            p_scr[u] = p.astype(BF16)

    def scores(group):
        group = jnp.asarray(group, jnp.int32)
        for u in range(tiles):
            s_scr[u] = scores_fn(group, u)

    def body(j, carry):
        values(j - 2)
        softmax()
        scores(j)
        return carry

    scores(0)
    softmax()
    scores(1)
    lax.fori_loop(2, n_groups, body, 0)
    values(n_groups - 2)
    softmax()
    values(n_groups - 1)


def _attention_scratch(tiles, rows, cols):
    return [pltpu.VMEM((tiles, rows, cols), F32), pltpu.VMEM((tiles, rows, cols), BF16),
            pltpu.VMEM((tiles, rows, HEAD_DIM), F32)]


_QK_DIMS = (((1,), (1,)), ((), ()))


def _band_attn_kernel(q_ref, k_ref, v_ref, g_ref, bias_ref, o_ref, *scratch):
    groups_per_head = q_ref.shape[2] // (TQ * ATTN_TILES_PER_STEP)
    col = lax.broadcasted_iota(jnp.int32, (TQ, _BAND), 1)

    def coords(group, u):
        head = group // groups_per_head
        qt = (group % groups_per_head) * ATTN_TILES_PER_STEP + u
        return head, qt, pl.multiple_of(qt * TQ, TQ)

    def band(ref, head, qt):
        tiles = []
        for a in range(N_KT):
            k0 = pl.multiple_of(jnp.maximum(qt - (N_KT - 1) + a, 0) * TQ, TQ)
            tiles.append(ref[0, head, pl.ds(k0, TQ), :])
        return jnp.concatenate(tiles, axis=0)

    def scores(group, u):
        head, qt, q0 = coords(group, u)
        s = lax.dot_general(q_ref[0, head, pl.ds(q0, TQ), :], band(k_ref, head, qt), _QK_DIMS,
                            preferred_element_type=F32)
        s = s + bias_ref[head]
        return jnp.where(col >= (N_KT - 1 - qt) * TQ, s, NEG_INF)

    def values(group, u, p, l):
        head, qt, q0 = coords(group, u)
        o = jnp.dot(p, band(v_ref, head, qt), preferred_element_type=F32) / l
        o_ref[0, head, pl.ds(q0, TQ), :] = (
            o * g_ref[0, head, pl.ds(q0, TQ), :].astype(F32)).astype(BF16)

    _attention_pipeline(ATTN_HEADS_PER_STEP * groups_per_head, ATTN_TILES_PER_STEP,
                        scores, values, *scratch)


def _band_attn(p, bias):
    B, _, S, _ = p.shape
    hb = ATTN_HEADS_PER_STEP
    blk = (1, hb, S, HEAD_DIM)

    def seg(g_first):
        return pl.BlockSpec(blk, lambda b, h: (b, g_first // hb + h, 0, 0))

    return pl.pallas_call(
        _band_attn_kernel,
        grid=(B, A_HEADS // hb),
        in_specs=[seg(G_QA), seg(G_KA), seg(G_VA), seg(G_GA),
                  pl.BlockSpec((hb, TQ, _BAND), lambda b, h: (h, 0, 0))],
        out_specs=seg(0),
        out_shape=jax.ShapeDtypeStruct((B, A_HEADS, S, HEAD_DIM), BF16),
        scratch_shapes=_attention_scratch(ATTN_TILES_PER_STEP, TQ, _BAND),
        compiler_params=_params(("parallel", "arbitrary")),
        name="band_attn",
    )(p, p, p, p, bias)


def _mem_attn_kernel(q_ref, g_ref, kv_ref, o_ref, *scratch):
    groups_per_head = q_ref.shape[2] // (TQ_MEM * MEM_TILES_PER_STEP)

    def coords(group, u):
        head = group // groups_per_head
        qt = (group % groups_per_head) * MEM_TILES_PER_STEP + u
        return head, pl.multiple_of(qt * TQ_MEM, TQ_MEM)

    def scores(group, u):
        head, q0 = coords(group, u)
        return lax.dot_general(q_ref[0, head, pl.ds(q0, TQ_MEM), :], kv_ref[0, head], _QK_DIMS,
                               preferred_element_type=F32)

    def values(group, u, p, l):
        head, q0 = coords(group, u)
        o = jnp.dot(p, kv_ref[0, M_HEADS + head], preferred_element_type=F32) / l
        o_ref[0, head, pl.ds(q0, TQ_MEM), :] = (
            o * g_ref[0, head, pl.ds(q0, TQ_MEM), :].astype(F32)).astype(BF16)

    _attention_pipeline(M_HEADS * groups_per_head, MEM_TILES_PER_STEP,
                        scores, values, *scratch)


def _mem_attn(p, memkv):
    B, _, S, _ = p.shape
    N = memkv.shape[2]
    blk = (1, M_HEADS, S, HEAD_DIM)
    return pl.pallas_call(
        _mem_attn_kernel,
        grid=(B,),
        in_specs=[
            pl.BlockSpec(blk, lambda b: (b, G_QM // M_HEADS, 0, 0)),
            pl.BlockSpec(blk, lambda b: (b, G_GM // M_HEADS, 0, 0)),
            pl.BlockSpec((1, 2 * M_HEADS, N, HEAD_DIM), lambda b: (b, 0, 0, 0)),
        ],
        out_specs=pl.BlockSpec(blk, lambda b: (b, 0, 0, 0)),
        out_shape=jax.ShapeDtypeStruct((B, M_HEADS, S, HEAD_DIM), BF16),
        scratch_shapes=_attention_scratch(MEM_TILES_PER_STEP, TQ_MEM, N),
        compiler_params=_params(("arbitrary",)),
        name="mem_attn",
    )(p, p, memkv)


def _pool_mix_kernel(vb_ref, prev_ref, g_ref, pw_ref, ps_ref, o_ref):
    t = pl.program_id(1)
    cur = vb_ref[0]
    prev = jnp.where(t > 0, prev_ref[0], 0.0)
    head_pos = lax.broadcasted_iota(jnp.int32, (POOL_HALO, 1), 0) + 1
    for g, w in enumerate(POOL_WINDOWS):
        assert w & (w - 1) == 0 and w <= POOL_HALO
        sl = slice(g * LANES_V7X, (g + 1) * LANES_V7X)
        acc = jnp.concatenate([prev[:, sl], cur[:, sl]], axis=0)
        d = 1
        while d < w:
            acc = acc + pltpu.roll(acc, d, 0)
            d *= 2
        sums = acc[POOL_HALO:, :]
        head_cnt = jnp.where(t == 0, jnp.minimum(head_pos, w), w).astype(F32)
        mean = jnp.concatenate([sums[:POOL_HALO] / head_cnt, sums[POOL_HALO:] * (1.0 / w)], axis=0)
        pooled = mean - cur[:, sl]
        mixed = jnp.dot(pooled.astype(BF16), pw_ref[g], preferred_element_type=F32)
        o_ref[0, g] = (mixed * ps_ref[:, sl] * g_ref[0, g].astype(F32)).astype(BF16)


def _pool_mix(vb, p, pool_w_bf, pool_scale):
    B, S, P = vb.shape
    G = len(POOL_WINDOWS)
    halo_blocks = TR_POOL // POOL_HALO
    return pl.pallas_call(
        _pool_mix_kernel,
        grid=(B, S // TR_POOL),
        in_specs=[
            pl.BlockSpec((1, TR_POOL, P), lambda b, t: (b, t, 0)),
            pl.BlockSpec((1, POOL_HALO, P), lambda b, t: (b, jnp.maximum(t * halo_blocks - 1, 0), 0)),
            pl.BlockSpec((1, G, TR_POOL, LANES_V7X), lambda b, t: (b, G_GB // G, t, 0)),
            pl.BlockSpec((G, LANES_V7X, LANES_V7X), lambda b, t: (0, 0, 0)),
            pl.BlockSpec((1, P), lambda b, t: (0, 0)),
        ],
        out_specs=pl.BlockSpec((1, G, TR_POOL, LANES_V7X), lambda b, t: (b, 0, t, 0)),
        out_shape=jax.ShapeDtypeStruct((B, G, S, LANES_V7X), BF16),
        compiler_params=_params(("parallel", "arbitrary")),
        name="pool_mix",
    )(vb, vb, p, pool_w_bf, pool_scale)


def _merge_out_kernel(x_ref, h_ref, oa_ref, ob_ref, om_ref,
                      wg0_ref, wg1_ref, wg2_ref, b0_ref, b1_ref, b2_ref,
                      wa_ref, wb_ref, wm_ref, wo_ref, out_ref):
    n = pl.program_id(2)

    @pl.when(n == 0)
    def _():
        out_ref[0] = x_ref[0]

    k_chunk = h_ref.shape[3]

    def branch(o_ref, w_ref, wg_ref, b_ref):
        o = jnp.concatenate([o_ref[0, g] for g in range(o_ref.shape[1])], axis=-1)
        pre = None
        for c in range(h_ref.shape[1]):
            part = jnp.dot(h_ref[0, c], wg_ref[c * k_chunk:(c + 1) * k_chunk, :],
                           preferred_element_type=F32)
            pre = part if pre is None else pre + part
        return _sigmoid(pre + b_ref[...]) * jnp.dot(o, w_ref[...], preferred_element_type=F32)

    y = (branch(oa_ref, wa_ref, wg0_ref, b0_ref)
         + branch(ob_ref, wb_ref, wg1_ref, b1_ref)
         + branch(om_ref, wm_ref, wg2_ref, b2_ref))
    out_ref[0] += jnp.dot(y.astype(BF16), wo_ref[...], preferred_element_type=F32)


def _merge_out(x, h, oga, ogb, ogm, w_merge_bf, b_merge, wa_bf, wb_bf, wm_bf, wo_bf):
    B, S, D = x.shape
    n_c = D // CN_OUT

    def gate_w(r):
        return pl.BlockSpec((D, CN_OUT), lambda b, t, n: (0, r * n_c + n))

    def gate_b(r):
        return pl.BlockSpec((1, CN_OUT), lambda b, t, n: (0, r * n_c + n))

    def heads(nh):
        return pl.BlockSpec((1, nh, TM_OUT, HEAD_DIM), lambda b, t, n: (b, 0, t, 0))

    def cols(rows):
        return pl.BlockSpec((rows, CN_OUT), lambda b, t, n: (0, n))

    rows = pl.BlockSpec((1, TM_OUT, D), lambda b, t, n: (b, t, 0))
    return pl.pallas_call(
        _merge_out_kernel,
        grid=(B, S // TM_OUT, n_c),
        in_specs=[
            rows, pl.BlockSpec((1, h.shape[1], TM_OUT, h.shape[3]), lambda b, t, n: (b, 0, t, 0)),
            heads(oga.shape[1]), heads(ogb.shape[1]), heads(ogm.shape[1]),
            gate_w(0), gate_w(1), gate_w(2), gate_b(0), gate_b(1), gate_b(2),
            cols(wa_bf.shape[0]), cols(wb_bf.shape[0]), cols(wm_bf.shape[0]),
            pl.BlockSpec((CN_OUT, D), lambda b, t, n: (n, 0)),
        ],
        out_specs=rows,
        out_shape=jax.ShapeDtypeStruct((B, S, D), F32),
        compiler_params=_params(("parallel", "arbitrary", "arbitrary")),
        name="merge_out",
    )(x, h, oga, ogb, ogm, w_merge_bf, w_merge_bf, w_merge_bf,
      b_merge, b_merge, b_merge, wa_bf, wb_bf, wm_bf, wo_bf)


def kernel(x, mem, norm_gain, mem_norm_gain, w_in, w_merge, b_merge, a_q_gain, a_k_gain,
           a_rel_bias, pool_w, pool_scale, w_mem_kv, m_q_gain, m_k_gain,
           w_branch_a, w_branch_b, w_branch_m, w_out):
    depth = w_in.shape[0]
    scale = HEAD_DIM ** -0.5 * LOG2_E
    for l in range(depth):
        a_w, p_w, m_w = w_branch_a.shape[1], w_branch_b.shape[1], w_branch_m.shape[1]
        assert w_in.shape[2] == N_GROUPS * LANES_V7X and a_w == (G_KA - G_QA) * LANES_V7X
        assert p_w == (G_GB - G_VB) * LANES_V7X and m_w == (G_GM - G_QM) * LANES_V7X
        gain_all = jnp.concatenate([
            a_q_gain[l].reshape(1, a_w) * scale, a_k_gain[l].reshape(1, a_w),
            jnp.ones((1, 2 * a_w + 2 * p_w), F32),
            m_q_gain[l].reshape(1, m_w) * scale, jnp.ones((1, m_w), F32)], axis=1)

        p, vb, h, (wg_bf, wo_bf, wa_bf, wb_bf, wm_bf, wkv_bf) = _in_proj(
            x, norm_gain[l:l + 1], w_in[l].astype(BF16), gain_all,
            [w_merge[l], w_out[l], w_branch_a[l], w_branch_b[l], w_branch_m[l], w_mem_kv[l]])
        memkv = _mem_kv(mem, mem_norm_gain[l:l + 1], wkv_bf, m_k_gain[l].reshape(1, m_w))
        bias = _rel_bias_tiles(a_rel_bias[l])
        oga = _band_attn(p, bias)
        ogm = _mem_attn(p, memkv)
        ogb = _pool_mix(vb, p, pool_w[l].astype(BF16), pool_scale[l:l + 1])
        x = _merge_out(x, h, oga, ogb, ogm, wg_bf, b_merge[l:l + 1], wa_bf, wb_bf, wm_bf, wo_bf)
    return x
```

```python
import functools

import jax
import jax.numpy as jnp
from jax import lax
from jax.experimental import pallas as pl
from jax.experimental.pallas import tpu as pltpu

F32 = jnp.float32
BF16 = jnp.bfloat16

CHUNK = 64
N_LEFT_CHUNKS = 8
A_HEADS = 8
M_HEADS = 4
HEAD_DIM = 128
REL_CLIP = 256
POOL_WINDOWS = (2, 4, 8, 16)
EPS = 1e-6
NEG_INF = -1e30
LOG2_E = 1.4426950408889634

LANES_V7X = 128
F32_SUBLANES_V7X = 8
BF16_SUBLANES_V7X = 16
MXU_COLS_V7X = 256
VMEM_LIMIT_V7X = 60000 * 1024

TM_IN = 1024
TN_IN = 1536
IN_K_CHUNKS = 8
TQ = 128
N_KT = N_LEFT_CHUNKS * CHUNK // TQ + 1
ATTN_TILES_PER_STEP = 8
ATTN_HEADS_PER_STEP = 4
TQ_MEM = 256
MEM_TILES_PER_STEP = 4
TR_POOL = 2048
POOL_HALO = 16
TM_OUT = 512
CN_OUT = 512

G_QA, G_KA, G_VA, G_GA, G_VB, G_GB, G_QM, G_GM = 0, 8, 16, 24, 32, 36, 40, 44
N_GROUPS = 48


def _params(sem):
    return pltpu.CompilerParams(dimension_semantics=sem, vmem_limit_bytes=VMEM_LIMIT_V7X)


def _rms_rows(t):
    return t * lax.rsqrt(jnp.mean(t * t, axis=-1, keepdims=True) + EPS)


def _sigmoid(z):
    return 1.0 / (1.0 + jnp.exp(-z))


def _fold_rows(src):
    sub = F32_SUBLANES_V7X
    return jnp.sum(src.reshape(src.shape[0] // sub, sub, src.shape[1]), axis=0)


def _tied_zero(folded, shape, dtype):
    bits = pltpu.bitcast(folded, jnp.uint32)
    zero = pltpu.bitcast((bits >> 16) >> 16, F32)
    return jnp.tile(zero, (shape[0] // F32_SUBLANES_V7X, shape[1] // folded.shape[1])).astype(dtype)


def _in_proj_kernel(*refs, n_steps, n_j, n_cast):
    x_ref, ng_ref, w_ref, gain_ref = refs[:4]
    cast_src = refs[4:4 + n_cast]
    p_ref, vb_ref, h_ref = refs[4 + n_cast:7 + n_cast]
    cast_dst = refs[7 + n_cast:7 + 2 * n_cast]
    h_scr, acc_scr = refs[7 + 2 * n_cast:]
    f = pl.program_id(0)
    groups = TN_IN // LANES_V7X
    k_chunk = h_scr.shape[2]
    j_cur = jnp.minimum(f, n_steps - 1) % n_j
    g_prev = (jnp.maximum(f - 1, 0) % n_j) * groups

    @pl.when(f == 0)
    def _():
        acc_scr[...] = jnp.zeros(acc_scr.shape, F32)

    @pl.when((j_cur == 0) & (f < n_steps))
    def _():
        h = (_rms_rows(x_ref[0]) * ng_ref[...]).astype(BF16)
        for c in range(IN_K_CHUNKS):
            h_scr[c] = h[:, c * k_chunk:(c + 1) * k_chunk]

    @pl.when(g_prev == G_VB // groups * groups)
    def _():
        vb0 = G_VB % groups
        for g in range(vb_ref.shape[2] // LANES_V7X):
            vb_ref[0, :, g * LANES_V7X:(g + 1) * LANES_V7X] = acc_scr[vb0 + g]

    pieces = [(n0, c) for n0 in range(0, TN_IN, MXU_COLS_V7X) for c in range(IN_K_CHUNKS)]
    stride = len(pieces) // groups

    def finish_group(g):
        sl = slice(g * LANES_V7X, (g + 1) * LANES_V7X)
        a = acc_scr[g]
        gi = g_prev + g
        is_norm = (gi < G_VA) | ((gi >= G_QM) & (gi < G_GM))
        is_silu = ((gi >= G_GA) & (gi < G_VB)) | ((gi >= G_GB) & (gi < G_QM)) | (gi >= G_GM)
        normed = _rms_rows(a) * gain_ref[:, sl]
        res = jnp.where(is_norm, normed, jnp.where(is_silu, a * _sigmoid(a), a))
        p_ref[0, g] = res.astype(BF16)
        return res

    def side_jobs():
        for src, dst in zip(cast_src, cast_dst):
            dst[...] = src[...].astype(BF16)
        per_step = IN_K_CHUNKS // n_j
        for i in range(per_step):
            h_ref[0, i] = h_scr[j_cur * per_step + i]

    @pl.when(f < n_steps)
    def _():
        accs = {}
        ties = {}
        for i, (n0, c) in enumerate(pieces):
            ks = slice(c * k_chunk, (c + 1) * k_chunk)
            lhs = h_scr[c]
            if i in ties:
                top = BF16_SUBLANES_V7X
                zero = _tied_zero(ties.pop(i), (top, k_chunk), BF16)
                lhs = jnp.concatenate([lhs[:top] + zero, lhs[top:]], axis=0)
            part = jnp.dot(lhs, w_ref[ks, n0:n0 + MXU_COLS_V7X], preferred_element_type=F32)
            accs[n0] = part if c == 0 else accs[n0] + part
            if i % stride == 0:
                ties[i + stride - 1] = _fold_rows(finish_group(i // stride))
        for n0, acc in accs.items():
            for gg in range(MXU_COLS_V7X // LANES_V7X):
                acc_scr[n0 // LANES_V7X + gg] = acc[:, gg * LANES_V7X:(gg + 1) * LANES_V7X]
        side_jobs()

    @pl.when(f == n_steps)
    def _():
        for g in range(groups):
            finish_group(g)
        side_jobs()


def _cast_blocks(rows, n_grid):
    blocks = 1
    while blocks * 2 <= n_grid and rows % (blocks * 2 * BF16_SUBLANES_V7X) == 0:
        blocks *= 2
    return blocks


def _in_proj(x, norm_gain, w_in_bf, gain_all, cast_weights):
    B, S, D = x.shape
    n_t = S // TM_IN
    n_j = w_in_bf.shape[1] // TN_IN
    n_steps = B * n_t * n_j
    groups = TN_IN // LANES_V7X
    vb_cols = (G_GB - G_VB) * LANES_V7X
    k_chunk = D // IN_K_CHUNKS

    def cur(f):
        f = jnp.minimum(f, n_steps - 1)
        return f // (n_t * n_j), (f // n_j) % n_t, f % n_j

    def prev(f):
        return cur(jnp.maximum(f - 1, 0))

    def cast_spec(w):
        blocks = _cast_blocks(w.shape[0], n_steps + 1)
        return pl.BlockSpec((w.shape[0] // blocks, w.shape[1]),
                            lambda f: (jnp.minimum(f, blocks - 1), 0))

    cast_specs = [cast_spec(w) for w in cast_weights]
    outs = pl.pallas_call(
        functools.partial(_in_proj_kernel, n_steps=n_steps, n_j=n_j, n_cast=len(cast_weights)),
        grid=(n_steps + 1,),
        in_specs=[
            pl.BlockSpec((1, TM_IN, D), lambda f: (cur(f)[0], cur(f)[1], 0)),
            pl.BlockSpec((1, D), lambda f: (0, 0)),
            pl.BlockSpec((D, TN_IN), lambda f: (0, cur(f)[2])),
            pl.BlockSpec((1, TN_IN), lambda f: (0, prev(f)[2])),
        ] + cast_specs,
        out_specs=[
            pl.BlockSpec((1, groups, TM_IN, LANES_V7X),
                         lambda f: (prev(f)[0], prev(f)[2], prev(f)[1], 0)),
            pl.BlockSpec((1, TM_IN, vb_cols), lambda f: (prev(f)[0], prev(f)[1], 0)),
            pl.BlockSpec((1, IN_K_CHUNKS // n_j, TM_IN, k_chunk),
                         lambda f: (cur(f)[0], cur(f)[2], cur(f)[1], 0)),
        ] + cast_specs,
        out_shape=[
            jax.ShapeDtypeStruct((B, N_GROUPS, S, LANES_V7X), BF16),
            jax.ShapeDtypeStruct((B, S, vb_cols), F32),
            jax.ShapeDtypeStruct((B, IN_K_CHUNKS, S, k_chunk), BF16),
        ] + [jax.ShapeDtypeStruct(w.shape, BF16) for w in cast_weights],
        scratch_shapes=[pltpu.VMEM((IN_K_CHUNKS, TM_IN, k_chunk), BF16),
                        pltpu.VMEM((groups, TM_IN, LANES_V7X), F32)],
        compiler_params=_params(("arbitrary",)),
        name="in_proj",
    )(x, norm_gain, w_in_bf, gain_all, *cast_weights)
    return outs[0], outs[1], outs[2], outs[3:]


def _mem_kv_kernel(mem_ref, mg_ref, w_ref, kg_ref, o_ref):
    mh = (_rms_rows(mem_ref[0]) * mg_ref[...]).astype(BF16)
    kv = jnp.dot(mh, w_ref[...], preferred_element_type=F32)
    for g in range(M_HEADS):
        sl = slice(g * HEAD_DIM, (g + 1) * HEAD_DIM)
        o_ref[0, g] = (_rms_rows(kv[:, sl]) * kg_ref[:, sl]).astype(BF16)
    for g in range(M_HEADS, 2 * M_HEADS):
        o_ref[0, g] = kv[:, g * HEAD_DIM:(g + 1) * HEAD_DIM].astype(BF16)


def _mem_kv(mem, mem_norm_gain, w_kv_bf, k_gain):
    B, N, D = mem.shape
    W = w_kv_bf.shape[1]
    return pl.pallas_call(
        _mem_kv_kernel,
        grid=(B,),
        in_specs=[
            pl.BlockSpec((1, N, D), lambda b: (b, 0, 0)),
            pl.BlockSpec((1, D), lambda b: (0, 0)),
            pl.BlockSpec((D, W), lambda b: (0, 0)),
            pl.BlockSpec((1, W // 2), lambda b: (0, 0)),
        ],
        out_specs=pl.BlockSpec((1, 2 * M_HEADS, N, HEAD_DIM), lambda b: (b, 0, 0, 0)),
        out_shape=jax.ShapeDtypeStruct((B, 2 * M_HEADS, N, HEAD_DIM), BF16),
        compiler_params=_params(("arbitrary",)),
        name="mem_kv",
    )(mem, mem_norm_gain, w_kv_bf, k_gain)


_BAND = N_KT * TQ
_RWRAP = _BAND + TQ


def _rel_bias_kernel(r_ref, o_ref):
    row = lax.broadcasted_iota(jnp.int32, (TQ, _RWRAP), 0)
    row_b = lax.broadcasted_iota(jnp.int32, (TQ, _BAND), 0)
    col_b = lax.broadcasted_iota(jnp.int32, (TQ, _BAND), 1)
    lo = jnp.where(row_b < CHUNK, 0, CHUNK)
    valid = (col_b >= lo) & (col_b < lo + _BAND - CHUNK)
    for h in range(A_HEADS):
        t = jnp.broadcast_to(r_ref[h:h + 1, :], (TQ, _RWRAP))
        for k in range(TQ.bit_length() - 1):
            t = jnp.where(((row >> k) & 1) == 1, pltpu.roll(t, 1 << k, 1), t)
        o_ref[h] = jnp.where(valid, t[:, :_BAND] * LOG2_E, NEG_INF)


def _rel_bias_tiles(rel_bias):
    H = rel_bias.shape[0]
    edge = jnp.broadcast_to(rel_bias[:, 2 * REL_CLIP:], (H, 2 * REL_CLIP))
    mid = jnp.flip(rel_bias[:, 2 * REL_CLIP + 1 - (_BAND - REL_CLIP):], axis=1)
    r_ext = jnp.concatenate([edge[:, :REL_CLIP], mid, edge[:, :_RWRAP - _BAND]], axis=1)
    return pl.pallas_call(
        _rel_bias_kernel,
        out_shape=jax.ShapeDtypeStruct((H, TQ, _BAND), F32),
        compiler_params=pltpu.CompilerParams(vmem_limit_bytes=VMEM_LIMIT_V7X),
        name="rel_bias",
    )(r_ext)


def _attention_pipeline(n_groups, tiles, scores_fn, values_fn, s_scr, p_scr, l_scr):
    rows, width = l_scr.shape[1:]

    def values(group):
        group = jnp.asarray(group, jnp.int32)
        for u in range(tiles):
            values_fn(group, u, p_scr[u], l_scr[u])

    def softmax():
        for u in range(tiles):
            s = s_scr[u]
            p = jnp.exp2(s - jnp.max(s, axis=-1, keepdims=True))
            l_scr[u] = jnp.broadcast_to(jnp.sum(p, axis=-1, keepdims=True), (rows, width))
            p_scr[u] = p.astype(BF16)

    def scores(group):
        group = jnp.asarray(group, jnp.int32)
        for u in range(tiles):
            s_scr[u] = scores_fn(group, u)

    def body(j, carry):
        values(j - 2)
        softmax()
        scores(j)
        return carry

    scores(0)
    softmax()
    scores(1)
    lax.fori_loop(2, n_groups, body, 0)
    values(n_groups - 2)
    softmax()
    values(n_groups - 1)


def _attention_scratch(tiles, rows, cols):
    return [pltpu.VMEM((tiles, rows, cols), F32), pltpu.VMEM((tiles, rows, cols), BF16),
            pltpu.VMEM((tiles, rows, HEAD_DIM), F32)]


_QK_DIMS = (((1,), (1,)), ((), ()))


def _band_attn_kernel(q_ref, k_ref, v_ref, g_ref, bias_ref, o_ref, *scratch):
    groups_per_head = q_ref.shape[2] // (TQ * ATTN_TILES_PER_STEP)
    col = lax.broadcasted_iota(jnp.int32, (TQ, _BAND), 1)

    def coords(group, u):
        head = group // groups_per_head
        qt = (group % groups_per_head) * ATTN_TILES_PER_STEP + u
        return head, qt, pl.multiple_of(qt * TQ, TQ)

    def band(ref, head, qt):
        tiles = []
        for a in range(N_KT):
            k0 = pl.multiple_of(jnp.maximum(qt - (N_KT - 1) + a, 0) * TQ, TQ)
            tiles.append(ref[0, head, pl.ds(k0, TQ), :])
        return jnp.concatenate(tiles, axis=0)

    def scores(group, u):
        head, qt, q0 = coords(group, u)
        s = lax.dot_general(q_ref[0, head, pl.ds(q0, TQ), :], band(k_ref, head, qt), _QK_DIMS,
                            preferred_element_type=F32)
        s = s + bias_ref[head]
        return jnp.where(col >= (N_KT - 1 - qt) * TQ, s, NEG_INF)

    def values(group, u, p, l):
        head, qt, q0 = coords(group, u)
        o = jnp.dot(p, band(v_ref, head, qt), preferred_element_type=F32) / l
        o_ref[0, head, pl.ds(q0, TQ), :] = (
            o * g_ref[0, head, pl.ds(q0, TQ), :].astype(F32)).astype(BF16)

    _attention_pipeline(ATTN_HEADS_PER_STEP * groups_per_head, ATTN_TILES_PER_STEP,
                        scores, values, *scratch)


def _band_attn(p, bias):
    B, _, S, _ = p.shape
    hb = ATTN_HEADS_PER_STEP
    blk = (1, hb, S, HEAD_DIM)

    def seg(g_first):
        return pl.BlockSpec(blk, lambda b, h: (b, g_first // hb + h, 0, 0))

    return pl.pallas_call(
        _band_attn_kernel,
        grid=(B, A_HEADS // hb),
        in_specs=[seg(G_QA), seg(G_KA), seg(G_VA), seg(G_GA),
                  pl.BlockSpec((hb, TQ, _BAND), lambda b, h: (h, 0, 0))],
        out_specs=seg(0),
        out_shape=jax.ShapeDtypeStruct((B, A_HEADS, S, HEAD_DIM), BF16),
        scratch_shapes=_attention_scratch(ATTN_TILES_PER_STEP, TQ, _BAND),
        compiler_params=_params(("parallel", "arbitrary")),
        name="band_attn",
    )(p, p, p, p, bias)


def _mem_attn_kernel(q_ref, g_ref, kv_ref, o_ref, *scratch):
    groups_per_head = q_ref.shape[2] // (TQ_MEM * MEM_TILES_PER_STEP)

    def coords(group, u):
        head = group // groups_per_head
        qt = (group % groups_per_head) * MEM_TILES_PER_STEP + u
        return head, pl.multiple_of(qt * TQ_MEM, TQ_MEM)

    def scores(group, u):
        head, q0 = coords(group, u)
        return lax.dot_general(q_ref[0, head, pl.ds(q0, TQ_MEM), :], kv_ref[0, head], _QK_DIMS,
                               preferred_element_type=F32)

    def values(group, u, p, l):
        head, q0 = coords(group, u)
        o = jnp.dot(p, kv_ref[0, M_HEADS + head], preferred_element_type=F32) / l
        o_ref[0, head, pl.ds(q0, TQ_MEM), :] = (
            o * g_ref[0, head, pl.ds(q0, TQ_MEM), :].astype(F32)).astype(BF16)

    _attention_pipeline(M_HEADS * groups_per_head, MEM_TILES_PER_STEP,
                        scores, values, *scratch)


def _mem_attn(p, memkv):
    B, _, S, _ = p.shape
    N = memkv.shape[2]
    blk = (1, M_HEADS, S, HEAD_DIM)
    return pl.pallas_call(
        _mem_attn_kernel,
        grid=(B,),
        in_specs=[
            pl.BlockSpec(blk, lambda b: (b, G_QM // M_HEADS, 0, 0)),
            pl.BlockSpec(blk, lambda b: (b, G_GM // M_HEADS, 0, 0)),
            pl.BlockSpec((1, 2 * M_HEADS, N, HEAD_DIM), lambda b: (b, 0, 0, 0)),
        ],
        out_specs=pl.BlockSpec(blk, lambda b: (b, 0, 0, 0)),
        out_shape=jax.ShapeDtypeStruct((B, M_HEADS, S, HEAD_DIM), BF16),
        scratch_shapes=_attention_scratch(MEM_TILES_PER_STEP, TQ_MEM, N),
        compiler_params=_params(("arbitrary",)),
        name="mem_attn",
    )(p, p, memkv)


def _pool_mix_kernel(vb_ref, prev_ref, g_ref, pw_ref, ps_ref, o_ref):
    t = pl.program_id(1)
    cur = vb_ref[0]
    prev = jnp.where(t > 0, prev_ref[0], 0.0)
    head_pos = lax.broadcasted_iota(jnp.int32, (POOL_HALO, 1), 0) + 1
    for g, w in enumerate(POOL_WINDOWS):
        assert w & (w - 1) == 0 and w <= POOL_HALO
        sl = slice(g * LANES_V7X, (g + 1) * LANES_V7X)
        acc = jnp.concatenate([prev[:, sl], cur[:, sl]], axis=0)
        d = 1
        while d < w:
            acc = acc + pltpu.roll(acc, d, 0)
            d *= 2
        sums = acc[POOL_HALO:, :]
        head_cnt = jnp.where(t == 0, jnp.minimum(head_pos, w), w).astype(F32)
        mean = jnp.concatenate([sums[:POOL_HALO] / head_cnt, sums[POOL_HALO:] * (1.0 / w)], axis=0)
        pooled = mean - cur[:, sl]
        mixed = jnp.dot(pooled.astype(BF16), pw_ref[g], preferred_element_type=F32)
        o_ref[0, g] = (mixed * ps_ref[:, sl] * g_ref[0, g].astype(F32)).astype(BF16)


def _pool_mix(vb, p, pool_w_bf, pool_scale):
    B, S, P = vb.shape
    G = len(POOL_WINDOWS)
    halo_blocks = TR_POOL // POOL_HALO
    return pl.pallas_call(
        _pool_mix_kernel,
        grid=(B, S // TR_POOL),
        in_specs=[
            pl.BlockSpec((1, TR_POOL, P), lambda b, t: (b, t, 0)),
            pl.BlockSpec((1, POOL_HALO, P), lambda b, t: (b, jnp.maximum(t * halo_blocks - 1, 0), 0)),
            pl.BlockSpec((1, G, TR_POOL, LANES_V7X), lambda b, t: (b, G_GB // G, t, 0)),
            pl.BlockSpec((G, LANES_V7X, LANES_V7X), lambda b, t: (0, 0, 0)),
            pl.BlockSpec((1, P), lambda b, t: (0, 0)),
        ],
        out_specs=pl.BlockSpec((1, G, TR_POOL, LANES_V7X), lambda b, t: (b, 0, t, 0)),
        out_shape=jax.ShapeDtypeStruct((B, G, S, LANES_V7X), BF16),
        compiler_params=_params(("parallel", "arbitrary")),
        name="pool_mix",
    )(vb, vb, p, pool_w_bf, pool_scale)


def _merge_out_kernel(x_ref, h_ref, oa_ref, ob_ref, om_ref,
                      wg0_ref, wg1_ref, wg2_ref, b0_ref, b1_ref, b2_ref,
                      wa_ref, wb_ref, wm_ref, wo_ref, out_ref):
    n = pl.program_id(2)

    @pl.when(n == 0)
    def _():
        out_ref[0] = x_ref[0]

    k_chunk = h_ref.shape[3]

    def branch(o_ref, w_ref, wg_ref, b_ref):
        o = jnp.concatenate([o_ref[0, g] for g in range(o_ref.shape[1])], axis=-1)
        pre = None
        for c in range(h_ref.shape[1]):
            part = jnp.dot(h_ref[0, c], wg_ref[c * k_chunk:(c + 1) * k_chunk, :],
                           preferred_element_type=F32)
            pre = part if pre is None else pre + part
        return _sigmoid(pre + b_ref[...]) * jnp.dot(o, w_ref[...], preferred_element_type=F32)

    y = (branch(oa_ref, wa_ref, wg0_ref, b0_ref)
         + branch(ob_ref, wb_ref, wg1_ref, b1_ref)
         + branch(om_ref, wm_ref, wg2_ref, b2_ref))
    out_ref[0] += jnp.dot(y.astype(BF16), wo_ref[...], preferred_element_type=F32)


def _merge_out(x, h, oga, ogb, ogm, w_merge_bf, b_merge, wa_bf, wb_bf, wm_bf, wo_bf):
    B, S, D = x.shape
    n_c = D // CN_OUT

    def gate_w(r):
        return pl.BlockSpec((D, CN_OUT), lambda b, t, n: (0, r * n_c + n))

    def gate_b(r):
        return pl.BlockSpec((1, CN_OUT), lambda b, t, n: (0, r * n_c + n))

    def heads(nh):
        return pl.BlockSpec((1, nh, TM_OUT, HEAD_DIM), lambda b, t, n: (b, 0, t, 0))

    def cols(rows):
        return pl.BlockSpec((rows, CN_OUT), lambda b, t, n: (0, n))

    rows = pl.BlockSpec((1, TM_OUT, D), lambda b, t, n: (b, t, 0))
    return pl.pallas_call(
        _merge_out_kernel,
        grid=(B, S // TM_OUT, n_c),
        in_specs=[
            rows, pl.BlockSpec((1, h.shape[1], TM_OUT, h.shape[3]), lambda b, t, n: (b, 0, t, 0)),
            heads(oga.shape[1]), heads(ogb.shape[1]), heads(ogm.shape[1]),
            gate_w(0), gate_w(1), gate_w(2), gate_b(0), gate_b(1), gate_b(2),
            cols(wa_bf.shape[0]), cols(wb_bf.shape[0]), cols(wm_bf.shape[0]),
            pl.BlockSpec((CN_OUT, D), lambda b, t, n: (n, 0)),
        ],
        out_specs=rows,
        out_shape=jax.ShapeDtypeStruct((B, S, D), F32),
        compiler_params=_params(("parallel", "arbitrary", "arbitrary")),
        name="merge_out",
    )(x, h, oga, ogb, ogm, w_merge_bf, w_merge_bf, w_merge_bf,
      b_merge, b_merge, b_merge, wa_bf, wb_bf, wm_bf, wo_bf)


def kernel(x, mem, norm_gain, mem_norm_gain, w_in, w_merge, b_merge, a_q_gain, a_k_gain,
           a_rel_bias, pool_w, pool_scale, w_mem_kv, m_q_gain, m_k_gain,
           w_branch_a, w_branch_b, w_branch_m, w_out):
    depth = w_in.shape[0]
    scale = HEAD_DIM ** -0.5 * LOG2_E
    for l in range(depth):
        a_w, p_w, m_w = w_branch_a.shape[1], w_branch_b.shape[1], w_branch_m.shape[1]
        assert w_in.shape[2] == N_GROUPS * LANES_V7X and a_w == (G_KA - G_QA) * LANES_V7X
        assert p_w == (G_GB - G_VB) * LANES_V7X and m_w == (G_GM - G_QM) * LANES_V7X
        gain_all = jnp.concatenate([
            a_q_gain[l].reshape(1, a_w) * scale, a_k_gain[l].reshape(1, a_w),
            jnp.ones((1, 2 * a_w + 2 * p_w), F32),
            m_q_gain[l].reshape(1, m_w) * scale, jnp.ones((1, m_w), F32)], axis=1)

        p, vb, h, (wg_bf, wo_bf, wa_bf, wb_bf, wm_bf, wkv_bf) = _in_proj(
            x, norm_gain[l:l + 1], w_in[l].astype(BF16), gain_all,
            [w_merge[l], w_out[l], w_branch_a[l], w_branch_b[l], w_branch_m[l], w_mem_kv[l]])
        memkv = _mem_kv(mem, mem_norm_gain[l:l + 1], wkv_bf, m_k_gain[l].reshape(1, m_w))
        bias = _rel_bias_tiles(a_rel_bias[l])
        oga = _band_attn(p, bias)
        ogm = _mem_attn(p, memkv)
        ogb = _pool_mix(vb, p, pool_w[l].astype(BF16), pool_scale[l:l + 1])
        x = _merge_out(x, h, oga, ogb, ogm, wg_bf, b_merge[l:l + 1], wa_bf, wb_bf, wm_bf, wo_bf)
    return x
```

```python
import functools

import jax
import jax.numpy as jnp
from jax import lax
from jax.experimental import pallas as pl
from jax.experimental.pallas import tpu as pltpu

F32 = jnp.float32
BF16 = jnp.bfloat16

CHUNK = 64
N_LEFT_CHUNKS = 8
A_HEADS = 8
M_HEADS = 4
HEAD_DIM = 128
REL_CLIP = 256
POOL_WINDOWS = (2, 4, 8, 16)
EPS = 1e-6
NEG_INF = -1e30
LOG2_E = 1.4426950408889634

LANES_V7X = 128
F32_SUBLANES_V7X = 8
BF16_SUBLANES_V7X = 16
MXU_COLS_V7X = 256
VMEM_LIMIT_V7X = 60000 * 1024

TM_IN = 1024
TN_IN = 1536
IN_K_CHUNKS = 8
TQ = 128
N_KT = N_LEFT_CHUNKS * CHUNK // TQ + 1
ATTN_TILES_PER_STEP = 16
ATTN_HEADS_PER_STEP = 4
TQ_MEM = 256
MEM_TILES_PER_STEP = 2
TR_POOL = 2048
POOL_HALO = 16
TM_OUT = 512
CN_OUT = 512

G_QA, G_KA, G_VA, G_GA, G_VB, G_GB, G_QM, G_GM = 0, 8, 16, 24, 32, 36, 40, 44
N_GROUPS = 48


def _params(sem):
    return pltpu.CompilerParams(dimension_semantics=sem, vmem_limit_bytes=VMEM_LIMIT_V7X)


def _rms_rows(t):
    return t * lax.rsqrt(jnp.mean(t * t, axis=-1, keepdims=True) + EPS)


def _sigmoid(z):
    return 1.0 / (1.0 + jnp.exp(-z))


def _fold_rows(src):
    sub = F32_SUBLANES_V7X
    return jnp.sum(src.reshape(src.shape[0] // sub, sub, src.shape[1]), axis=0)


def _tied_zero(folded, shape, dtype):
    bits = pltpu.bitcast(folded, jnp.uint32)
    zero = pltpu.bitcast((bits >> 16) >> 16, F32)
    return jnp.tile(zero, (shape[0] // F32_SUBLANES_V7X, shape[1] // folded.shape[1])).astype(dtype)


def _in_proj_kernel(*refs, n_steps, n_j, n_cast):
    x_ref, ng_ref, w_ref, gain_ref = refs[:4]
    cast_src = refs[4:4 + n_cast]
    p_ref, vb_ref, h_ref = refs[4 + n_cast:7 + n_cast]
    cast_dst = refs[7 + n_cast:7 + 2 * n_cast]
    h_scr, acc_scr = refs[7 + 2 * n_cast:]
    f = pl.program_id(0)
    groups = TN_IN // LANES_V7X
    k_chunk = h_scr.shape[2]
    j_cur = jnp.minimum(f, n_steps - 1) % n_j
    g_prev = (jnp.maximum(f - 1, 0) % n_j) * groups

    @pl.when(f == 0)
    def _():
        acc_scr[...] = jnp.zeros(acc_scr.shape, F32)

    @pl.when((j_cur == 0) & (f < n_steps))
    def _():
        h = (_rms_rows(x_ref[0]) * ng_ref[...]).astype(BF16)
        for c in range(IN_K_CHUNKS):
            h_scr[c] = h[:, c * k_chunk:(c + 1) * k_chunk]

    @pl.when(g_prev == G_VB // groups * groups)
    def _():
        vb0 = G_VB % groups
        for g in range(vb_ref.shape[2] // LANES_V7X):
            vb_ref[0, :, g * LANES_V7X:(g + 1) * LANES_V7X] = acc_scr[vb0 + g]

    pieces = [(n0, c) for n0 in range(0, TN_IN, MXU_COLS_V7X) for c in range(IN_K_CHUNKS)]
    stride = len(pieces) // groups

    def finish_group(g):
        sl = slice(g * LANES_V7X, (g + 1) * LANES_V7X)
        a = acc_scr[g]
        gi = g_prev + g
        is_norm = (gi < G_VA) | ((gi >= G_QM) & (gi < G_GM))
        is_silu = ((gi >= G_GA) & (gi < G_VB)) | ((gi >= G_GB) & (gi < G_QM)) | (gi >= G_GM)
        normed = _rms_rows(a) * gain_ref[:, sl]
        res = jnp.where(is_norm, normed, jnp.where(is_silu, a * _sigmoid(a), a))
        p_ref[0, g] = res.astype(BF16)
        return res

    def side_jobs():
        for src, dst in zip(cast_src, cast_dst):
            dst[...] = src[...].astype(BF16)
        per_step = IN_K_CHUNKS // n_j
        for i in range(per_step):
            h_ref[0, i] = h_scr[j_cur * per_step + i]

    @pl.when(f < n_steps)
    def _():
        accs = {}
        ties = {}
        for i, (n0, c) in enumerate(pieces):
            ks = slice(c * k_chunk, (c + 1) * k_chunk)
            lhs = h_scr[c]
            if i in ties:
                top = BF16_SUBLANES_V7X
                zero = _tied_zero(ties.pop(i), (top, k_chunk), BF16)
                lhs = jnp.concatenate([lhs[:top] + zero, lhs[top:]], axis=0)
            part = jnp.dot(lhs, w_ref[ks, n0:n0 + MXU_COLS_V7X], preferred_element_type=F32)
            accs[n0] = part if c == 0 else accs[n0] + part
            if i % stride == 0:
                ties[i + stride - 1] = _fold_rows(finish_group(i // stride))
        for n0, acc in accs.items():
            for gg in range(MXU_COLS_V7X // LANES_V7X):
                acc_scr[n0 // LANES_V7X + gg] = acc[:, gg * LANES_V7X:(gg + 1) * LANES_V7X]
        side_jobs()

    @pl.when(f == n_steps)
    def _():
        for g in range(groups):
            finish_group(g)
        side_jobs()


def _cast_blocks(rows, n_grid):
    blocks = 1
    while blocks * 2 <= n_grid and rows % (blocks * 2 * BF16_SUBLANES_V7X) == 0:
        blocks *= 2
    return blocks


def _in_proj(x, norm_gain, w_in_bf, gain_all, cast_weights):
    B, S, D = x.shape
    n_t = S // TM_IN
    n_j = w_in_bf.shape[1] // TN_IN
    n_steps = B * n_t * n_j
    groups = TN_IN // LANES_V7X
    vb_cols = (G_GB - G_VB) * LANES_V7X
    k_chunk = D // IN_K_CHUNKS

    def cur(f):
        f = jnp.minimum(f, n_steps - 1)
        return f // (n_t * n_j), (f // n_j) % n_t, f % n_j

    def prev(f):
        return cur(jnp.maximum(f - 1, 0))

    def cast_spec(w):
        blocks = _cast_blocks(w.shape[0], n_steps + 1)
        return pl.BlockSpec((w.shape[0] // blocks, w.shape[1]),
                            lambda f: (jnp.minimum(f, blocks - 1), 0))

    cast_specs = [cast_spec(w) for w in cast_weights]
    outs = pl.pallas_call(
        functools.partial(_in_proj_kernel, n_steps=n_steps, n_j=n_j, n_cast=len(cast_weights)),
        grid=(n_steps + 1,),
        in_specs=[
            pl.BlockSpec((1, TM_IN, D), lambda f: (cur(f)[0], cur(f)[1], 0)),
            pl.BlockSpec((1, D), lambda f: (0, 0)),
            pl.BlockSpec((D, TN_IN), lambda f: (0, cur(f)[2])),
            pl.BlockSpec((1, TN_IN), lambda f: (0, prev(f)[2])),
        ] + cast_specs,
        out_specs=[
            pl.BlockSpec((1, groups, TM_IN, LANES_V7X),
                         lambda f: (prev(f)[0], prev(f)[2], prev(f)[1], 0)),
            pl.BlockSpec((1, TM_IN, vb_cols), lambda f: (prev(f)[0], prev(f)[1], 0)),
            pl.BlockSpec((1, IN_K_CHUNKS // n_j, TM_IN, k_chunk),
                         lambda f: (cur(f)[0], cur(f)[2], cur(f)[1], 0)),
        ] + cast_specs,
        out_shape=[
            jax.ShapeDtypeStruct((B, N_GROUPS, S, LANES_V7X), BF16),
            jax.ShapeDtypeStruct((B, S, vb_cols), F32),
            jax.ShapeDtypeStruct((B, IN_K_CHUNKS, S, k_chunk), BF16),
        ] + [jax.ShapeDtypeStruct(w.shape, BF16) for w in cast_weights],
        scratch_shapes=[pltpu.VMEM((IN_K_CHUNKS, TM_IN, k_chunk), BF16),
                        pltpu.VMEM((groups, TM_IN, LANES_V7X), F32)],
        compiler_params=_params(("arbitrary",)),
        name="in_proj",
    )(x, norm_gain, w_in_bf, gain_all, *cast_weights)
    return outs[0], outs[1], outs[2], outs[3:]


def _mem_kv_kernel(mem_ref, mg_ref, w_ref, kg_ref, o_ref):
    mh = (_rms_rows(mem_ref[0]) * mg_ref[...]).astype(BF16)
    kv = jnp.dot(mh, w_ref[...], preferred_element_type=F32)
    for g in range(M_HEADS):
        sl = slice(g * HEAD_DIM, (g + 1) * HEAD_DIM)
        o_ref[0, g] = (_rms_rows(kv[:, sl]) * kg_ref[:, sl]).astype(BF16)
    for g in range(M_HEADS, 2 * M_HEADS):
        o_ref[0, g] = kv[:, g * HEAD_DIM:(g + 1) * HEAD_DIM].astype(BF16)


def _mem_kv(mem, mem_norm_gain, w_kv_bf, k_gain):
    B, N, D = mem.shape
    W = w_kv_bf.shape[1]
    return pl.pallas_call(
        _mem_kv_kernel,
        grid=(B,),
        in_specs=[
            pl.BlockSpec((1, N, D), lambda b: (b, 0, 0)),
            pl.BlockSpec((1, D), lambda b: (0, 0)),
            pl.BlockSpec((D, W), lambda b: (0, 0)),
            pl.BlockSpec((1, W // 2), lambda b: (0, 0)),
        ],
        out_specs=pl.BlockSpec((1, 2 * M_HEADS, N, HEAD_DIM), lambda b: (b, 0, 0, 0)),
        out_shape=jax.ShapeDtypeStruct((B, 2 * M_HEADS, N, HEAD_DIM), BF16),
        compiler_params=_params(("arbitrary",)),
        name="mem_kv",
    )(mem, mem_norm_gain, w_kv_bf, k_gain)


_BAND = N_KT * TQ
_RWRAP = _BAND + TQ


def _rel_bias_kernel(r_ref, o_ref):
    row = lax.broadcasted_iota(jnp.int32, (TQ, _RWRAP), 0)
    row_b = lax.broadcasted_iota(jnp.int32, (TQ, _BAND), 0)
    col_b = lax.broadcasted_iota(jnp.int32, (TQ, _BAND), 1)
    lo = jnp.where(row_b < CHUNK, 0, CHUNK)
    valid = (col_b >= lo) & (col_b < lo + _BAND - CHUNK)
    for h in range(A_HEADS):
        t = jnp.broadcast_to(r_ref[h:h + 1, :], (TQ, _RWRAP))
        for k in range(TQ.bit_length() - 1):
            t = jnp.where(((row >> k) & 1) == 1, pltpu.roll(t, 1 << k, 1), t)
        o_ref[h] = jnp.where(valid, t[:, :_BAND] * LOG2_E, NEG_INF)


def _rel_bias_tiles(rel_bias):
    H = rel_bias.shape[0]
    edge = jnp.broadcast_to(rel_bias[:, 2 * REL_CLIP:], (H, 2 * REL_CLIP))
    mid = jnp.flip(rel_bias[:, 2 * REL_CLIP + 1 - (_BAND - REL_CLIP):], axis=1)
    r_ext = jnp.concatenate([edge[:, :REL_CLIP], mid, edge[:, :_RWRAP - _BAND]], axis=1)
    return pl.pallas_call(
        _rel_bias_kernel,
        out_shape=jax.ShapeDtypeStruct((H, TQ, _BAND), F32),
        compiler_params=pltpu.CompilerParams(vmem_limit_bytes=VMEM_LIMIT_V7X),
        name="rel_bias",
    )(r_ext)


def _attention_pipeline(n_groups, tiles, scores_fn, values_fn, s_scr, p_scr, l_scr):
    rows, width = l_scr.shape[1:]

    def values(group):
        group = jnp.asarray(group, jnp.int32)
        for u in range(tiles):
            values_fn(group, u, p_scr[u], l_scr[u])

    def softmax():
        for u in range(tiles):
            s = s_scr[u]
            p = jnp.exp2(s - jnp.max(s, axis=-1, keepdims=True))
            l_scr[u] = jnp.broadcast_to(jnp.sum(p, axis=-1, keepdims=True), (rows, width))
            p_scr[u] = p.astype(BF16)

    def scores(group):
        group = jnp.asarray(group, jnp.int32)
        for u in range(tiles):
            s_scr[u] = scores_fn(group, u)

    def body(j, carry):
        values(j - 2)
        softmax()
        scores(j)
        return carry

    scores(0)
    softmax()
    scores(1)
    lax.fori_loop(2, n_groups, body, 0)
    values(n_groups - 2)
    softmax()
    values(n_groups - 1)


def _attention_scratch(tiles, rows, cols):
    return [pltpu.VMEM((tiles, rows, cols), F32), pltpu.VMEM((tiles, rows, cols), BF16),
            pltpu.VMEM((tiles, rows, HEAD_DIM), F32)]


_QK_DIMS = (((1,), (1,)), ((), ()))


def _band_attn_kernel(q_ref, k_ref, v_ref, g_ref, bias_ref, o_ref, *scratch):
    groups_per_head = q_ref.shape[2] // (TQ * ATTN_TILES_PER_STEP)
    col = lax.broadcasted_iota(jnp.int32, (TQ, _BAND), 1)

    def coords(group, u):
        head = group // groups_per_head
        qt = (group % groups_per_head) * ATTN_TILES_PER_STEP + u
        return head, qt, pl.multiple_of(qt * TQ, TQ)

    def band(ref, head, qt):
        tiles = []
        for a in range(N_KT):
            k0 = pl.multiple_of(jnp.maximum(qt - (N_KT - 1) + a, 0) * TQ, TQ)
            tiles.append(ref[0, head, pl.ds(k0, TQ), :])
        return jnp.concatenate(tiles, axis=0)

    def scores(group, u):
        head, qt, q0 = coords(group, u)
        s = lax.dot_general(q_ref[0, head, pl.ds(q0, TQ), :], band(k_ref, head, qt), _QK_DIMS,
                            preferred_element_type=F32)
        s = s + bias_ref[head]
        return jnp.where(col >= (N_KT - 1 - qt) * TQ, s, NEG_INF)

    def values(group, u, p, l):
        head, qt, q0 = coords(group, u)
        o = jnp.dot(p, band(v_ref, head, qt), preferred_element_type=F32) / l
        o_ref[0, head, pl.ds(q0, TQ), :] = (
            o * g_ref[0, head, pl.ds(q0, TQ), :].astype(F32)).astype(BF16)

    _attention_pipeline(ATTN_HEADS_PER_STEP * groups_per_head, ATTN_TILES_PER_STEP,
                        scores, values, *scratch)


def _band_attn(p, bias):
    B, _, S, _ = p.shape
    hb = ATTN_HEADS_PER_STEP
    blk = (1, hb, S, HEAD_DIM)

    def seg(g_first):
        return pl.BlockSpec(blk, lambda b, h: (b, g_first // hb + h, 0, 0))

    return pl.pallas_call(
        _band_attn_kernel,
        grid=(B, A_HEADS // hb),
        in_specs=[seg(G_QA), seg(G_KA), seg(G_VA), seg(G_GA),
                  pl.BlockSpec((hb, TQ, _BAND), lambda b, h: (h, 0, 0))],
        out_specs=seg(0),
        out_shape=jax.ShapeDtypeStruct((B, A_HEADS, S, HEAD_DIM), BF16),
        scratch_shapes=_attention_scratch(ATTN_TILES_PER_STEP, TQ, _BAND),
        compiler_params=_params(("parallel", "arbitrary")),
        name="band_attn",
    )(p, p, p, p, bias)


def _mem_attn_kernel(q_ref, g_ref, kv_ref, o_ref, *scratch):
    groups_per_head = q_ref.shape[2] // (TQ_MEM * MEM_TILES_PER_STEP)

    def coords(group, u):
        head = group // groups_per_head
        qt = (group % groups_per_head) * MEM_TILES_PER_STEP + u
        return head, pl.multiple_of(qt * TQ_MEM, TQ_MEM)

    def scores(group, u):
        head, q0 = coords(group, u)
        return lax.dot_general(q_ref[0, head, pl.ds(q0, TQ_MEM), :], kv_ref[0, head], _QK_DIMS,
                               preferred_element_type=F32)

    def values(group, u, p, l):
        head, q0 = coords(group, u)
        o = jnp.dot(p, kv_ref[0, M_HEADS + head], preferred_element_type=F32) / l
        o_ref[0, head, pl.ds(q0, TQ_MEM), :] = (
            o * g_ref[0, head, pl.ds(q0, TQ_MEM), :].astype(F32)).astype(BF16)

    _attention_pipeline(M_HEADS * groups_per_head, MEM_TILES_PER_STEP,
                        scores, values, *scratch)


def _mem_attn(p, memkv):
    B, _, S, _ = p.shape
    N = memkv.shape[2]
    blk = (1, M_HEADS, S, HEAD_DIM)
    return pl.pallas_call(
        _mem_attn_kernel,
        grid=(B,),
        in_specs=[
            pl.BlockSpec(blk, lambda b: (b, G_QM // M_HEADS, 0, 0)),
            pl.BlockSpec(blk, lambda b: (b, G_GM // M_HEADS, 0, 0)),
            pl.BlockSpec((1, 2 * M_HEADS, N, HEAD_DIM), lambda b: (b, 0, 0, 0)),
        ],
        out_specs=pl.BlockSpec(blk, lambda b: (b, 0, 0, 0)),
        out_shape=jax.ShapeDtypeStruct((B, M_HEADS, S, HEAD_DIM), BF16),
        scratch_shapes=_attention_scratch(MEM_TILES_PER_STEP, TQ_MEM, N),
        compiler_params=_params(("arbitrary",)),
        name="mem_attn",
    )(p, p, memkv)


def _pool_mix_kernel(vb_ref, prev_ref, g_ref, pw_ref, ps_ref, o_ref):
    t = pl.program_id(1)
    cur = vb_ref[0]
    prev = jnp.where(t > 0, prev_ref[0], 0.0)
    head_pos = lax.broadcasted_iota(jnp.int32, (POOL_HALO, 1), 0) + 1
    for g, w in enumerate(POOL_WINDOWS):
        assert w & (w - 1) == 0 and w <= POOL_HALO
        sl = slice(g * LANES_V7X, (g + 1) * LANES_V7X)
        acc = jnp.concatenate([prev[:, sl], cur[:, sl]], axis=0)
        d = 1
        while d < w:
            acc = acc + pltpu.roll(acc, d, 0)
            d *= 2
        sums = acc[POOL_HALO:, :]
        head_cnt = jnp.where(t == 0, jnp.minimum(head_pos, w), w).astype(F32)
        mean = jnp.concatenate([sums[:POOL_HALO] / head_cnt, sums[POOL_HALO:] * (1.0 / w)], axis=0)
        pooled = mean - cur[:, sl]
        mixed = jnp.dot(pooled.astype(BF16), pw_ref[g], preferred_element_type=F32)
        o_ref[0, g] = (mixed * ps_ref[:, sl] * g_ref[0, g].astype(F32)).astype(BF16)


def _pool_mix(vb, p, pool_w_bf, pool_scale):
    B, S, P = vb.shape
    G = len(POOL_WINDOWS)
    halo_blocks = TR_POOL // POOL_HALO
    return pl.pallas_call(
        _pool_mix_kernel,
        grid=(B, S // TR_POOL),
        in_specs=[
            pl.BlockSpec((1, TR_POOL, P), lambda b, t: (b, t, 0)),
            pl.BlockSpec((1, POOL_HALO, P), lambda b, t: (b, jnp.maximum(t * halo_blocks - 1, 0), 0)),
            pl.BlockSpec((1, G, TR_POOL, LANES_V7X), lambda b, t: (b, G_GB // G, t, 0)),
            pl.BlockSpec((G, LANES_V7X, LANES_V7X), lambda b, t: (0, 0, 0)),
            pl.BlockSpec((1, P), lambda b, t: (0, 0)),
        ],
        out_specs=pl.BlockSpec((1, G, TR_POOL, LANES_V7X), lambda b, t: (b, 0, t, 0)),
        out_shape=jax.ShapeDtypeStruct((B, G, S, LANES_V7X), BF16),
        compiler_params=_params(("parallel", "arbitrary")),
        name="pool_mix",
    )(vb, vb, p, pool_w_bf, pool_scale)


def _merge_out_kernel(x_ref, h_ref, oa_ref, ob_ref, om_ref,
                      wg0_ref, wg1_ref, wg2_ref, b0_ref, b1_ref, b2_ref,
                      wa_ref, wb_ref, wm_ref, wo_ref, out_ref):
    n = pl.program_id(2)

    @pl.when(n == 0)
    def _():
        out_ref[0] = x_ref[0]

    k_chunk = h_ref.shape[3]

    def branch(o_ref, w_ref, wg_ref, b_ref):
        o = jnp.concatenate([o_ref[0, g] for g in range(o_ref.shape[1])], axis=-1)
        pre = None
        for c in range(h_ref.shape[1]):
            part = jnp.dot(h_ref[0, c], wg_ref[c * k_chunk:(c + 1) * k_chunk, :],
                           preferred_element_type=F32)
            pre = part if pre is None else pre + part
        return _sigmoid(pre + b_ref[...]) * jnp.dot(o, w_ref[...], preferred_element_type=F32)

    y = (branch(oa_ref, wa_ref, wg0_ref, b0_ref)
         + branch(ob_ref, wb_ref, wg1_ref, b1_ref)
         + branch(om_ref, wm_ref, wg2_ref, b2_ref))
    out_ref[0] += jnp.dot(y.astype(BF16), wo_ref[...], preferred_element_type=F32)


def _merge_out(x, h, oga, ogb, ogm, w_merge_bf, b_merge, wa_bf, wb_bf, wm_bf, wo_bf):
    B, S, D = x.shape
    n_c = D // CN_OUT

    def gate_w(r):
        return pl.BlockSpec((D, CN_OUT), lambda b, t, n: (0, r * n_c + n))

    def gate_b(r):
        return pl.BlockSpec((1, CN_OUT), lambda b, t, n: (0, r * n_c + n))

    def heads(nh):
        return pl.BlockSpec((1, nh, TM_OUT, HEAD_DIM), lambda b, t, n: (b, 0, t, 0))

    def cols(rows):
        return pl.BlockSpec((rows, CN_OUT), lambda b, t, n: (0, n))

    rows = pl.BlockSpec((1, TM_OUT, D), lambda b, t, n: (b, t, 0))
    return pl.pallas_call(
        _merge_out_kernel,
        grid=(B, S // TM_OUT, n_c),
        in_specs=[
            rows, pl.BlockSpec((1, h.shape[1], TM_OUT, h.shape[3]), lambda b, t, n: (b, 0, t, 0)),
            heads(oga.shape[1]), heads(ogb.shape[1]), heads(ogm.shape[1]),
            gate_w(0), gate_w(1), gate_w(2), gate_b(0), gate_b(1), gate_b(2),
            cols(wa_bf.shape[0]), cols(wb_bf.shape[0]), cols(wm_bf.shape[0]),
            pl.BlockSpec((CN_OUT, D), lambda b, t, n: (n, 0)),
        ],
        out_specs=rows,
        out_shape=jax.ShapeDtypeStruct((B, S, D), F32),
        compiler_params=_params(("parallel", "arbitrary", "arbitrary")),
        name="merge_out",
    )(x, h, oga, ogb, ogm, w_merge_bf, w_merge_bf, w_merge_bf,
      b_merge, b_merge, b_merge, wa_bf, wb_bf, wm_bf, wo_bf)


def kernel(x, mem, norm_gain, mem_norm_gain, w_in, w_merge, b_merge, a_q_gain, a_k_gain,
           a_rel_bias, pool_w, pool_scale, w_mem_kv, m_q_gain, m_k_gain,
           w_branch_a, w_branch_b, w_branch_m, w_out):
    depth = w_in.shape[0]
    scale = HEAD_DIM ** -0.5 * LOG2_E
    for l in range(depth):
        a_w, p_w, m_w = w_branch_a.shape[1], w_branch_b.shape[1], w_branch_m.shape[1]
        assert w_in.shape[2] == N_GROUPS * LANES_V7X and a_w == (G_KA - G_QA) * LANES_V7X
        assert p_w == (G_GB - G_VB) * LANES_V7X and m_w == (G_GM - G_QM) * LANES_V7X
        gain_all = jnp.concatenate([
            a_q_gain[l].reshape(1, a_w) * scale, a_k_gain[l].reshape(1, a_w),
            jnp.ones((1, 2 * a_w + 2 * p_w), F32),
            m_q_gain[l].reshape(1, m_w) * scale, jnp.ones((1, m_w), F32)], axis=1)

        p, vb, h, (wg_bf, wo_bf, wa_bf, wb_bf, wm_bf, wkv_bf) = _in_proj(
            x, norm_gain[l:l + 1], w_in[l].astype(BF16), gain_all,
            [w_merge[l], w_out[l], w_branch_a[l], w_branch_b[l], w_branch_m[l], w_mem_kv[l]])
        memkv = _mem_kv(mem, mem_norm_gain[l:l + 1], wkv_bf, m_k_gain[l].reshape(1, m_w))
        bias = _rel_bias_tiles(a_rel_bias[l])
        oga = _band_attn(p, bias)
        ogm = _mem_attn(p, memkv)
        ogb = _pool_mix(vb, p, pool_w[l].astype(BF16), pool_scale[l:l + 1])
        x = _merge_out(x, h, oga, ogb, ogm, wg_bf, b_merge[l:l + 1], wa_bf, wb_bf, wm_bf, wo_bf)
    return x
```

```python
import functools

import jax
import jax.numpy as jnp
from jax import lax
from jax.experimental import pallas as pl
from jax.experimental.pallas import tpu as pltpu

F32 = jnp.float32
BF16 = jnp.bfloat16

CHUNK = 64
N_LEFT_CHUNKS = 8
A_HEADS = 8
M_HEADS = 4
HEAD_DIM = 128
REL_CLIP = 256
POOL_WINDOWS = (2, 4, 8, 16)
EPS = 1e-6
NEG_INF = -1e30
LOG2_E = 1.4426950408889634

LANES_V7X = 128
F32_SUBLANES_V7X = 8
BF16_SUBLANES_V7X = 16
MXU_COLS_V7X = 256
VMEM_LIMIT_V7X = 60000 * 1024

TM_IN = 1024
TN_IN = 1536
IN_K_CHUNKS = 8
TQ = 128
N_KT = N_LEFT_CHUNKS * CHUNK // TQ + 1
ATTN_TILES_PER_STEP = 16
ATTN_HEADS_PER_STEP = 4
TQ_MEM = 256
MEM_TILES_PER_STEP = 2
TR_POOL = 4096
POOL_HALO = 16
TM_OUT = 512
CN_OUT = 512

G_QA, G_KA, G_VA, G_GA, G_VB, G_GB, G_QM, G_GM = 0, 8, 16, 24, 32, 36, 40, 44
N_GROUPS = 48


def _params(sem):
    return pltpu.CompilerParams(dimension_semantics=sem, vmem_limit_bytes=VMEM_LIMIT_V7X)


def _rms_rows(t):
    return t * lax.rsqrt(jnp.mean(t * t, axis=-1, keepdims=True) + EPS)


def _sigmoid(z):
    return 1.0 / (1.0 + jnp.exp(-z))


def _fold_rows(src):
    sub = F32_SUBLANES_V7X
    return jnp.sum(src.reshape(src.shape[0] // sub, sub, src.shape[1]), axis=0)


def _tied_zero(folded, shape, dtype):
    bits = pltpu.bitcast(folded, jnp.uint32)
    zero = pltpu.bitcast((bits >> 16) >> 16, F32)
    return jnp.tile(zero, (shape[0] // F32_SUBLANES_V7X, shape[1] // folded.shape[1])).astype(dtype)


def _in_proj_kernel(*refs, n_steps, n_j, n_cast):
    x_ref, ng_ref, w_ref, gain_ref = refs[:4]
    cast_src = refs[4:4 + n_cast]
    p_ref, vb_ref, h_ref = refs[4 + n_cast:7 + n_cast]
    cast_dst = refs[7 + n_cast:7 + 2 * n_cast]
    h_scr, acc_scr = refs[7 + 2 * n_cast:]
    f = pl.program_id(0)
    groups = TN_IN // LANES_V7X
    k_chunk = h_scr.shape[2]
    j_cur = jnp.minimum(f, n_steps - 1) % n_j
    g_prev = (jnp.maximum(f - 1, 0) % n_j) * groups

    @pl.when(f == 0)
    def _():
        acc_scr[...] = jnp.zeros(acc_scr.shape, F32)

    @pl.when((j_cur == 0) & (f < n_steps))
    def _():
        h = (_rms_rows(x_ref[0]) * ng_ref[...]).astype(BF16)
        for c in range(IN_K_CHUNKS):
            h_scr[c] = h[:, c * k_chunk:(c + 1) * k_chunk]

    @pl.when(g_prev == G_VB // groups * groups)
    def _():
        vb0 = G_VB % groups
        for g in range(vb_ref.shape[2] // LANES_V7X):
            vb_ref[0, :, g * LANES_V7X:(g + 1) * LANES_V7X] = acc_scr[vb0 + g]

    pieces = [(n0, c) for n0 in range(0, TN_IN, MXU_COLS_V7X) for c in range(IN_K_CHUNKS)]
    stride = len(pieces) // groups

    def finish_group(g):
        sl = slice(g * LANES_V7X, (g + 1) * LANES_V7X)
        a = acc_scr[g]
        gi = g_prev + g
        is_norm = (gi < G_VA) | ((gi >= G_QM) & (gi < G_GM))
        is_silu = ((gi >= G_GA) & (gi < G_VB)) | ((gi >= G_GB) & (gi < G_QM)) | (gi >= G_GM)
        normed = _rms_rows(a) * gain_ref[:, sl]
        res = jnp.where(is_norm, normed, jnp.where(is_silu, a * _sigmoid(a), a))
        p_ref[0, g] = res.astype(BF16)
        return res

    def side_jobs():
        for src, dst in zip(cast_src, cast_dst):
            dst[...] = src[...].astype(BF16)
        per_step = IN_K_CHUNKS // n_j
        for i in range(per_step):
            h_ref[0, i] = h_scr[j_cur * per_step + i]

    @pl.when(f < n_steps)
    def _():
        accs = {}
        ties = {}
        for i, (n0, c) in enumerate(pieces):
            ks = slice(c * k_chunk, (c + 1) * k_chunk)
            lhs = h_scr[c]
            if i in ties:
                top = BF16_SUBLANES_V7X
                zero = _tied_zero(ties.pop(i), (top, k_chunk), BF16)
                lhs = jnp.concatenate([lhs[:top] + zero, lhs[top:]], axis=0)
            part = jnp.dot(lhs, w_ref[ks, n0:n0 + MXU_COLS_V7X], preferred_element_type=F32)
            accs[n0] = part if c == 0 else accs[n0] + part
            if i % stride == 0:
                ties[i + stride - 1] = _fold_rows(finish_group(i // stride))
        for n0, acc in accs.items():
            for gg in range(MXU_COLS_V7X // LANES_V7X):
                acc_scr[n0 // LANES_V7X + gg] = acc[:, gg * LANES_V7X:(gg + 1) * LANES_V7X]
        side_jobs()

    @pl.when(f == n_steps)
    def _():
        for g in range(groups):
            finish_group(g)
        side_jobs()


def _cast_blocks(rows, n_grid):
    blocks = 1
    while blocks * 2 <= n_grid and rows % (blocks * 2 * BF16_SUBLANES_V7X) == 0:
        blocks *= 2
    return blocks


def _in_proj(x, norm_gain, w_in_bf, gain_all, cast_weights):
    B, S, D = x.shape
    n_t = S // TM_IN
    n_j = w_in_bf.shape[1] // TN_IN
    n_steps = B * n_t * n_j
    groups = TN_IN // LANES_V7X
    vb_cols = (G_GB - G_VB) * LANES_V7X
    k_chunk = D // IN_K_CHUNKS

    def cur(f):
        f = jnp.minimum(f, n_steps - 1)
        return f // (n_t * n_j), (f // n_j) % n_t, f % n_j

    def prev(f):
        return cur(jnp.maximum(f - 1, 0))

    def cast_spec(w):
        blocks = _cast_blocks(w.shape[0], n_steps + 1)
        return pl.BlockSpec((w.shape[0] // blocks, w.shape[1]),
                            lambda f: (jnp.minimum(f, blocks - 1), 0))

    cast_specs = [cast_spec(w) for w in cast_weights]
    outs = pl.pallas_call(
        functools.partial(_in_proj_kernel, n_steps=n_steps, n_j=n_j, n_cast=len(cast_weights)),
        grid=(n_steps + 1,),
        in_specs=[
            pl.BlockSpec((1, TM_IN, D), lambda f: (cur(f)[0], cur(f)[1], 0)),
            pl.BlockSpec((1, D), lambda f: (0, 0)),
            pl.BlockSpec((D, TN_IN), lambda f: (0, cur(f)[2])),
            pl.BlockSpec((1, TN_IN), lambda f: (0, prev(f)[2])),
        ] + cast_specs,
        out_specs=[
            pl.BlockSpec((1, groups, TM_IN, LANES_V7X),
                         lambda f: (prev(f)[0], prev(f)[2], prev(f)[1], 0)),
            pl.BlockSpec((1, TM_IN, vb_cols), lambda f: (prev(f)[0], prev(f)[1], 0)),
            pl.BlockSpec((1, IN_K_CHUNKS // n_j, TM_IN, k_chunk),
                         lambda f: (cur(f)[0], cur(f)[2], cur(f)[1], 0)),
        ] + cast_specs,
        out_shape=[
            jax.ShapeDtypeStruct((B, N_GROUPS, S, LANES_V7X), BF16),
            jax.ShapeDtypeStruct((B, S, vb_cols), F32),
            jax.ShapeDtypeStruct((B, IN_K_CHUNKS, S, k_chunk), BF16),
        ] + [jax.ShapeDtypeStruct(w.shape, BF16) for w in cast_weights],
        scratch_shapes=[pltpu.VMEM((IN_K_CHUNKS, TM_IN, k_chunk), BF16),
                        pltpu.VMEM((groups, TM_IN, LANES_V7X), F32)],
        compiler_params=_params(("arbitrary",)),
        name="in_proj",
    )(x, norm_gain, w_in_bf, gain_all, *cast_weights)
    return outs[0], outs[1], outs[2], outs[3:]


def _mem_kv_kernel(mem_ref, mg_ref, w_ref, kg_ref, o_ref):
    mh = (_rms_rows(mem_ref[0]) * mg_ref[...]).astype(BF16)
    kv = jnp.dot(mh, w_ref[...], preferred_element_type=F32)
    for g in range(M_HEADS):
        sl = slice(g * HEAD_DIM, (g + 1) * HEAD_DIM)
        o_ref[0, g] = (_rms_rows(kv[:, sl]) * kg_ref[:, sl]).astype(BF16)
    for g in range(M_HEADS, 2 * M_HEADS):
        o_ref[0, g] = kv[:, g * HEAD_DIM:(g + 1) * HEAD_DIM].astype(BF16)


def _mem_kv(mem, mem_norm_gain, w_kv_bf, k_gain):
    B, N, D = mem.shape
    W = w_kv_bf.shape[1]
    return pl.pallas_call(
        _mem_kv_kernel,
        grid=(B,),
        in_specs=[
            pl.BlockSpec((1, N, D), lambda b: (b, 0, 0)),
            pl.BlockSpec((1, D), lambda b: (0, 0)),
            pl.BlockSpec((D, W), lambda b: (0, 0)),
            pl.BlockSpec((1, W // 2), lambda b: (0, 0)),
        ],
        out_specs=pl.BlockSpec((1, 2 * M_HEADS, N, HEAD_DIM), lambda b: (b, 0, 0, 0)),
        out_shape=jax.ShapeDtypeStruct((B, 2 * M_HEADS, N, HEAD_DIM), BF16),
        compiler_params=_params(("arbitrary",)),
        name="mem_kv",
    )(mem, mem_norm_gain, w_kv_bf, k_gain)


_BAND = N_KT * TQ
_RWRAP = _BAND + TQ


def _rel_bias_kernel(r_ref, o_ref):
    row = lax.broadcasted_iota(jnp.int32, (TQ, _RWRAP), 0)
    row_b = lax.broadcasted_iota(jnp.int32, (TQ, _BAND), 0)
    col_b = lax.broadcasted_iota(jnp.int32, (TQ, _BAND), 1)
    lo = jnp.where(row_b < CHUNK, 0, CHUNK)
    valid = (col_b >= lo) & (col_b < lo + _BAND - CHUNK)
    for h in range(A_HEADS):
        t = jnp.broadcast_to(r_ref[h:h + 1, :], (TQ, _RWRAP))
        for k in range(TQ.bit_length() - 1):
            t = jnp.where(((row >> k) & 1) == 1, pltpu.roll(t, 1 << k, 1), t)
        o_ref[h] = jnp.where(valid, t[:, :_BAND] * LOG2_E, NEG_INF)


def _rel_bias_tiles(rel_bias):
    H = rel_bias.shape[0]
    edge = jnp.broadcast_to(rel_bias[:, 2 * REL_CLIP:], (H, 2 * REL_CLIP))
    mid = jnp.flip(rel_bias[:, 2 * REL_CLIP + 1 - (_BAND - REL_CLIP):], axis=1)
    r_ext = jnp.concatenate([edge[:, :REL_CLIP], mid, edge[:, :_RWRAP - _BAND]], axis=1)
    return pl.pallas_call(
        _rel_bias_kernel,
        out_shape=jax.ShapeDtypeStruct((H, TQ, _BAND), F32),
        compiler_params=pltpu.CompilerParams(vmem_limit_bytes=VMEM_LIMIT_V7X),
        name="rel_bias",
    )(r_ext)


def _attention_pipeline(n_groups, tiles, scores_fn, values_fn, s_scr, p_scr, l_scr):
    rows, width = l_scr.shape[1:]

    def values(group):
        group = jnp.asarray(group, jnp.int32)
        for u in range(tiles):
            values_fn(group, u, p_scr[u], l_scr[u])

    def softmax():
        for u in range(tiles):
            s = s_scr[u]
            p = jnp.exp2(s - jnp.max(s, axis=-1, keepdims=True))
            l_scr[u] = jnp.broadcast_to(jnp.sum(p, axis=-1, keepdims=True), (rows, width))
            p_scr[u] = p.astype(BF16)

    def scores(group):
        group = jnp.asarray(group, jnp.int32)
        for u in range(tiles):
            s_scr[u] = scores_fn(group, u)

    def body(j, carry):
        values(j - 2)
        softmax()
        scores(j)
        return carry

    scores(0)
    softmax()
    scores(1)
    lax.fori_loop(2, n_groups, body, 0)
    values(n_groups - 2)
    softmax()
    values(n_groups - 1)


def _attention_scratch(tiles, rows, cols):
    return [pltpu.VMEM((tiles, rows, cols), F32), pltpu.VMEM((tiles, rows, cols), BF16),
            pltpu.VMEM((tiles, rows, HEAD_DIM), F32)]


_QK_DIMS = (((1,), (1,)), ((), ()))


def _band_attn_kernel(q_ref, k_ref, v_ref, g_ref, bias_ref, o_ref, *scratch):
    groups_per_head = q_ref.shape[2] // (TQ * ATTN_TILES_PER_STEP)
    col = lax.broadcasted_iota(jnp.int32, (TQ, _BAND), 1)

    def coords(group, u):
        head = group // groups_per_head
        qt = (group % groups_per_head) * ATTN_TILES_PER_STEP + u
        return head, qt, pl.multiple_of(qt * TQ, TQ)

    def band(ref, head, qt):
        tiles = []
        for a in range(N_KT):
            k0 = pl.multiple_of(jnp.maximum(qt - (N_KT - 1) + a, 0) * TQ, TQ)
            tiles.append(ref[0, head, pl.ds(k0, TQ), :])
        return jnp.concatenate(tiles, axis=0)

    def scores(group, u):
        head, qt, q0 = coords(group, u)
        s = lax.dot_general(q_ref[0, head, pl.ds(q0, TQ), :], band(k_ref, head, qt), _QK_DIMS,
                            preferred_element_type=F32)
        s = s + bias_ref[head]
        return jnp.where(col >= (N_KT - 1 - qt) * TQ, s, NEG_INF)

    def values(group, u, p, l):
        head, qt, q0 = coords(group, u)
        o = jnp.dot(p, band(v_ref, head, qt), preferred_element_type=F32) / l
        o_ref[0, head, pl.ds(q0, TQ), :] = (
            o * g_ref[0, head, pl.ds(q0, TQ), :].astype(F32)).astype(BF16)

    _attention_pipeline(ATTN_HEADS_PER_STEP * groups_per_head, ATTN_TILES_PER_STEP,
                        scores, values, *scratch)


def _band_attn(p, bias):
    B, _, S, _ = p.shape
    hb = ATTN_HEADS_PER_STEP
    blk = (1, hb, S, HEAD_DIM)

    def seg(g_first):
        return pl.BlockSpec(blk, lambda b, h: (b, g_first // hb + h, 0, 0))

    return pl.pallas_call(
        _band_attn_kernel,
        grid=(B, A_HEADS // hb),
        in_specs=[seg(G_QA), seg(G_KA), seg(G_VA), seg(G_GA),
                  pl.BlockSpec((hb, TQ, _BAND), lambda b, h: (h, 0, 0))],
        out_specs=seg(0),
        out_shape=jax.ShapeDtypeStruct((B, A_HEADS, S, HEAD_DIM), BF16),
        scratch_shapes=_attention_scratch(ATTN_TILES_PER_STEP, TQ, _BAND),
        compiler_params=_params(("parallel", "arbitrary")),
        name="band_attn",
    )(p, p, p, p, bias)


def _mem_attn_kernel(q_ref, g_ref, kv_ref, o_ref, *scratch):
    groups_per_head = q_ref.shape[2] // (TQ_MEM * MEM_TILES_PER_STEP)

    def coords(group, u):
        head = group // groups_per_head
        qt = (group % groups_per_head) * MEM_TILES_PER_STEP + u
        return head, pl.multiple_of(qt * TQ_MEM, TQ_MEM)

    def scores(group, u):
        head, q0 = coords(group, u)
        return lax.dot_general(q_ref[0, head, pl.ds(q0, TQ_MEM), :], kv_ref[0, head], _QK_DIMS,
                               preferred_element_type=F32)

    def values(group, u, p, l):
        head, q0 = coords(group, u)
        o = jnp.dot(p, kv_ref[0, M_HEADS + head], preferred_element_type=F32) / l
        o_ref[0, head, pl.ds(q0, TQ_MEM), :] = (
            o * g_ref[0, head, pl.ds(q0, TQ_MEM), :].astype(F32)).astype(BF16)

    _attention_pipeline(M_HEADS * groups_per_head, MEM_TILES_PER_STEP,
                        scores, values, *scratch)


def _mem_attn(p, memkv):
    B, _, S, _ = p.shape
    N = memkv.shape[2]
    blk = (1, M_HEADS, S, HEAD_DIM)
    return pl.pallas_call(
        _mem_attn_kernel,
        grid=(B,),
        in_specs=[
            pl.BlockSpec(blk, lambda b: (b, G_QM // M_HEADS, 0, 0)),
            pl.BlockSpec(blk, lambda b: (b, G_GM // M_HEADS, 0, 0)),
            pl.BlockSpec((1, 2 * M_HEADS, N, HEAD_DIM), lambda b: (b, 0, 0, 0)),
        ],
        out_specs=pl.BlockSpec(blk, lambda b: (b, 0, 0, 0)),
        out_shape=jax.ShapeDtypeStruct((B, M_HEADS, S, HEAD_DIM), BF16),
        scratch_shapes=_attention_scratch(MEM_TILES_PER_STEP, TQ_MEM, N),
        compiler_params=_params(("arbitrary",)),
        name="mem_attn",
    )(p, p, memkv)


def _pool_mix_kernel(vb_ref, prev_ref, g_ref, pw_ref, ps_ref, o_ref):
    t = pl.program_id(1)
    cur = vb_ref[0]
    prev = jnp.where(t > 0, prev_ref[0], 0.0)
    head_pos = lax.broadcasted_iota(jnp.int32, (POOL_HALO, 1), 0) + 1
    for g, w in enumerate(POOL_WINDOWS):
        assert w & (w - 1) == 0 and w <= POOL_HALO
        sl = slice(g * LANES_V7X, (g + 1) * LANES_V7X)
        acc = jnp.concatenate([prev[:, sl], cur[:, sl]], axis=0)
        d = 1
        while d < w:
            acc = acc + pltpu.roll(acc, d, 0)
            d *= 2
        sums = acc[POOL_HALO:, :]
        head_cnt = jnp.where(t == 0, jnp.minimum(head_pos, w), w).astype(F32)
        mean = jnp.concatenate([sums[:POOL_HALO] / head_cnt, sums[POOL_HALO:] * (1.0 / w)], axis=0)
        pooled = mean - cur[:, sl]
        mixed = jnp.dot(pooled.astype(BF16), pw_ref[g], preferred_element_type=F32)
        o_ref[0, g] = (mixed * ps_ref[:, sl] * g_ref[0, g].astype(F32)).astype(BF16)


def _pool_mix(vb, p, pool_w_bf, pool_scale):
    B, S, P = vb.shape
    G = len(POOL_WINDOWS)
    halo_blocks = TR_POOL // POOL_HALO
    return pl.pallas_call(
        _pool_mix_kernel,
        grid=(B, S // TR_POOL),
        in_specs=[
            pl.BlockSpec((1, TR_POOL, P), lambda b, t: (b, t, 0)),
            pl.BlockSpec((1, POOL_HALO, P), lambda b, t: (b, jnp.maximum(t * halo_blocks - 1, 0), 0)),
            pl.BlockSpec((1, G, TR_POOL, LANES_V7X), lambda b, t: (b, G_GB // G, t, 0)),
            pl.BlockSpec((G, LANES_V7X, LANES_V7X), lambda b, t: (0, 0, 0)),
            pl.BlockSpec((1, P), lambda b, t: (0, 0)),
        ],
        out_specs=pl.BlockSpec((1, G, TR_POOL, LANES_V7X), lambda b, t: (b, 0, t, 0)),
        out_shape=jax.ShapeDtypeStruct((B, G, S, LANES_V7X), BF16),
        compiler_params=_params(("parallel", "arbitrary")),
        name="pool_mix",
    )(vb, vb, p, pool_w_bf, pool_scale)


def _merge_out_kernel(x_ref, h_ref, oa_ref, ob_ref, om_ref,
                      wg0_ref, wg1_ref, wg2_ref, b0_ref, b1_ref, b2_ref,
                      wa_ref, wb_ref, wm_ref, wo_ref, out_ref):
    n = pl.program_id(2)

    @pl.when(n == 0)
    def _():
        out_ref[0] = x_ref[0]

    k_chunk = h_ref.shape[3]

    def branch(o_ref, w_ref, wg_ref, b_ref):
        o = jnp.concatenate([o_ref[0, g] for g in range(o_ref.shape[1])], axis=-1)
        pre = None
        for c in range(h_ref.shape[1]):
            part = jnp.dot(h_ref[0, c], wg_ref[c * k_chunk:(c + 1) * k_chunk, :],
                           preferred_element_type=F32)
            pre = part if pre is None else pre + part
        return _sigmoid(pre + b_ref[...]) * jnp.dot(o, w_ref[...], preferred_element_type=F32)

    y = (branch(oa_ref, wa_ref, wg0_ref, b0_ref)
         + branch(ob_ref, wb_ref, wg1_ref, b1_ref)
         + branch(om_ref, wm_ref, wg2_ref, b2_ref))
    out_ref[0] += jnp.dot(y.astype(BF16), wo_ref[...], preferred_element_type=F32)


def _merge_out(x, h, oga, ogb, ogm, w_merge_bf, b_merge, wa_bf, wb_bf, wm_bf, wo_bf):
    B, S, D = x.shape
    n_c = D // CN_OUT

    def gate_w(r):
        return pl.BlockSpec((D, CN_OUT), lambda b, t, n: (0, r * n_c + n))

    def gate_b(r):
        return pl.BlockSpec((1, CN_OUT), lambda b, t, n: (0, r * n_c + n))

    def heads(nh):
        return pl.BlockSpec((1, nh, TM_OUT, HEAD_DIM), lambda b, t, n: (b, 0, t, 0))

    def cols(rows):
        return pl.BlockSpec((rows, CN_OUT), lambda b, t, n: (0, n))

    rows = pl.BlockSpec((1, TM_OUT, D), lambda b, t, n: (b, t, 0))
    return pl.pallas_call(
        _merge_out_kernel,
        grid=(B, S // TM_OUT, n_c),
        in_specs=[
            rows, pl.BlockSpec((1, h.shape[1], TM_OUT, h.shape[3]), lambda b, t, n: (b, 0, t, 0)),
            heads(oga.shape[1]), heads(ogb.shape[1]), heads(ogm.shape[1]),
            gate_w(0), gate_w(1), gate_w(2), gate_b(0), gate_b(1), gate_b(2),
            cols(wa_bf.shape[0]), cols(wb_bf.shape[0]), cols(wm_bf.shape[0]),
            pl.BlockSpec((CN_OUT, D), lambda b, t, n: (n, 0)),
        ],
        out_specs=rows,
        out_shape=jax.ShapeDtypeStruct((B, S, D), F32),
        compiler_params=_params(("parallel", "arbitrary", "arbitrary")),
        name="merge_out",
    )(x, h, oga, ogb, ogm, w_merge_bf, w_merge_bf, w_merge_bf,
      b_merge, b_merge, b_merge, wa_bf, wb_bf, wm_bf, wo_bf)


def kernel(x, mem, norm_gain, mem_norm_gain, w_in, w_merge, b_merge, a_q_gain, a_k_gain,
           a_rel_bias, pool_w, pool_scale, w_mem_kv, m_q_gain, m_k_gain,
           w_branch_a, w_branch_b, w_branch_m, w_out):
    depth = w_in.shape[0]
    scale = HEAD_DIM ** -0.5 * LOG2_E
    for l in range(depth):
        a_w, p_w, m_w = w_branch_a.shape[1], w_branch_b.shape[1], w_branch_m.shape[1]
        assert w_in.shape[2] == N_GROUPS * LANES_V7X and a_w == (G_KA - G_QA) * LANES_V7X
        assert p_w == (G_GB - G_VB) * LANES_V7X and m_w == (G_GM - G_QM) * LANES_V7X
        gain_all = jnp.concatenate([
            a_q_gain[l].reshape(1, a_w) * scale, a_k_gain[l].reshape(1, a_w),
            jnp.ones((1, 2 * a_w + 2 * p_w), F32),
            m_q_gain[l].reshape(1, m_w) * scale, jnp.ones((1, m_w), F32)], axis=1)

        p, vb, h, (wg_bf, wo_bf, wa_bf, wb_bf, wm_bf, wkv_bf) = _in_proj(
            x, norm_gain[l:l + 1], w_in[l].astype(BF16), gain_all,
            [w_merge[l], w_out[l], w_branch_a[l], w_branch_b[l], w_branch_m[l], w_mem_kv[l]])
        memkv = _mem_kv(mem, mem_norm_gain[l:l + 1], wkv_bf, m_k_gain[l].reshape(1, m_w))
        bias = _rel_bias_tiles(a_rel_bias[l])
        oga = _band_attn(p, bias)
        ogm = _mem_attn(p, memkv)
        ogb = _pool_mix(vb, p, pool_w[l].astype(BF16), pool_scale[l:l + 1])
        x = _merge_out(x, h, oga, ogb, ogm, wg_bf, b_merge[l:l + 1], wa_bf, wb_bf, wm_bf, wo_bf)
    return x
```
